```python
import math
import jax
import jax.numpy as jnp
from jax import lax
import numpy as np

D_MODEL = 2048
BATCH = 2
SEQ = 4096
DEPTH = 4
DEC_BATCH = 8
DEC_SEQ = 1
PAST_LEN = 16384
PAGE_SIZE = 128

HEAD_DIM = 128
D_FF = 4 * D_MODEL
N_MIXERS = 3
RMS_EPS = 1e-6
NEG_INF = -1e30
SOFTMAX_SCALE = HEAD_DIM ** -0.5

NUM_BUCKETS = 32
MAX_DISTANCE = 128
N_BIAS_HEADS = D_MODEL // HEAD_DIM

A_PATTERNS = ((128, 1), (512, 4), (2048, 16))
A_HEADS = D_MODEL // 256
A_IN = len(A_PATTERNS) * 3 * A_HEADS * HEAD_DIM

B_HEADS = D_MODEL // HEAD_DIM
B_KV_HEADS = B_HEADS // 4
B_BLOCK = 256
B_TOPK = 3
B_Q_CHUNK = 32
B_IN = (B_HEADS + 2 * B_KV_HEADS) * HEAD_DIM

C_HEADS = D_MODEL // HEAD_DIM
C_KV_HEADS = C_HEADS // 4
C_CMP_BLOCK = 32
C_CMP_STRIDE = 16
C_CMP_HIDDEN = HEAD_DIM
C_SLC_BLOCK = 64
C_SLC_TOPK = 16
C_WINDOW = 512
C_Q_CHUNK = 128
C_IN = (C_HEADS + 6 * C_KV_HEADS) * HEAD_DIM + 3 * C_HEADS

N_A_LAYERS = len(range(0, DEPTH, N_MIXERS))
N_B_LAYERS = len(range(1, DEPTH, N_MIXERS))
N_C_LAYERS = len(range(2, DEPTH, N_MIXERS))

kernel_name = 'hybrid_dilated_moba_nsa_decode_step'


def rmsnorm(x, g):
    xf = x.astype(jnp.float32)
    y = xf * lax.rsqrt(jnp.mean(xf * xf, axis=-1, keepdims=True) + RMS_EPS)
    return (y * g.astype(jnp.float32)).astype(x.dtype)


def t5_bucket(dist):
    exact = NUM_BUCKETS // 2
    d = jnp.maximum(dist, 0)
    far = exact + (jnp.log(jnp.maximum(d, exact).astype(jnp.float32) / exact)
                   / math.log(MAX_DISTANCE / exact) * (NUM_BUCKETS - exact)).astype(jnp.int32)
    return jnp.where(d < exact, d, jnp.minimum(far, NUM_BUCKETS - 1))


def masked_softmax(logits, mask):
    logits = jnp.where(mask, logits, NEG_INF)
    m = jnp.max(logits, axis=-1, keepdims=True)
    e = jnp.where(mask, jnp.exp(logits - m), 0.0)
    s = jnp.maximum(jnp.sum(e, axis=-1, keepdims=True), 1e-30)
    return e / s, (m + jnp.log(s))[..., 0]


def sq_relu_mlp(h, w_up, w_down):
    u = jnp.einsum('bsd,df->bsf', h, w_up)
    return jnp.einsum('bsf,fd->bsd', jnp.square(jax.nn.relu(u)), w_down)


def a_project(h, w_in):
    n, s, _ = h.shape
    return jnp.einsum('bsd,de->bse', h, w_in).reshape(n, s, len(A_PATTERNS), 3, A_HEADS, HEAD_DIM)


def a_band_prompt(q, k, v, window, dilation, bias_tab):
    b, s, h, dh = q.shape
    n = window // dilation
    length = s // dilation
    nb = -(-length // n)
    lp = nb * n

    def by_residue(x):
        x = x.reshape(b, length, dilation, h, dh).transpose(0, 2, 1, 3, 4)
        x = jnp.pad(x, ((0, 0), (0, 0), (0, lp - length), (0, 0), (0, 0)))
        return x.reshape(b, dilation, nb, n, h, dh)

    def with_prev(x):
        prev = jnp.pad(x[:, :, :-1], ((0, 0), (0, 0), (1, 0), (0, 0), (0, 0), (0, 0)))
        return jnp.concatenate([prev, x], axis=3)

    qb = by_residue(q)
    kk, vv = with_prev(by_residue(k)), with_prev(by_residue(v))
    step = n + jnp.arange(n)[:, None] - jnp.arange(2 * n)[None, :]
    blk = jnp.arange(nb)[:, None, None]
    mask = (step >= 0) & (step <= n) & ((blk > 0) | (jnp.arange(2 * n) >= n))
    bias = bias_tab[t5_bucket(step * dilation)].astype(jnp.float32).transpose(2, 0, 1)
    logits = jnp.einsum('brnqhd,brnkhd->brnhqk', qb, kk).astype(jnp.float32) * SOFTMAX_SCALE + bias[None, None, None]
    p, lse = masked_softmax(logits, mask[None, None, :, None])
    o = jnp.einsum('brnhqk,brnkhd->brnqhd', p.astype(v.dtype), vv)
    o = o.reshape(b, dilation, lp, h, dh)[:, :, :length].transpose(0, 2, 1, 3, 4).reshape(b, s, h, dh)
    lse = lse.transpose(0, 1, 2, 4, 3).reshape(b, dilation, lp, h)[:, :, :length].transpose(0, 2, 1, 3).reshape(b, s, h)
    return o, lse


def a_band_sample(q, k, v, buf, window, dilation, bias_tab):
    _, t, _, _ = q.shape
    wb = buf.shape[1]
    n = window // dilation
    kc = jnp.concatenate([buf[:, :, 0], k], axis=1)
    vc = jnp.concatenate([buf[:, :, 1], v], axis=1)
    steps = jnp.arange(n + 1)
    idx = wb + jnp.arange(t)[:, None] - steps[None, :] * dilation
    mask = idx >= 0
    idx = jnp.maximum(idx, 0)
    kg, vg = kc[:, idx], vc[:, idx]
    bias = bias_tab[t5_bucket(steps * dilation)].astype(jnp.float32).T
    logits = jnp.einsum('bthd,btkhd->bthk', q, kg).astype(jnp.float32) * SOFTMAX_SCALE + bias[None, None]
    p, lse = masked_softmax(logits, mask[None, :, None, :])
    o = jnp.einsum('bthk,btkhd->bthd', p.astype(v.dtype), vg)
    new_buf = jnp.stack([kc, vc], axis=2)[:, t:]
    return o, lse, new_buf


def a_combine(outs, lses, w_out):
    w = jax.nn.softmax(jnp.stack(lses, axis=0), axis=0)
    o = jnp.einsum('gbsh,gbshd->bshd', w, jnp.stack(outs, axis=0).astype(jnp.float32))
    return jnp.einsum('bshd,hde->bse', o.astype(w_out.dtype), w_out.reshape(A_HEADS, HEAD_DIM, -1))


def mixer_a(hp, hs, bufs, w_in, w_out, rel_bias):
    bias_tab = rel_bias[:, :A_HEADS]
    s = hp.shape[1]
    pp, ps = a_project(hp, w_in), a_project(hs, w_in)
    op, lp, osm, lsm, new_p, new_s = [], [], [], [], [], []
    for g, (window, dilation) in enumerate(A_PATTERNS):
        o, l = a_band_prompt(pp[:, :, g, 0], pp[:, :, g, 1], pp[:, :, g, 2], window, dilation, bias_tab)
        op.append(o)
        lp.append(l)
        new_p.append(jnp.stack([pp[:, :, g, 1], pp[:, :, g, 2]], axis=2)[:, s - min(window, s):])
        o, l, nbuf = a_band_sample(ps[:, :, g, 0], ps[:, :, g, 1], ps[:, :, g, 2], bufs[g], window, dilation, bias_tab)
        osm.append(o)
        lsm.append(l)
        new_s.append(nbuf)
    return a_combine(op, lp, w_out), a_combine(osm, lsm, w_out), new_p, new_s


def b_project(h, w_in):
    n, s, _ = h.shape
    proj = jnp.einsum('bsd,de->bse', h, w_in)
    q = proj[..., :B_HEADS * HEAD_DIM].reshape(n, s, B_HEADS, HEAD_DIM)
    kv = proj[..., B_HEADS * HEAD_DIM:].reshape(n, s, 2, B_KV_HEADS, HEAD_DIM)
    return q, kv


def moba_attend(q, q_pos, k, v, bias_tab):
    n, tq, h, dh = q.shape
    length, kvh = k.shape[1], k.shape[2]
    grp = h // kvh
    nblk = -(-length // B_BLOCK)
    pad = ((0, 0), (0, nblk * B_BLOCK - length), (0, 0), (0, 0))
    kb = jnp.pad(k, pad).reshape(n, nblk, B_BLOCK, kvh, dh).transpose(0, 3, 1, 2, 4)
    vb = jnp.pad(v, pad).reshape(n, nblk, B_BLOCK, kvh, dh).transpose(0, 3, 1, 2, 4)
    kv_of_head = jnp.arange(h) // grp
    kmean = jnp.mean(kb.astype(jnp.float32), axis=3)[:, kv_of_head]
    n_sel = min(B_TOPK, nblk)
    n_slot = n_sel + 1
    chunk = B_Q_CHUNK if tq % B_Q_CHUNK == 0 else tq
    n_chunks = tq // chunk
    q_chunks = q.reshape(n, n_chunks, chunk, h, dh).swapaxes(0, 1)
    pos_chunks = q_pos.reshape(n_chunks, chunk)
    bias_hb = bias_tab.T.astype(jnp.float32)
    b_ix = jnp.arange(n)[:, None, None, None]
    kv_ix = kv_of_head[None, None, :, None]
    h_ix = jnp.arange(h)[None, None, :, None, None]
    slot = jnp.arange(n_slot)

    def attend_chunk(args):
        qc, pc = args
        own = pc // B_BLOCK
        gate = jnp.einsum('bchd,bhnd->bchn', qc.astype(jnp.float32), kmean)
        past = jnp.arange(nblk)[None, :] < own[:, None]
        gate = jnp.where(past[None, :, None, :], gate, NEG_INF)
        _, sel = lax.top_k(gate, n_sel)
        own_b = jnp.broadcast_to(own[None, :, None, None], sel.shape[:3] + (1,))
        blocks = jnp.concatenate([sel, own_b], axis=-1)
        kg = kb[b_ix, kv_ix, blocks]
        vg = vb[b_ix, kv_ix, blocks]
        key_pos = blocks[..., None] * B_BLOCK + jnp.arange(B_BLOCK)
        slot_ok = (slot[None, :] < own[:, None]) | (slot[None, :] == n_sel)
        dist = pc[None, :, None, None, None] - key_pos
        mask = slot_ok[None, :, None, :, None] & (dist >= 0)
        bias = bias_hb[h_ix, t5_bucket(dist)]
        logits = jnp.einsum('bchd,bchskd->bchsk', qc, kg).astype(jnp.float32) * SOFTMAX_SCALE + bias
        nk = n_slot * B_BLOCK
        p, _ = masked_softmax(logits.reshape(n, chunk, h, nk), mask.reshape(n, chunk, h, nk))
        return jnp.einsum('bchk,bchkd->bchd', p.astype(v.dtype), vg.reshape(n, chunk, h, nk, dh))

    out = lax.map(attend_chunk, (q_chunks, pos_chunks))
    return out.swapaxes(0, 1).reshape(n, tq, h, dh)


def mixer_b(hp, hs, pool, page_table, layer, w_in, w_out, rel_bias):
    bias_tab = rel_bias[:, :B_HEADS]
    n_dec, t = hs.shape[:2]
    s = hp.shape[1]
    past_len = page_table.shape[1] * PAGE_SIZE
    w_o = w_out.reshape(B_HEADS, HEAD_DIM, -1)
    qp, kvp = b_project(hp, w_in)
    op = moba_attend(qp, jnp.arange(s), kvp[:, :, 0], kvp[:, :, 1], bias_tab)
    qs, kvs = b_project(hs, w_in)
    past = pool[page_table, :, layer].reshape(n_dec, past_len, 2, B_KV_HEADS, HEAD_DIM)
    full = jnp.concatenate([past, kvs], axis=1)
    osm = moba_attend(qs, past_len + jnp.arange(t), full[:, :, 0], full[:, :, 1], bias_tab)
    yp = jnp.einsum('bshd,hde->bse', op, w_o)
    ys = jnp.einsum('bshd,hde->bse', osm, w_o)
    return yp, ys, kvp, kvs


def c_project(h, w_in):
    n, s, _ = h.shape
    proj = jnp.einsum('bsd,de->bse', h, w_in)
    nq = C_HEADS * HEAD_DIM
    nkv = 6 * C_KV_HEADS * HEAD_DIM
    q = proj[..., :nq].reshape(n, s, C_HEADS, HEAD_DIM)
    kv = proj[..., nq:nq + nkv].reshape(n, s, 6, C_KV_HEADS, HEAD_DIM)
    gates = jax.nn.sigmoid(proj[..., nq + nkv:]).reshape(n, s, C_HEADS, 3)
    return q, kv, gates


def nsa_compress(x, pe, w1, w2):
    n, length, kvh, dh = x.shape
    r = C_CMP_BLOCK // C_CMP_STRIDE
    nseg = length // C_CMP_STRIDE
    nc = nseg - r + 1
    seg = x[:, :nseg * C_CMP_STRIDE].reshape(n, nseg, C_CMP_STRIDE, kvh, dh)
    w1r = w1.reshape(r, C_CMP_STRIDE, dh, C_CMP_HIDDEN)
    pre = jnp.einsum('msd,msde->e', pe.reshape(r, C_CMP_STRIDE, dh), w1r)
    for m in range(r):
        pre = pre + jnp.einsum('bnsgd,sde->bnge', seg[:, m:m + nc], w1r[m])
    return jnp.einsum('bnge,ed->bngd', jax.nn.relu(pre), w2)


def nsa_cmp_to_slc(nc, nsel):
    ratio = C_SLC_BLOCK // C_CMP_STRIDE
    i = jnp.arange(nc)
    return sum(jax.nn.one_hot((i + m) // ratio, nsel, dtype=jnp.float32) for m in range(C_CMP_BLOCK // C_CMP_STRIDE))


def nsa_attend(q, gates, q_pos, kc, vc, ks, vs, kw_ctx, vw_ctx, bias_tab):
    n, tq, h, dh = q.shape
    nc, kvh = kc.shape[1], kc.shape[2]
    grp = h // kvh
    length = ks.shape[1]
    nsel = -(-length // C_SLC_BLOCK)
    pad = ((0, 0), (0, nsel * C_SLC_BLOCK - length), (0, 0), (0, 0))
    ksb = jnp.pad(ks, pad).reshape(n, nsel, C_SLC_BLOCK, kvh, dh).transpose(0, 3, 1, 2, 4)
    vsb = jnp.pad(vs, pad).reshape(n, nsel, C_SLC_BLOCK, kvh, dh).transpose(0, 3, 1, 2, 4)
    n_top = min(C_SLC_TOPK, nsel)
    nk = n_top * C_SLC_BLOCK
    cmp_map = nsa_cmp_to_slc(nc, nsel)
    cmp_end = jnp.arange(nc) * C_CMP_STRIDE + (C_CMP_BLOCK - 1)
    bias_t = bias_tab.T.astype(jnp.float32).reshape(kvh, grp, NUM_BUCKETS)
    kv_ix3 = jnp.arange(kvh)[:, None, None]
    grp_ix = jnp.arange(grp)[:, None]
    b_ix = jnp.arange(n)[:, None, None, None]
    kv_ix = jnp.arange(kvh)[None, None, :, None]
    chunk = C_Q_CHUNK if tq % C_Q_CHUNK == 0 else tq
    n_chunks = tq // chunk
    q_chunks = q.reshape(n, n_chunks, chunk, kvh, grp, dh).swapaxes(0, 1)
    g_chunks = gates.reshape(n, n_chunks, chunk, kvh, grp, 3).swapaxes(0, 1)
    pos_chunks = q_pos.reshape(n_chunks, chunk)
    starts = jnp.arange(n_chunks) * chunk

    def attend_chunk(args):
        qc, gc, pc, start = args
        dist_c = pc[:, None] - cmp_end[None, :]
        bias_c = bias_t[:, :, t5_bucket(dist_c)].transpose(2, 0, 1, 3)
        lc = jnp.einsum('bcgjd,bngd->bcgjn', qc, kc).astype(jnp.float32) * SOFTMAX_SCALE + bias_c
        p_c, _ = masked_softmax(lc, (dist_c >= 0)[None, :, None, None, :])
        o_c = jnp.einsum('bcgjn,bngd->bcgjd', p_c.astype(vc.dtype), vc)
        imp = jnp.einsum('bcgjn,ns->bcgs', p_c, cmp_map)
        cur = pc // C_SLC_BLOCK
        blk = jnp.arange(nsel)[None, :]
        avail = blk <= cur[:, None]
        forced = (blk == 0) | (blk == cur[:, None]) | (blk == cur[:, None] - 1)
        score = jnp.where(forced[None, :, None, :], jnp.inf, imp)
        score = jnp.where(avail[None, :, None, :], score, -jnp.inf)
        _, sel = lax.top_k(score, n_top)
        kg = ksb[b_ix, kv_ix, sel].reshape(n, chunk, kvh, nk, dh)
        vg = vsb[b_ix, kv_ix, sel].reshape(n, chunk, kvh, nk, dh)
        key_pos = (sel[..., None] * C_SLC_BLOCK + jnp.arange(C_SLC_BLOCK)).reshape(n, chunk, kvh, nk)
        slot_ok = jnp.repeat(jnp.arange(n_top)[None, :] <= cur[:, None], C_SLC_BLOCK, axis=1)
        dist_s = pc[None, :, None, None] - key_pos
        mask_s = slot_ok[None, :, None, :] & (dist_s >= 0)
        bias_s = bias_t[kv_ix3, grp_ix, t5_bucket(dist_s)[:, :, :, None, :]]
        ls = jnp.einsum('bcgjd,bcgkd->bcgjk', qc, kg).astype(jnp.float32) * SOFTMAX_SCALE + bias_s
        p_s, _ = masked_softmax(ls, mask_s[:, :, :, None, :])
        o_s = jnp.einsum('bcgjk,bcgkd->bcgjd', p_s.astype(vg.dtype), vg)
        kw = lax.dynamic_slice_in_dim(kw_ctx, start, C_WINDOW + chunk, axis=1)
        vw = lax.dynamic_slice_in_dim(vw_ctx, start, C_WINDOW + chunk, axis=1)
        key_pos_w = pc[0] - C_WINDOW + jnp.arange(C_WINDOW + chunk)
        dist_w = pc[:, None] - key_pos_w[None, :]
        mask_w = (dist_w >= 0) & (dist_w <= C_WINDOW) & (key_pos_w[None, :] >= 0)
        bias_w = bias_t[:, :, t5_bucket(dist_w)].transpose(2, 0, 1, 3)
        lw = jnp.einsum('bcgjd,bkgd->bcgjk', qc, kw).astype(jnp.float32) * SOFTMAX_SCALE + bias_w
        p_w, _ = masked_softmax(lw, mask_w[None, :, None, None, :])
        o_w = jnp.einsum('bcgjk,bkgd->bcgjd', p_w.astype(vw.dtype), vw)
        return gc[..., 0:1] * o_c + gc[..., 1:2] * o_s + gc[..., 2:3] * o_w

    out = lax.map(attend_chunk, (q_chunks, g_chunks, pos_chunks, starts))
    return out.swapaxes(0, 1).reshape(n, tq, h, dh)


def mixer_c(hp, hs, pool, win_buf, page_table, layer, w_in, w_out, pe, w1, w2, rel_bias):
    bias_tab = rel_bias[:, :C_HEADS]
    n_dec, t = hs.shape[:2]
    s = hp.shape[1]
    past_len = page_table.shape[1] * PAGE_SIZE
    w_o = w_out.reshape(C_HEADS, HEAD_DIM, -1)

    def compress_kv(rows):
        return (nsa_compress(rows[:, :, 0], pe[0], w1[0], w2[0]),
                nsa_compress(rows[:, :, 1], pe[1], w1[1], w2[1]))

    qp, kvp, gp = c_project(hp, w_in)
    kcp, vcp = compress_kv(kvp)
    ctx_p = jnp.pad(kvp[:, :, 4:6], ((0, 0), (C_WINDOW, 0), (0, 0), (0, 0), (0, 0)))
    op = nsa_attend(qp, gp, jnp.arange(s), kcp, vcp, kvp[:, :, 2], kvp[:, :, 3], ctx_p[:, :, 0], ctx_p[:, :, 1], bias_tab)

    qs, kvs, gs = c_project(hs, w_in)
    past = pool[page_table, :, layer].reshape(n_dec, past_len, 4, C_KV_HEADS, HEAD_DIM)
    full = jnp.concatenate([past, kvs[:, :, :4]], axis=1)
    kcs, vcs = compress_kv(full)
    wb = win_buf.shape[1]
    ctx_s = jnp.concatenate([jnp.zeros((n_dec, C_WINDOW - wb) + win_buf.shape[2:], win_buf.dtype),
                             win_buf, kvs[:, :, 4:6]], axis=1)
    osm = nsa_attend(qs, gs, past_len + jnp.arange(t), kcs, vcs, full[:, :, 2], full[:, :, 3],
                     ctx_s[:, :, 0], ctx_s[:, :, 1], bias_tab)
    yp = jnp.einsum('bshd,hde->bse', op, w_o)
    ys = jnp.einsum('bshd,hde->bse', osm, w_o)
    win_p = kvp[:, s - min(C_WINDOW, s):, 4:6]
    win_s = jnp.concatenate([win_buf, kvs[:, :, 4:6]], axis=1)[:, t:]
    return yp, ys, kvp[:, :, :4], kvs[:, :, :4], win_p, win_s


def setup_inputs(seed: int = 0) -> dict:
    key = jax.random.key(seed)
    ks = jax.random.split(key, 24)

    def nrm(k, shape, scale):
        return jax.random.normal(k, shape, jnp.float32) * scale

    n_pages = PAST_LEN // PAGE_SIZE
    n_phys = (5 * DEC_BATCH * n_pages + 3) // 4
    page_table = jax.random.permutation(ks[0], n_phys)[:DEC_BATCH * n_pages].reshape(DEC_BATCH, n_pages).astype(jnp.int32)
    a_buf = lambda k, w: nrm(k, (N_A_LAYERS, DEC_BATCH, min(w, PAST_LEN), 2, A_HEADS, HEAD_DIM), 1.0)
    return {
        'x_prompt': nrm(ks[1], (BATCH, SEQ, D_MODEL), 1.0),
        'x_sample': nrm(ks[2], (DEC_BATCH, DEC_SEQ, D_MODEL), 1.0),
        'state_a_w128': a_buf(ks[3], A_PATTERNS[0][0]),
        'state_a_w512': a_buf(ks[4], A_PATTERNS[1][0]),
        'state_a_w2048': a_buf(ks[5], A_PATTERNS[2][0]),
        'cache_b_kv': nrm(ks[6], (n_phys, PAGE_SIZE, N_B_LAYERS, 2, B_KV_HEADS, HEAD_DIM), 1.0),
        'cache_c_kv': nrm(ks[7], (n_phys, PAGE_SIZE, N_C_LAYERS, 4, C_KV_HEADS, HEAD_DIM), 1.0),
        'state_c_win': nrm(ks[8], (N_C_LAYERS, DEC_BATCH, min(C_WINDOW, PAST_LEN), 2, C_KV_HEADS, HEAD_DIM), 1.0),
        'page_table': page_table,
        'rel_bias': nrm(ks[9], (NUM_BUCKETS, N_BIAS_HEADS), 0.5),
        'norm_mix': 1.0 + nrm(ks[10], (DEPTH, D_MODEL), 0.02),
        'norm_ffn': 1.0 + nrm(ks[11], (DEPTH, D_MODEL), 0.02),
        'norm_final': 1.0 + nrm(ks[12], (D_MODEL,), 0.02),
        'w_in_a': nrm(ks[13], (N_A_LAYERS, D_MODEL, A_IN), D_MODEL ** -0.5),
        'w_out_a': nrm(ks[14], (N_A_LAYERS, A_HEADS * HEAD_DIM, D_MODEL), (A_HEADS * HEAD_DIM) ** -0.5),
        'w_in_b': nrm(ks[15], (N_B_LAYERS, D_MODEL, B_IN), D_MODEL ** -0.5),
        'w_out_b': nrm(ks[16], (N_B_LAYERS, B_HEADS * HEAD_DIM, D_MODEL), (B_HEADS * HEAD_DIM) ** -0.5),
        'w_in_c': nrm(ks[17], (N_C_LAYERS, D_MODEL, C_IN), D_MODEL ** -0.5),
        'w_out_c': nrm(ks[18], (N_C_LAYERS, C_HEADS * HEAD_DIM, D_MODEL), (C_HEADS * HEAD_DIM) ** -0.5),
        'cmp_pe': nrm(ks[19], (N_C_LAYERS, 2, C_CMP_BLOCK, HEAD_DIM), 0.1),
        'cmp_w1': nrm(ks[20], (N_C_LAYERS, 2, C_CMP_BLOCK * HEAD_DIM, C_CMP_HIDDEN), (C_CMP_BLOCK * HEAD_DIM) ** -0.5),
        'cmp_w2': nrm(ks[21], (N_C_LAYERS, 2, C_CMP_HIDDEN, HEAD_DIM), (2.0 / C_CMP_HIDDEN) ** 0.5),
        'w_up': nrm(ks[22], (DEPTH, D_MODEL, D_FF), D_MODEL ** -0.5),
        'w_down': nrm(ks[23], (DEPTH, D_FF, D_MODEL), D_FF ** -0.5),
    }


def reference(x_prompt, x_sample, state_a_w128, state_a_w512, state_a_w2048, cache_b_kv, cache_c_kv, state_c_win,
              page_table, rel_bias, norm_mix, norm_ffn, norm_final, w_in_a, w_out_a, w_in_b, w_out_b,
              w_in_c, w_out_c, cmp_pe, cmp_w1, cmp_w2, w_up, w_down):
    a_bufs = (state_a_w128, state_a_w512, state_a_w2048)
    xp, xs = x_prompt, x_sample
    a_new_p = [[], [], []]
    a_new_s = [[], [], []]
    b_new_p, b_new_s, c_new_p, c_new_s, cw_new_p, cw_new_s = [], [], [], [], [], []
    for li in range(DEPTH):
        kind = li % N_MIXERS
        slot = li // N_MIXERS
        hp = rmsnorm(xp, norm_mix[li])
        hs = rmsnorm(xs, norm_mix[li])
        if kind == 0:
            mp, ms, new_p, new_s = mixer_a(hp, hs, [buf[slot] for buf in a_bufs], w_in_a[slot], w_out_a[slot], rel_bias)
            for g in range(len(A_PATTERNS)):
                a_new_p[g].append(new_p[g])
                a_new_s[g].append(new_s[g])
        elif kind == 1:
            mp, ms, kvp, kvs = mixer_b(hp, hs, cache_b_kv, page_table, slot, w_in_b[slot], w_out_b[slot], rel_bias)
            b_new_p.append(kvp)
            b_new_s.append(kvs)
        else:
            mp, ms, kvp, kvs, wp, ws = mixer_c(hp, hs, cache_c_kv, state_c_win[slot], page_table, slot,
                                               w_in_c[slot], w_out_c[slot], cmp_pe[slot], cmp_w1[slot],
                                               cmp_w2[slot], rel_bias)
            c_new_p.append(kvp)
            c_new_s.append(kvs)
            cw_new_p.append(wp)
            cw_new_s.append(ws)
        xp = xp + mp
        xs = xs + ms
        xp = xp + sq_relu_mlp(rmsnorm(xp, norm_ffn[li]), w_up[li], w_down[li])
        xs = xs + sq_relu_mlp(rmsnorm(xs, norm_ffn[li]), w_up[li], w_down[li])
    y_prompt = rmsnorm(xp, norm_final)
    y_sample = rmsnorm(xs, norm_final)
    return (y_prompt, y_sample,
            jnp.stack(a_new_p[0], axis=0), jnp.stack(a_new_s[0], axis=0),
            jnp.stack(a_new_p[1], axis=0), jnp.stack(a_new_s[1], axis=0),
            jnp.stack(a_new_p[2], axis=0), jnp.stack(a_new_s[2], axis=0),
            jnp.stack(b_new_p, axis=2), jnp.stack(b_new_s, axis=2),
            jnp.stack(c_new_p, axis=2), jnp.stack(c_new_s, axis=2),
            jnp.stack(cw_new_p, axis=0), jnp.stack(cw_new_s, axis=0))
```

```python
import functools
import math

import jax
import jax.numpy as jnp
import numpy as np
from jax import lax
from jax.experimental import pallas as pl
from jax.experimental.pallas import tpu as pltpu

F32 = jnp.float32
BF16 = jnp.bfloat16

HEAD_DIM = 128
RMS_EPS = 1e-6
NEG_INF = -1e30
SOFTMAX_SCALE = HEAD_DIM ** -0.5
PAGE_SIZE = 128

NUM_BUCKETS = 32
MAX_DISTANCE = 128

A_PATTERNS = ((128, 1), (512, 4), (2048, 16))
A_HEADS = 8
B_HEADS = 16
B_KV_HEADS = 4
B_BLOCK = 256
B_TOPK = 3
C_HEADS = 16
C_KV_HEADS = 4
C_CMP_BLOCK = 32
C_CMP_STRIDE = 16
C_SLC_BLOCK = 64
C_SLC_TOPK = 16
C_WINDOW = 512
GQA_GROUP = 4

ATTN_TILE = 256
VMEM_LIMIT_BYTES = 56 * 1024 * 1024


def _params(n_axes, vmem=VMEM_LIMIT_BYTES):
    return pltpu.CompilerParams(dimension_semantics=("arbitrary",) * n_axes, vmem_limit_bytes=vmem)


def _bucket_np(dist):
    exact = NUM_BUCKETS // 2
    d = np.maximum(np.asarray(dist, np.int64), 0)
    far = exact + (np.log(np.maximum(d, exact).astype(np.float64) / exact)
                   / math.log(MAX_DISTANCE / exact) * (NUM_BUCKETS - exact)).astype(np.int64)
    return np.where(d < exact, d, np.minimum(far, NUM_BUCKETS - 1)).astype(np.int32)


def _bias_by_dist(rel_bias, dist_np):
    tab = rel_bias[jnp.asarray(_bucket_np(dist_np))]
    return jnp.moveaxis(tab, -1, 0)


def _tile_bias_tables(rel_bias, t):
    i = np.arange(t)[:, None]
    j = np.arange(t)[None, :]
    dist = np.stack([np.maximum(i - j, 0), t + i - j])
    return _bias_by_dist(rel_bias, dist)


def _rms(x, g):
    return x * lax.rsqrt(jnp.mean(x * x, axis=-1, keepdims=True) + RMS_EPS) * g


def _split_bf16(x, parts):
    out = []
    for _ in range(parts):
        p = x.astype(BF16)
        out.append(p)
        x = x - p.astype(F32)
    return out


def _dot(a, b, dims=(((1,), (0,)), ((), ()))):
    return lax.dot_general(a, b, dims, preferred_element_type=F32)


_NT = (((1,), (1,)), ((), ()))
_TN = (((0,), (0,)), ((), ()))


def _dot_f32_lhs(a, b_bf16, dims=(((1,), (0,)), ((), ()))):
    return sum(_dot(p, b_bf16, dims) for p in _split_bf16(a, 3))


def _dot_f32_rhs(a_bf16, b, dims=(((1,), (0,)), ((), ()))):
    return sum(_dot(a_bf16, p, dims) for p in _split_bf16(b, 3))


def _dot_f32_both(a, b, dims):
    a_hi, a_lo = _split_bf16(a, 2)
    b_hi, b_lo = _split_bf16(b, 2)
    return _dot(a_hi, b_hi, dims) + (_dot(a_hi, b_lo, dims) + _dot(a_lo, b_hi, dims))


def _rank_lanes(score, n_lanes):
    lane = lax.broadcasted_iota(jnp.int32, score.shape, 1)
    rank = jnp.zeros(score.shape, jnp.int32)
    for m in range(n_lanes):
        sm = score[:, m:m + 1]
        beats = (sm > score) | ((sm == score) & (lane > m))
        rank = rank + beats.astype(jnp.int32)
    return rank


def _rank_rows(score, n_rows):
    row = lax.broadcasted_iota(jnp.int32, score.shape, 0)
    rank = jnp.zeros(score.shape, jnp.int32)
    for m in range(n_rows):
        sm = score[m:m + 1, :]
        beats = (sm > score) | ((sm == score) & (row > m))
        rank = rank + beats.astype(jnp.int32)
    return rank


def _norm_matmul_kernel(x_ref, g_ref, w_ref, o_ref, h_ref):
    @pl.when(pl.program_id(1) == 0)
    def _():
        h_ref[...] = _rms(x_ref[...], g_ref[...]).astype(BF16)

    o_ref[...] = _dot(h_ref[...], w_ref[...].astype(BF16))


def _norm_matmul(x, g_all, li, w_all, slot, n_cols, tm, tn):
    m, k = x.shape
    return pl.pallas_call(
        _norm_matmul_kernel,
        grid=(m // tm, n_cols // tn),
        in_specs=[
            pl.BlockSpec((tm, k), lambda i, j: (i, 0)),
            pl.BlockSpec((None, 1, k), lambda i, j: (li, 0, 0)),
            pl.BlockSpec((None, k, tn), lambda i, j: (slot, 0, j)),
        ],
        out_specs=pl.BlockSpec((tm, tn), lambda i, j: (i, j)),
        out_shape=jax.ShapeDtypeStruct((m, n_cols), F32),
        scratch_shapes=[pltpu.VMEM((tm, k), BF16)],
        compiler_params=_params(2),
        name="norm_matmul",
    )(x, g_all, w_all)


def _outproj_kernel(*refs, mode, n_heads):
    if mode == "plain":
        o_refs, extra, (w_ref, x_ref, out_ref, a_ref) = refs[:1], (), refs[1:]
    elif mode == "lse3":
        o_refs, extra, (w_ref, x_ref, out_ref, a_ref) = refs[:3], refs[3:6], refs[6:]
    else:
        o_refs, extra, (w_ref, x_ref, out_ref, a_ref) = refs[:3], refs[3:4], refs[4:]

    @pl.when(pl.program_id(1) == 0)
    def _():
        if mode == "plain":
            a = o_refs[0][...]
        elif mode == "lse3":
            lse = [r[...] for r in extra]
            mx = jnp.maximum(jnp.maximum(lse[0], lse[1]), lse[2])
            e = [jnp.exp(l - mx) for l in lse]
            den = e[0] + e[1] + e[2]
            a = sum((e[g] / den) * o_refs[g][...] for g in range(3))
        else:
            gate = jax.nn.sigmoid(extra[0][...])
            tm = gate.shape[0]
            heads = []
            for h in range(n_heads):
                acc = jnp.zeros((tm, HEAD_DIM), F32)
                for br in range(3):
                    gcol = gate[:, 3 * h + br:3 * h + br + 1]
                    acc = acc + gcol * o_refs[br][:, h * HEAD_DIM:(h + 1) * HEAD_DIM]
                heads.append(acc)
            a = jnp.concatenate(heads, axis=1)
        a_ref[...] = a.astype(BF16)

    out_ref[...] = x_ref[...] + _dot(a_ref[...], w_ref[...].astype(BF16))


def _outproj(o_list, extra_list, w_all, slot, x, mode, tm, tn):
    m, k = o_list[0].shape
    n = x.shape[1]
    row_spec = lambda width: pl.BlockSpec((tm, width), lambda i, j: (i, 0))
    in_specs = [row_spec(k) for _ in o_list] + [row_spec(e.shape[1]) for e in extra_list]
    in_specs += [pl.BlockSpec((None, k, tn), lambda i, j: (slot, 0, j)),
                 pl.BlockSpec((tm, tn), lambda i, j: (i, j))]
    return pl.pallas_call(
        functools.partial(_outproj_kernel, mode=mode, n_heads=k // HEAD_DIM),
        grid=(m // tm, n // tn),
        in_specs=in_specs,
        out_specs=pl.BlockSpec((tm, tn), lambda i, j: (i, j)),
        out_shape=jax.ShapeDtypeStruct((m, n), F32),
        scratch_shapes=[pltpu.VMEM((tm, k), BF16)],
        compiler_params=_params(2),
        name="outproj_" + mode,
    )(*o_list, *extra_list, w_all, x)


def _mlp_kernel(*refs, final):
    if final:
        x_ref, g_ref, wu_ref, wd_ref, gf_ref, o_ref, h_ref, acc_ref = refs
    else:
        x_ref, g_ref, wu_ref, wd_ref, o_ref, h_ref, acc_ref = refs
    f = pl.program_id(1)

    @pl.when(f == 0)
    def _():
        h_ref[...] = _rms(x_ref[...], g_ref[...]).astype(BF16)
        acc_ref[...] = jnp.zeros_like(acc_ref)

    u = _dot(h_ref[...], wu_ref[...].astype(BF16))
    act = jnp.square(jnp.maximum(u, 0.0)).astype(BF16)
    acc_ref[...] += _dot(act, wd_ref[...].astype(BF16))

    @pl.when(f == pl.num_programs(1) - 1)
    def _():
        y = x_ref[...] + acc_ref[...]
        o_ref[...] = _rms(y, gf_ref[...]) if final else y


def _mlp(x, g_all, li, w_up, w_down, final_g, tm, tf):
    m, d = x.shape
    d_ff = w_up.shape[2]
    in_specs = [
        pl.BlockSpec((tm, d), lambda i, f: (i, 0)),
        pl.BlockSpec((None, 1, d), lambda i, f: (li, 0, 0)),
        pl.BlockSpec((None, d, tf), lambda i, f: (li, 0, f)),
        pl.BlockSpec((None, tf, d), lambda i, f: (li, f, 0)),
    ]
    args = [x, g_all, w_up, w_down]
    if final_g is not None:
        in_specs.append(pl.BlockSpec((1, d), lambda i, f: (0, 0)))
        args.append(final_g)
    return pl.pallas_call(
        functools.partial(_mlp_kernel, final=final_g is not None),
        grid=(m // tm, d_ff // tf),
        in_specs=in_specs,
        out_specs=pl.BlockSpec((tm, d), lambda i, f: (i, 0)),
        out_shape=jax.ShapeDtypeStruct((m, d), F32),
        scratch_shapes=[pltpu.VMEM((tm, d), BF16), pltpu.VMEM((tm, d), F32)],
        compiler_params=_params(2),
        name="mlp",
    )(*args)


def _flash_kernel(*refs, group, t, window, dil, sel_blk, sel_per_head, want_lse, head0):
    it = iter(refs)
    q_ref, k_ref, v_ref, tab_ref, far_ref = (next(it) for _ in range(5))
    sel_ref = next(it) if sel_blk else None
    o_ref = next(it)
    lse_ref = next(it) if want_lse else None
    kb_ref, vb_ref, qs_ref, m_ref, l_ref, acc_ref = (next(it) for _ in range(6))

    kvh = pl.program_id(1)
    qi = pl.program_id(2)
    rows = group * t

    @pl.when(qi == 0)
    def _():
        kb_ref[...] = k_ref[0].astype(BF16)
        vb_ref[...] = v_ref[0].astype(BF16)

    for j in range(group):
        qs_ref[j * t:(j + 1) * t, :] = q_ref[0, :, j * HEAD_DIM:(j + 1) * HEAD_DIM].astype(BF16)
    m_ref[...] = jnp.full((rows, HEAD_DIM), NEG_INF, F32)
    l_ref[...] = jnp.zeros((rows, HEAD_DIM), F32)
    acc_ref[...] = jnp.zeros((rows, HEAD_DIM), F32)

    delta = (lax.broadcasted_iota(jnp.int32, (t, t), 0) - lax.broadcasted_iota(jnp.int32, (t, t), 1))

    def tile(kj, kind):
        start = pl.multiple_of(kj * t, t)
        k = kb_ref[pl.ds(start, t), :]
        v = vb_ref[pl.ds(start, t), :]
        s = _dot(qs_ref[...], k, _NT) * SOFTMAX_SCALE
        dist = delta + (qi - kj) * t
        ok = None
        if kind == "diag":
            ok = dist >= 0
        if window is not None:
            w_ok = dist <= window
            ok = w_ok if ok is None else ok & w_ok
        if dil > 1:
            d_ok = (dist & (dil - 1)) == 0
            ok = d_ok if ok is None else ok & d_ok
        sel_ok = None
        if sel_blk:
            blk_of_key = (start + lax.broadcasted_iota(jnp.int32, (HEAD_DIM, t), 1)) // sel_blk
            expand = (lax.broadcasted_iota(jnp.int32, (HEAD_DIM, t), 0) == blk_of_key)
            expand = jnp.where(expand, 1.0, 0.0).astype(BF16)
            if sel_per_head:
                sel_rows = jnp.concatenate([sel_ref[0, j] for j in range(group)], axis=0)
            else:
                sel_rows = sel_ref[0, 0]
            sel_ok = _dot(sel_rows.astype(BF16), expand) > 0.5
        pieces = []
        for j in range(group):
            sj = s[j * t:(j + 1) * t]
            if kind == "far":
                sj = sj + far_ref[head0 + kvh * group + j]
            else:
                sj = sj + tab_ref[j, 0 if kind == "diag" else 1]
            mask = ok
            if sel_blk:
                sel_j = sel_ok[j * t:(j + 1) * t] if sel_per_head else sel_ok
                mask = sel_j if mask is None else mask & sel_j
            pieces.append((sj, mask))
        if pieces[0][1] is not None:
            s = jnp.concatenate([jnp.where(mk, sj, NEG_INF) for sj, mk in pieces], axis=0)
            mask_all = jnp.concatenate([mk for _, mk in pieces], axis=0)
        else:
            s = jnp.concatenate([sj for sj, _ in pieces], axis=0)
            mask_all = None
        m_prev = m_ref[...]
        m_next = jnp.maximum(m_prev, jnp.max(s, axis=1, keepdims=True))
        p = jnp.exp(s - jnp.concatenate([m_next] * (t // HEAD_DIM), axis=1))
        if mask_all is not None:
            p = jnp.where(mask_all, p, 0.0)
        alpha = jnp.exp(m_prev - m_next)
        l_ref[...] = alpha * l_ref[...] + jnp.sum(p, axis=1, keepdims=True)
        acc_ref[...] = alpha * acc_ref[...] + _dot(p.astype(BF16), v)
        m_ref[...] = m_next

    lo = 0 if window is None else jnp.maximum(qi * t - window, 0) // t

    def far_body(kj, carry):
        tile(kj, "far")
        return carry

    lax.fori_loop(lo, jnp.maximum(qi - 1, lo), far_body, 0)

    @pl.when(qi - 1 >= lo)
    def _():
        tile(qi - 1, "near")

    tile(qi, "diag")

    l = l_ref[...]
    out = acc_ref[...] / l
    for j in range(group):
        o_ref[0, :, j * HEAD_DIM:(j + 1) * HEAD_DIM] = out[j * t:(j + 1) * t]
    if want_lse:
        lse = m_ref[...] + jnp.log(l)
        for j in range(group):
            lse_ref[0, :, j * HEAD_DIM:(j + 1) * HEAD_DIM] = lse[j * t:(j + 1) * t]


def _flash(proj, tabs, far, sel, *, n_kv, group, q_col, k_col, v_col, head0=0,
           window=None, dil=1, sel_blk=0, sel_per_head=False, want_lse=False):
    b, s, _ = proj.shape
    t = ATTN_TILE
    gw = group * HEAD_DIM
    in_specs = [
        pl.BlockSpec((1, t, gw), lambda bi, h, qi: (bi, qi, q_col // group + h)),
        pl.BlockSpec((1, s, HEAD_DIM), lambda bi, h, qi: (bi, 0, k_col + h)),
        pl.BlockSpec((1, s, HEAD_DIM), lambda bi, h, qi: (bi, 0, v_col + h)),
        pl.BlockSpec((group, 2, t, t), lambda bi, h, qi: (head0 // group + h, 0, 0, 0)),
        pl.BlockSpec(memory_space=pltpu.SMEM),
    ]
    args = [proj, proj, proj, tabs, far]
    if sel_blk:
        if sel_per_head:
            in_specs.append(pl.BlockSpec((1, group, t, HEAD_DIM), lambda bi, h, qi: (bi, h, qi, 0)))
        else:
            in_specs.append(pl.BlockSpec((1, 1, t, HEAD_DIM), lambda bi, h, qi: (bi, h, qi, 0)))
        args.append(sel)
    out_spec = pl.BlockSpec((1, t, gw), lambda bi, h, qi: (bi, qi, h))
    out_sds = jax.ShapeDtypeStruct((b, s, n_kv * gw), F32)
    rows = group * t
    return pl.pallas_call(
        functools.partial(_flash_kernel, group=group, t=t, window=window, dil=dil, sel_blk=sel_blk,
                          sel_per_head=sel_per_head, want_lse=want_lse, head0=head0),
        grid=(b, n_kv, s // t),
        in_specs=in_specs,
        out_specs=[out_spec, out_spec] if want_lse else out_spec,
        out_shape=[out_sds, out_sds] if want_lse else out_sds,
        scratch_shapes=[pltpu.VMEM((s, HEAD_DIM), BF16), pltpu.VMEM((s, HEAD_DIM), BF16),
                        pltpu.VMEM((rows, HEAD_DIM), BF16), pltpu.VMEM((rows, HEAD_DIM), F32),
                        pltpu.VMEM((rows, HEAD_DIM), F32), pltpu.VMEM((rows, HEAD_DIM), F32)],
        compiler_params=_params(3),
        name="flash",
    )(*args)


def _moba_select_kernel(q_ref, k_ref, sel_ref, kmean_ref, *, group, t, n_blk):
    qi = pl.program_id(2)

    @pl.when(qi == 0)
    def _():
        kmean_ref[...] = jnp.zeros_like(kmean_ref)
        s = k_ref.shape[1]
        km = jnp.sum(k_ref[0].reshape(n_blk, s // n_blk, HEAD_DIM), axis=1) * (1.0 / B_BLOCK)
        kmean_ref[0:n_blk, :] = km

    lane = lax.broadcasted_iota(jnp.int32, (t, HEAD_DIM), 1)
    pos = qi * t + lax.broadcasted_iota(jnp.int32, (t, HEAD_DIM), 0)
    own = pos // B_BLOCK
    past = lane < own
    kmean = kmean_ref[...]
    for j in range(group):
        q = q_ref[0, :, j * HEAD_DIM:(j + 1) * HEAD_DIM]
        gate = _dot_f32_both(q, kmean, _NT)
        gate = jnp.where(past, gate, NEG_INF)
        rank = _rank_lanes(gate, n_blk)
        sel = (past & (rank < B_TOPK)) | (lane == own)
        sel_ref[0, j] = jnp.where(sel, 1.0, 0.0)


def _moba_select(proj, *, n_kv, group, k_col):
    b, s, _ = proj.shape
    t = ATTN_TILE
    n_blk = s // B_BLOCK
    return pl.pallas_call(
        functools.partial(_moba_select_kernel, group=group, t=t, n_blk=n_blk),
        grid=(b, n_kv, s // t),
        in_specs=[pl.BlockSpec((1, t, group * HEAD_DIM), lambda bi, h, qi: (bi, qi, h)),
                  pl.BlockSpec((1, s, HEAD_DIM), lambda bi, h, qi: (bi, 0, k_col + h))],
        out_specs=pl.BlockSpec((1, group, t, HEAD_DIM), lambda bi, h, qi: (bi, h, qi, 0)),
        out_shape=jax.ShapeDtypeStruct((b, n_kv * group, s, HEAD_DIM), F32),
        scratch_shapes=[pltpu.VMEM((HEAD_DIM, HEAD_DIM), F32)],
        compiler_params=_params(3),
        name="moba_select",
    )(proj, proj)


def _cmp1_prompt_kernel(x_ref, w_ref, y_ref, *, n_seg):
    x2 = jnp.concatenate([x_ref[pl.ds(s, n_seg, stride=C_CMP_STRIDE), :] for s in range(C_CMP_STRIDE)], axis=1)
    y_ref[...] = _dot(x2.astype(BF16), w_ref[...])


def _cmp1_prompt(proj, w1cat, *, col0):
    b, s, _ = proj.shape
    n_seg = s // C_CMP_STRIDE
    return pl.pallas_call(
        functools.partial(_cmp1_prompt_kernel, n_seg=n_seg),
        grid=(b, 2, C_KV_HEADS),
        in_specs=[pl.BlockSpec((None, s, HEAD_DIM), lambda bi, c, g: (bi, 0, col0 + c * C_KV_HEADS + g)),
                  pl.BlockSpec((None, C_CMP_STRIDE * HEAD_DIM, 2 * HEAD_DIM), lambda bi, c, g: (c, 0, 0))],
        out_specs=pl.BlockSpec((None, n_seg, 2 * HEAD_DIM), lambda bi, c, g: ((bi * 2 + c) * C_KV_HEADS + g, 0, 0)),
        out_shape=jax.ShapeDtypeStruct((b * 2 * C_KV_HEADS, n_seg, 2 * HEAD_DIM), F32),
        compiler_params=_params(3),
        name="cmp1_prompt",
    )(proj, w1cat)


def _cmp2_kernel(y_ref, pe_ref, w1_ref, w2_ref, o_ref, *, n_seg):
    y = y_ref[0]
    first = y[:, :HEAD_DIM]
    second = pltpu.roll(y[:, HEAD_DIM:], n_seg - 1, 0)
    pe_term = _dot(pe_ref[0].astype(BF16), w1_ref[0].astype(BF16))[0:1]
    pre = first + second + pe_term
    tok = _dot(jnp.maximum(pre, 0.0).astype(BF16), w2_ref[0].astype(BF16))
    row = lax.broadcasted_iota(jnp.int32, tok.shape, 0)
    o_ref[0] = jnp.where(row < n_seg - 1, tok, 0.0)


def _cmp2(y, pe8, w1, w2):
    r, n_seg, _ = y.shape
    comp = lambda i: ((i // C_KV_HEADS) % 2, 0, 0)
    return pl.pallas_call(
        functools.partial(_cmp2_kernel, n_seg=n_seg),
        grid=(r,),
        in_specs=[pl.BlockSpec((1, n_seg, 2 * HEAD_DIM), lambda i: (i, 0, 0)),
                  pl.BlockSpec((1, 8, C_CMP_BLOCK * HEAD_DIM), comp),
                  pl.BlockSpec((1, C_CMP_BLOCK * HEAD_DIM, HEAD_DIM), comp),
                  pl.BlockSpec((1, HEAD_DIM, HEAD_DIM), comp)],
        out_specs=pl.BlockSpec((1, n_seg, HEAD_DIM), lambda i: (i, 0, 0)),
        out_shape=jax.ShapeDtypeStruct((r, n_seg, HEAD_DIM), F32),
        compiler_params=_params(1),
        name="cmp2",
    )(y, pe8, w1, w2)


def _nsa_cmp_select_kernel(q_ref, kc_ref, vc_ref, bias_ref, map_ref, oc_ref, sel_ref, *, group, t, n_cmp, n_sel):
    qi = pl.program_id(2)
    n_pad = kc_ref.shape[1]
    pos = qi * t + lax.broadcasted_iota(jnp.int32, (t, n_pad), 0)
    tok = lax.broadcasted_iota(jnp.int32, (t, n_pad), 1)
    valid = (pos - (tok * C_CMP_STRIDE + (C_CMP_BLOCK - 1)) >= 0) & (tok < n_cmp)
    kc = kc_ref[0].astype(BF16)
    vc = vc_ref[0].astype(BF16)
    p_sum = jnp.zeros((t, n_pad), F32)
    for j in range(group):
        q = q_ref[0, :, j * HEAD_DIM:(j + 1) * HEAD_DIM].astype(BF16)
        s = _dot(q, kc, _NT) * SOFTMAX_SCALE + bias_ref[j]
        s = jnp.where(valid, s, NEG_INF)
        m = jnp.max(s, axis=1, keepdims=True)
        e = jnp.where(valid, jnp.exp(s - m), 0.0)
        p = e / jnp.maximum(jnp.sum(e, axis=1, keepdims=True), 1e-30)
        oc_ref[0, :, j * HEAD_DIM:(j + 1) * HEAD_DIM] = _dot(p.astype(BF16), vc)
        p_sum = p_sum + p
    imp = _dot_f32_lhs(p_sum, map_ref[...])
    blk = lax.broadcasted_iota(jnp.int32, (t, HEAD_DIM), 1)
    cur = (qi * t + lax.broadcasted_iota(jnp.int32, (t, HEAD_DIM), 0)) // C_SLC_BLOCK
    avail = (blk <= cur) & (blk < n_sel)
    forced = (blk == 0) | (blk == cur) | (blk == cur - 1)
    score = jnp.where(forced, jnp.inf, imp)
    score = jnp.where(avail, score, -jnp.inf)
    rank = _rank_lanes(score, n_sel)
    sel = avail & (rank < min(C_SLC_TOPK, n_sel))
    sel_ref[0, 0] = jnp.where(sel, 1.0, 0.0)


def _cmp_map_np(n_rows, n_cmp, n_cols):
    ratio = C_SLC_BLOCK // C_CMP_STRIDE
    m = np.zeros((n_rows, n_cols), np.float32)
    for i in range(n_cmp):
        for r in range(C_CMP_BLOCK // C_CMP_STRIDE):
            m[i, (i + r) // ratio] += 1.0
    return m


def _nsa_cmp_select(proj, kvc, bias_c, *, n_kv, group):
    b, s, _ = proj.shape
    t = ATTN_TILE
    n_seg = kvc.shape[1]
    n_cmp = n_seg - 1
    n_sel = -(-s // C_SLC_BLOCK)
    cmap = jnp.asarray(_cmp_map_np(n_seg, n_cmp, HEAD_DIM), BF16)
    gw = group * HEAD_DIM
    return pl.pallas_call(
        functools.partial(_nsa_cmp_select_kernel, group=group, t=t, n_cmp=n_cmp, n_sel=n_sel),
        grid=(b, n_kv, s // t),
        in_specs=[pl.BlockSpec((1, t, gw), lambda bi, h, qi: (bi, qi, h)),
                  pl.BlockSpec((1, n_seg, HEAD_DIM), lambda bi, h, qi: (bi * 8 + h, 0, 0)),
                  pl.BlockSpec((1, n_seg, HEAD_DIM), lambda bi, h, qi: (bi * 8 + 4 + h, 0, 0)),
                  pl.BlockSpec((group, t, n_seg), lambda bi, h, qi: (h, qi, 0)),
                  pl.BlockSpec((n_seg, HEAD_DIM), lambda bi, h, qi: (0, 0))],
        out_specs=[pl.BlockSpec((1, t, gw), lambda bi, h, qi: (bi, qi, h)),
                   pl.BlockSpec((1, 1, t, HEAD_DIM), lambda bi, h, qi: (bi, h, qi, 0))],
        out_shape=[jax.ShapeDtypeStruct((b, s, n_kv * gw), F32),
                   jax.ShapeDtypeStruct((b, n_kv, s, HEAD_DIM), F32)],
        compiler_params=_params(3),
        name="nsa_cmp_select",
    )(proj, kvc, kvc, bias_c, cmap)


def _row_tile(m, pref):
    return pref if m % pref == 0 else m


def _mixer_a_prompt(x, b, g_mix, li, w_in, w_out, slot, tabs, far):
    m = x.shape[0]
    s = m // b
    n_in = w_in.shape[2]
    proj = _norm_matmul(x, g_mix, li, w_in, slot, n_in, _row_tile(m, 512), 512)
    proj3 = proj.reshape(b, s, n_in)
    outs, lses = [], []
    for g, (window, dil) in enumerate(A_PATTERNS):
        c0 = g * 3 * A_HEADS
        o, lse = _flash(proj3, tabs, far, None, n_kv=A_HEADS, group=1, q_col=c0, k_col=c0 + A_HEADS,
                        v_col=c0 + 2 * A_HEADS, window=window, dil=dil, want_lse=True)
        outs.append(o.reshape(m, -1))
        lses.append(lse.reshape(m, -1))
    x = _outproj(outs, lses, w_out, slot, x, "lse3", _row_tile(m, 512), 512)
    return x, proj3


def _mixer_b_prompt(x, b, g_mix, li, w_in, w_out, slot, tabs, far):
    m = x.shape[0]
    s = m // b
    n_in = w_in.shape[2]
    proj = _norm_matmul(x, g_mix, li, w_in, slot, n_in, _row_tile(m, 512), 512)
    proj3 = proj.reshape(b, s, n_in)
    sel = _moba_select(proj3, n_kv=B_KV_HEADS, group=GQA_GROUP, k_col=B_HEADS)
    o = _flash(proj3, tabs, far, sel, n_kv=B_KV_HEADS, group=GQA_GROUP, q_col=0, k_col=B_HEADS,
               v_col=B_HEADS + B_KV_HEADS, sel_blk=B_BLOCK, sel_per_head=True)
    x = _outproj([o.reshape(m, -1)], [], w_out, slot, x, "plain", _row_tile(m, 512), 512)
    return x, proj3


def _cmp_weights(cmp_pe, cmp_w1, cmp_w2, slot):
    half = C_CMP_STRIDE * HEAD_DIM
    w1 = cmp_w1[slot]
    w1cat = jnp.concatenate([w1[:, :half], w1[:, half:]], axis=2).astype(BF16)
    pe8 = jnp.broadcast_to(cmp_pe[slot].reshape(2, 1, C_CMP_BLOCK * HEAD_DIM), (2, 8, C_CMP_BLOCK * HEAD_DIM))
    return w1cat, pe8, w1, cmp_w2[slot]


def _mixer_c_prompt(x, b, g_mix, li, w_in, w_gate, w_out, slot, cmp_w, tabs, far, rel_bias):
    m = x.shape[0]
    s = m // b
    n_main = (C_HEADS + 6 * C_KV_HEADS) * HEAD_DIM
    proj = _norm_matmul(x, g_mix, li, w_in, slot, n_main, _row_tile(m, 512), 512)
    gate = _norm_matmul(x, g_mix, li, w_gate, 0, HEAD_DIM, _row_tile(m, 512), HEAD_DIM)
    proj3 = proj.reshape(b, s, n_main)
    w1cat, pe8, w1, w2 = cmp_w
    y = _cmp1_prompt(proj3, w1cat, col0=C_HEADS)
    kvc = _cmp2(y, pe8, w1, w2)
    n_seg = kvc.shape[1]
    dist_c = np.arange(s)[:, None] - (np.arange(n_seg)[None, :] * C_CMP_STRIDE + (C_CMP_BLOCK - 1))
    bias_c = _bias_by_dist(rel_bias, np.maximum(dist_c, 0))
    o_cmp, sel = _nsa_cmp_select(proj3, kvc, bias_c, n_kv=C_KV_HEADS, group=GQA_GROUP)
    kv0 = C_HEADS
    o_slc = _flash(proj3, tabs, far, sel, n_kv=C_KV_HEADS, group=GQA_GROUP, q_col=0, k_col=kv0 + 2 * C_KV_HEADS,
                   v_col=kv0 + 3 * C_KV_HEADS, sel_blk=C_SLC_BLOCK)
    o_win = _flash(proj3, tabs, far, None, n_kv=C_KV_HEADS, group=GQA_GROUP, q_col=0, k_col=kv0 + 4 * C_KV_HEADS,
                   v_col=kv0 + 5 * C_KV_HEADS, window=C_WINDOW)
    x = _outproj([o_cmp.reshape(m, -1), o_slc.reshape(m, -1), o_win.reshape(m, -1)], [gate], w_out, slot, x,
                 "gate3", _row_tile(m, 512), 512)
    return x, proj3


def _a_decode_kernel(qkv_ref, buf0_ref, buf1_ref, buf2_ref, bias_ref, o_ref):
    outs, lses = [], []
    for g, buf in enumerate((buf0_ref, buf1_ref, buf2_ref)):
        q, k_new, v_new = qkv_ref[0, g, 0], qkv_ref[0, g, 1], qkv_ref[0, g, 2]
        kb, vb = buf[:, 0], buf[:, 1]
        n = kb.shape[0]
        s = jnp.sum(q[None] * kb, axis=-1, keepdims=True) * SOFTMAX_SCALE + bias_ref[g, 0:n]
        s_new = jnp.sum(q * k_new, axis=-1, keepdims=True) * SOFTMAX_SCALE + bias_ref[g, n]
        m = jnp.maximum(jnp.max(s, axis=0), s_new)
        e = jnp.exp(s - m[None])
        e_new = jnp.exp(s_new - m)
        l = jnp.sum(e, axis=0) + e_new
        outs.append((jnp.sum(e * vb, axis=0) + e_new * v_new) / l)
        lses.append(m + jnp.log(l))
    mx = jnp.maximum(jnp.maximum(lses[0], lses[1]), lses[2])
    w = [jnp.exp(l - mx) for l in lses]
    o_ref[0] = (w[0] * outs[0] + w[1] * outs[1] + w[2] * outs[2]) / (w[0] + w[1] + w[2])


def _a_decode(qkv, bufs, slot, bias):
    n_dec = qkv.shape[0]
    in_specs = [pl.BlockSpec((1, 3, 3, A_HEADS, HEAD_DIM), lambda i: (i, 0, 0, 0, 0))]
    views = []
    for (window, dil), buf in zip(A_PATTERNS, bufs):
        n = window // dil
        assert buf.shape[2] == window, "window buffers must hold exactly `window` past rows"
        views.append(buf.reshape(buf.shape[0], n_dec, n, dil, 2, A_HEADS, HEAD_DIM))
        in_specs.append(pl.BlockSpec((None, None, n, None, 2, A_HEADS, HEAD_DIM),
                                     lambda i: (slot, i, 0, 0, 0, 0, 0)))
    in_specs.append(pl.BlockSpec(bias.shape, lambda i: (0, 0, 0, 0)))
    return pl.pallas_call(
        _a_decode_kernel,
        grid=(n_dec,),
        in_specs=in_specs,
        out_specs=pl.BlockSpec((1, A_HEADS, HEAD_DIM), lambda i: (i, 0, 0)),
        out_shape=jax.ShapeDtypeStruct((n_dec, A_HEADS, HEAD_DIM), F32),
        compiler_params=_params(1),
        name="a_decode",
    )(qkv, *views, bias)


DECODE_CHUNK = 2048


def _head_rows_store(o_ref, o_all, l_row):
    l_col = jnp.broadcast_to(l_row, (HEAD_DIM, HEAD_DIM)).T
    for g in range(B_KV_HEADS):
        rows = slice(g * GQA_GROUP, (g + 1) * GQA_GROUP)
        o_ref[0, rows, :] = o_all[rows, g * HEAD_DIM:(g + 1) * HEAD_DIM] / l_col[rows, :]


def _paged_decode_kernel(pt_ref, *refs, n_pg, n_sub, k_sub, v_sub, blk, topk, n_pages):
    page_refs = refs[:n_pg]
    qbd_ref, new_ref, bias_ref = refs[n_pg:n_pg + 3]
    rest = refs[n_pg + 3:]
    if topk is None:
        sel_ref, o_ref, st_ref, vs_ref, sel_s = rest
    else:
        o_ref, st_ref, vs_ref, sel_s = rest
    j = pl.program_id(1)
    past = n_pages * PAGE_SIZE
    n_blk = past // blk
    ch = DECODE_CHUNK

    if topk is not None:
        @pl.when(j == 0)
        def _():
            sel_s[...] = jnp.zeros_like(sel_s)

    for p in range(n_pg):
        page = j * n_pg + p
        row0 = pl.multiple_of(page * PAGE_SIZE, PAGE_SIZE)
        s_acc = jnp.zeros((PAGE_SIZE, HEAD_DIM), F32)
        for g in range(B_KV_HEADS):
            kg = page_refs[p][pl.ds(k_sub + g, PAGE_SIZE, stride=n_sub), :].astype(BF16)
            s_acc = s_acc + _dot(kg, qbd_ref[0, g])
            vg = page_refs[p][pl.ds(v_sub + g, PAGE_SIZE, stride=n_sub), :].astype(BF16)
            vs_ref[pl.ds(row0, PAGE_SIZE), g * HEAD_DIM:(g + 1) * HEAD_DIM] = vg
        st_ref[pl.ds(row0, PAGE_SIZE), :] = s_acc
        if topk is not None:
            b_idx = page // (blk // PAGE_SIZE)
            sel_s[pl.ds(b_idx, 1), :] = sel_s[pl.ds(b_idx, 1), :] + jnp.sum(s_acc, axis=0, keepdims=True)

    @pl.when(j == pl.num_programs(1) - 1)
    def _():
        s_new = jnp.zeros((8, HEAD_DIM), F32)
        for g in range(B_KV_HEADS):
            k_new = jnp.broadcast_to(new_ref[0, g:g + 1, :], (8, HEAD_DIM)).astype(BF16)
            s_new = s_new + _dot(k_new, qbd_ref[0, g])
            v_new = jnp.broadcast_to(new_ref[0, B_KV_HEADS + g:B_KV_HEADS + g + 1, :], (8, HEAD_DIM))
            vs_ref[past:past + 8, g * HEAD_DIM:(g + 1) * HEAD_DIM] = v_new.astype(BF16)
        l_new = s_new[0:1] * SOFTMAX_SCALE + bias_ref[ch + 1:ch + 2, :]

        if topk is not None:
            gate = sel_s[...] * (1.0 / blk)
            rank = _rank_rows(gate, n_blk)
            sel_s[...] = jnp.where(rank < topk, 1.0, 0.0)
        else:
            sel_s[...] = sel_ref[0, 0:n_blk, :]

        def masked_logits(c, bias):
            start = pl.multiple_of(c * ch, ch)
            lg = st_ref[pl.ds(start, ch), :] * SOFTMAX_SCALE + bias
            b0 = pl.multiple_of(c * (ch // blk), 8)
            sl = sel_s[pl.ds(b0, ch // blk), :]
            mk = jnp.broadcast_to(sl[:, None, :], (ch // blk, blk, HEAD_DIM)).reshape(ch, HEAD_DIM) > 0.5
            return jnp.where(mk, lg, NEG_INF), mk, start

        n_ch = past // ch
        far = bias_ref[ch:ch + 1, :]
        tail = bias_ref[0:ch, :]

        def max_body(c, m):
            lg, _, _ = masked_logits(c, far)
            return jnp.maximum(m, jnp.max(lg, axis=0, keepdims=True))

        m = lax.fori_loop(0, n_ch - 1, max_body, l_new)
        lg_t, mk_t, start_t = masked_logits(n_ch - 1, tail)
        m = jnp.maximum(m, jnp.max(lg_t, axis=0, keepdims=True))

        def acc_chunk(lg, mk, start, carry):
            l_sum, o_all = carry
            e = jnp.where(mk, jnp.exp(lg - m), 0.0)
            o_all = o_all + _dot(e.astype(BF16), vs_ref[pl.ds(start, ch), :], _TN)
            return l_sum + jnp.sum(e, axis=0, keepdims=True), o_all

        def acc_body(c, carry):
            lg, mk, start = masked_logits(c, far)
            return acc_chunk(lg, mk, start, carry)

        e_new = jnp.exp(l_new - m)
        row8 = lax.broadcasted_iota(jnp.int32, (8, HEAD_DIM), 0)
        e_new8 = jnp.where(row8 == 0, jnp.broadcast_to(e_new, (8, HEAD_DIM)), 0.0)
        o0 = _dot(e_new8.astype(BF16), vs_ref[past:past + 8, :], _TN)
        carry = lax.fori_loop(0, n_ch - 1, acc_body, (e_new, o0))
        l_sum, o_all = acc_chunk(lg_t, mk_t, start_t, carry)
        _head_rows_store(o_ref, o_all, l_sum)


def _paged_decode(page_table, pool2d, qbd, new_kv, bias, sel, *, n_pg, n_sub, k_sub, v_sub, blk, topk):
    n_dec, n_pages = page_table.shape
    past = n_pages * PAGE_SIZE
    assert past % DECODE_CHUNK == 0 and n_pages % n_pg == 0 and past % blk == 0
    n_blk = past // blk
    page_rows = PAGE_SIZE * n_sub
    in_specs = [pl.BlockSpec((page_rows, HEAD_DIM), functools.partial(lambda n, j, pt, p: (pt[n, j * n_pg + p], 0), p=p))
                for p in range(n_pg)]
    in_specs += [pl.BlockSpec((1, B_KV_HEADS, HEAD_DIM, HEAD_DIM), lambda n, j, pt: (n, 0, 0, 0)),
                 pl.BlockSpec((1, 2 * B_KV_HEADS, HEAD_DIM), lambda n, j, pt: (n, 0, 0)),
                 pl.BlockSpec(bias.shape, lambda n, j, pt: (0, 0))]
    args = [pool2d] * n_pg + [qbd, new_kv, bias]
    if topk is None:
        in_specs.append(pl.BlockSpec((1,) + sel.shape[1:], lambda n, j, pt: (n, 0, 0)))
        args.append(sel)
    return pl.pallas_call(
        functools.partial(_paged_decode_kernel, n_pg=n_pg, n_sub=n_sub, k_sub=k_sub, v_sub=v_sub, blk=blk,
                          topk=topk, n_pages=n_pages),
        grid_spec=pltpu.PrefetchScalarGridSpec(
            num_scalar_prefetch=1,
            grid=(n_dec, n_pages // n_pg),
            in_specs=in_specs,
            out_specs=pl.BlockSpec((1, B_HEADS, HEAD_DIM), lambda n, j, pt: (n, 0, 0)),
            scratch_shapes=[pltpu.VMEM((past + 8, HEAD_DIM), F32),
                            pltpu.VMEM((past + 8, B_KV_HEADS * HEAD_DIM), BF16),
                            pltpu.VMEM((n_blk, HEAD_DIM), F32)],
        ),
        out_shape=jax.ShapeDtypeStruct((n_dec, B_HEADS, HEAD_DIM), F32),
        compiler_params=_params(2),
        name="paged_decode",
    )(page_table, *args)


def _block_diag_q(q):
    n, h, d = q.shape
    qt = jnp.pad(jnp.swapaxes(q, 1, 2), ((0, 0), (0, 0), (0, HEAD_DIM - h)))
    grp = (np.arange(HEAD_DIM)[None, :] // GQA_GROUP == np.arange(B_KV_HEADS)[:, None]) & (np.arange(HEAD_DIM)[None, :] < h)
    return jnp.where(jnp.asarray(grp)[None, :, None, :], qt[:, None], 0.0).astype(BF16)


def _decode_bias(rel_bias, past, n_tail):
    dist = np.concatenate([n_tail - np.arange(n_tail), [MAX_DISTANCE * 8, 0], np.zeros(6, np.int64)])
    tab = rel_bias[jnp.asarray(_bucket_np(dist))]
    return jnp.pad(tab, ((0, 0), (0, HEAD_DIM - tab.shape[1])))


def _cmp1_paged_kernel(pt_ref, *refs, n_pg):
    page_refs = refs[:n_pg]
    w_ref, y_ref, ys_ref = refs[n_pg:]
    segs_per_page = PAGE_SIZE // C_CMP_STRIDE
    rows = segs_per_page * 8
    x2 = jnp.concatenate(
        [jnp.concatenate([ref[:, s, :, :].reshape(rows, HEAD_DIM) for s in range(C_CMP_STRIDE)], axis=1)
         for ref in page_refs], axis=0)
    y4 = _dot(x2.astype(BF16), w_ref[...])
    is_v = (lax.broadcasted_iota(jnp.int32, (x2.shape[0], 2 * HEAD_DIM), 0) % 8) >= C_KV_HEADS
    y = jnp.where(is_v, y4[:, 2 * HEAD_DIM:], y4[:, :2 * HEAD_DIM])
    ys_ref[0] = y[:, :HEAD_DIM]
    ys_ref[1] = y[:, HEAD_DIM:]
    for sub in range(8):
        y_ref[sub] = jnp.concatenate(
            [ys_ref[half, pl.ds(sub, n_pg * segs_per_page, stride=8), :] for half in range(2)], axis=1)


def _cmp1_paged(page_table, pool4d, w1cat4, *, sub_blk, n_pg):
    n_dec, n_pages = page_table.shape
    segs_per_page = PAGE_SIZE // C_CMP_STRIDE
    n_seg = n_pages * segs_per_page
    in_specs = [pl.BlockSpec((segs_per_page, C_CMP_STRIDE, 8, HEAD_DIM),
                             functools.partial(lambda n, j, pt, p: (pt[n, j * n_pg + p], 0, sub_blk, 0), p=p))
                for p in range(n_pg)]
    in_specs.append(pl.BlockSpec(w1cat4.shape, lambda n, j, pt: (0, 0)))
    return pl.pallas_call(
        functools.partial(_cmp1_paged_kernel, n_pg=n_pg),
        grid_spec=pltpu.PrefetchScalarGridSpec(
            num_scalar_prefetch=1,
            grid=(n_dec, n_pages // n_pg),
            in_specs=in_specs,
            out_specs=pl.BlockSpec((8, n_pg * segs_per_page, 2 * HEAD_DIM), lambda n, j, pt: (n, j, 0)),
            scratch_shapes=[pltpu.VMEM((2, n_pg * segs_per_page * 8, HEAD_DIM), F32)],
        ),
        out_shape=jax.ShapeDtypeStruct((n_dec * 8, n_seg, 2 * HEAD_DIM), F32),
        compiler_params=_params(2),
        name="cmp1_paged",
    )(page_table, *([pool4d] * n_pg), w1cat4)


def _nsa_decode_kernel(kvc_ref, qbd_ref, bias_c_ref, map_ref, gsum_ref, win_ref, new_ref, bias_w_ref,
                       oc_ref, ow_ref, sel_ref, *, n_cmp, n_blk, n_sub_w, k_sub_w):
    n_seg = kvc_ref.shape[1]
    sc = jnp.zeros((n_seg, HEAD_DIM), F32)
    for g in range(C_KV_HEADS):
        sc = sc + _dot(kvc_ref[g].astype(BF16), qbd_ref[0, g])
    valid = lax.broadcasted_iota(jnp.int32, (n_seg, HEAD_DIM), 0) < n_cmp
    lg = jnp.where(valid, sc * SOFTMAX_SCALE + bias_c_ref[...], NEG_INF)
    m = jnp.max(lg, axis=0, keepdims=True)
    e = jnp.where(valid, jnp.exp(lg - m), 0.0)
    p = e / jnp.maximum(jnp.sum(e, axis=0, keepdims=True), 1e-30)
    vc = jnp.concatenate([kvc_ref[C_KV_HEADS + g].astype(BF16) for g in range(C_KV_HEADS)], axis=1)
    _head_rows_store(oc_ref, _dot(p.astype(BF16), vc, _TN), jnp.ones((1, HEAD_DIM), F32))
    imp_h = _dot_f32_rhs(map_ref[...], p)
    imp = _dot_f32_lhs(imp_h, gsum_ref[...])
    blk = lax.broadcasted_iota(jnp.int32, imp.shape, 0)
    cur = n_blk - 1
    avail = blk <= cur
    forced = (blk == 0) | (blk == cur) | (blk == cur - 1)
    score = jnp.where(forced, jnp.inf, imp)
    score = jnp.where(avail, score, -jnp.inf)
    rank = _rank_rows(score, n_blk)
    sel_ref[0] = jnp.where(avail & (rank < min(C_SLC_TOPK, n_blk)), 1.0, 0.0)
    n_win = bias_w_ref.shape[0] - 8
    sw = jnp.zeros((n_win, HEAD_DIM), F32)
    s_new = jnp.zeros((8, HEAD_DIM), F32)
    v_parts, vn_parts = [], []
    for g in range(C_KV_HEADS):
        kg = win_ref[pl.ds(k_sub_w + g, n_win, stride=n_sub_w), :].astype(BF16)
        sw = sw + _dot(kg, qbd_ref[0, g])
        v_parts.append(win_ref[pl.ds(k_sub_w + C_KV_HEADS + g, n_win, stride=n_sub_w), :].astype(BF16))
        k_new = jnp.broadcast_to(new_ref[0, g:g + 1, :], (8, HEAD_DIM)).astype(BF16)
        s_new = s_new + _dot(k_new, qbd_ref[0, g])
        vn_parts.append(jnp.broadcast_to(new_ref[0, C_KV_HEADS + g:C_KV_HEADS + g + 1, :], (8, HEAD_DIM)).astype(BF16))
    lw = sw * SOFTMAX_SCALE + bias_w_ref[0:n_win, :]
    l_new = s_new[0:1] * SOFTMAX_SCALE + bias_w_ref[n_win + 1:n_win + 2, :]
    mw = jnp.maximum(jnp.max(lw, axis=0, keepdims=True), l_new)
    ew = jnp.exp(lw - mw)
    e_new = jnp.exp(l_new - mw)
    row8 = lax.broadcasted_iota(jnp.int32, (8, HEAD_DIM), 0)
    e_new8 = jnp.where(row8 == 0, jnp.broadcast_to(e_new, (8, HEAD_DIM)), 0.0)
    o_all = _dot(ew.astype(BF16), jnp.concatenate(v_parts, axis=1), _TN)
    o_all = o_all + _dot(e_new8.astype(BF16), jnp.concatenate(vn_parts, axis=1), _TN)
    _head_rows_store(ow_ref, o_all, jnp.sum(ew, axis=0, keepdims=True) + e_new)


def _nsa_decode(kvc, qbd, bias_c, cmap_t, gsum, win2d, new_win, bias_w, *, slot, n_blk):
    n_dec = qbd.shape[0]
    n_seg = kvc.shape[1]
    n_win = bias_w.shape[0] - 8
    n_blk_pad = cmap_t.shape[0]
    win_rows = n_win * 8
    o_sds = jax.ShapeDtypeStruct((n_dec, C_HEADS, HEAD_DIM), F32)
    o_spec = pl.BlockSpec((1, C_HEADS, HEAD_DIM), lambda n: (n, 0, 0))
    return pl.pallas_call(
        functools.partial(_nsa_decode_kernel, n_cmp=n_seg - 1, n_blk=n_blk, n_sub_w=8, k_sub_w=0),
        grid=(n_dec,),
        in_specs=[pl.BlockSpec((8, n_seg, HEAD_DIM), lambda n: (n, 0, 0)),
                  pl.BlockSpec((1, C_KV_HEADS, HEAD_DIM, HEAD_DIM), lambda n: (n, 0, 0, 0)),
                  pl.BlockSpec(bias_c.shape, lambda n: (0, 0)),
                  pl.BlockSpec(cmap_t.shape, lambda n: (0, 0)),
                  pl.BlockSpec(gsum.shape, lambda n: (0, 0)),
                  pl.BlockSpec((win_rows, HEAD_DIM), lambda n: (slot * n_dec + n, 0)),
                  pl.BlockSpec((1, 2 * C_KV_HEADS, HEAD_DIM), lambda n: (n, 0, 0)),
                  pl.BlockSpec(bias_w.shape, lambda n: (0, 0))],
        out_specs=[o_spec, o_spec, pl.BlockSpec((1, n_blk_pad, HEAD_DIM), lambda n: (n, 0, 0))],
        out_shape=[o_sds, o_sds, jax.ShapeDtypeStruct((n_dec, n_blk_pad, HEAD_DIM), F32)],
        compiler_params=_params(1),
        name="nsa_decode",
    )(kvc, qbd, bias_c, cmap_t, gsum, win2d, new_win, bias_w)


def _mixer_a_decode(x, g_mix, li, w_in, w_out, slot, bufs, rel_bias):
    n_dec = x.shape[0]
    n_in = w_in.shape[2]
    proj = _norm_matmul(x, g_mix, li, w_in, slot, n_in, n_dec, 512)
    qkv = proj.reshape(n_dec, len(A_PATTERNS), 3, A_HEADS, HEAD_DIM)
    dist = np.stack([np.concatenate([(w // d - np.arange(w // d)) * d, [0]]) for w, d in A_PATTERNS])
    bias = _bias_by_dist(rel_bias[:, :A_HEADS], dist)
    bias = jnp.broadcast_to(jnp.transpose(bias, (1, 2, 0))[..., None], dist.shape + (A_HEADS, HEAD_DIM))
    o = _a_decode(qkv, bufs, slot, bias)
    x = _outproj([o.reshape(n_dec, -1)], [], w_out, slot, x, "plain", n_dec, 512)
    return x, proj


def _mixer_b_decode(x, g_mix, li, w_in, w_out, slot, pool, page_table, bias_dec):
    n_dec = x.shape[0]
    n_in = w_in.shape[2]
    proj = _norm_matmul(x, g_mix, li, w_in, slot, n_in, n_dec, 512)
    nq = B_HEADS * HEAD_DIM
    qbd = _block_diag_q(proj[:, :nq].reshape(n_dec, B_HEADS, HEAD_DIM))
    new_kv = proj[:, nq:].reshape(n_dec, 2 * B_KV_HEADS, HEAD_DIM)
    n_sub = pool.shape[2] * 2 * B_KV_HEADS
    o = _paged_decode(page_table, pool.reshape(-1, HEAD_DIM), qbd, new_kv, bias_dec, None, n_pg=8, n_sub=n_sub,
                      k_sub=slot * 2 * B_KV_HEADS, v_sub=slot * 2 * B_KV_HEADS + B_KV_HEADS, blk=B_BLOCK, topk=B_TOPK)
    x = _outproj([o.reshape(n_dec, -1)], [], w_out, slot, x, "plain", n_dec, 512)
    return x, proj


def _mixer_c_decode(x, g_mix, li, w_in, w_gate, w_out, slot, cmp_w, pool, win_buf, page_table, bias_dec, rel_bias):
    n_dec = x.shape[0]
    n_pages = page_table.shape[1]
    past = n_pages * PAGE_SIZE
    assert win_buf.shape[2] == C_WINDOW, "window buffer must hold exactly C_WINDOW past rows"
    n_main = (C_HEADS + 6 * C_KV_HEADS) * HEAD_DIM
    proj = _norm_matmul(x, g_mix, li, w_in, slot, n_main, n_dec, 512)
    gate = _norm_matmul(x, g_mix, li, w_gate, 0, HEAD_DIM, n_dec, HEAD_DIM)
    nq = C_HEADS * HEAD_DIM
    kvw = C_KV_HEADS * HEAD_DIM
    qbd = _block_diag_q(proj[:, :nq].reshape(n_dec, C_HEADS, HEAD_DIM))
    new_slc = proj[:, nq + 2 * kvw:nq + 4 * kvw].reshape(n_dec, 2 * C_KV_HEADS, HEAD_DIM)
    new_win = proj[:, nq + 4 * kvw:nq + 6 * kvw].reshape(n_dec, 2 * C_KV_HEADS, HEAD_DIM)
    w1cat, pe8, w1, w2 = cmp_w
    w1cat4 = jnp.concatenate([w1cat[0], w1cat[1]], axis=1)
    n_layers = pool.shape[2]
    n_sub = n_layers * 4 * C_KV_HEADS
    pool4d = pool.reshape(-1, C_CMP_STRIDE, n_sub, HEAD_DIM)
    y = _cmp1_paged(page_table, pool4d, w1cat4, sub_blk=slot * 2, n_pg=8)
    kvc = _cmp2(y, pe8, w1, w2)
    n_seg = kvc.shape[1]
    n_cmp = n_seg - 1
    n_blk = past // C_SLC_BLOCK + 1
    n_blk_pad = -(-n_blk // 8) * 8
    dist_c = past - (np.arange(n_seg) * C_CMP_STRIDE + (C_CMP_BLOCK - 1))
    bias_c = jnp.pad(rel_bias[jnp.asarray(_bucket_np(dist_c))], ((0, 0), (0, HEAD_DIM - C_HEADS)))
    cmap_t = jnp.asarray(_cmp_map_np(n_seg, n_cmp, n_blk_pad).T, BF16)
    lanes = np.arange(HEAD_DIM)
    gsum = jnp.asarray((lanes[:, None] // GQA_GROUP == lanes[None, :] // GQA_GROUP)
                       & (lanes[:, None] < C_HEADS) & (lanes[None, :] < C_HEADS), BF16)
    bias_w = _decode_bias(rel_bias, past, C_WINDOW)
    o_cmp, o_win, sel = _nsa_decode(kvc, qbd, bias_c, cmap_t, gsum, win_buf.reshape(-1, HEAD_DIM), new_win, bias_w,
                                    slot=slot, n_blk=n_blk)
    o_slc = _paged_decode(page_table, pool.reshape(-1, HEAD_DIM), qbd, new_slc, bias_dec, sel, n_pg=4, n_sub=n_sub,
                          k_sub=slot * 16 + 2 * C_KV_HEADS, v_sub=slot * 16 + 3 * C_KV_HEADS, blk=C_SLC_BLOCK, topk=None)
    x = _outproj([o_cmp.reshape(n_dec, -1), o_slc.reshape(n_dec, -1), o_win.reshape(n_dec, -1)], [gate], w_out, slot, x,
                 "gate3", n_dec, 512)
    return x, proj


def _shifted(buf, new_rows):
    return jnp.concatenate([buf[:, :, 1:], new_rows[:, :, None]], axis=2)


def kernel(x_prompt, x_sample, state_a_w128, state_a_w512, state_a_w2048, cache_b_kv, cache_c_kv, state_c_win, page_table, rel_bias, norm_mix, norm_ffn, norm_final, w_in_a, w_out_a, w_in_b, w_out_b, w_in_c, w_out_c, cmp_pe, cmp_w1, cmp_w2, w_up, w_down):
    b, s, d = x_prompt.shape
    n_dec, t_dec, _ = x_sample.shape
    depth = norm_mix.shape[0]
    assert t_dec == 1 and s % B_BLOCK == 0 and s % ATTN_TILE == 0
    past = page_table.shape[1] * PAGE_SIZE
    assert past % B_BLOCK == 0 and past % C_SLC_BLOCK == 0

    xp = x_prompt.reshape(b * s, d)
    xs = x_sample.reshape(n_dec, d)
    g_mix = norm_mix.reshape(depth, 1, d)
    g_ffn = norm_ffn.reshape(depth, 1, d)
    g_fin = norm_final.reshape(1, d)
    tabs = _tile_bias_tables(rel_bias, ATTN_TILE)
    far = rel_bias[NUM_BUCKETS - 1]
    bias_dec = _decode_bias(rel_bias, past, DECODE_CHUNK)
    n_gate = 3 * C_HEADS
    n_main_c = (C_HEADS + 6 * C_KV_HEADS) * HEAD_DIM
    w_gate = jnp.pad(w_in_c[:, :, n_main_c:], ((0, 0), (0, 0), (0, HEAD_DIM - n_gate)))
    a_bufs = (state_a_w128, state_a_w512, state_a_w2048)

    a_p, a_s = [[], [], []], [[], [], []]
    b_p, b_s, c_p, c_s, cw_p, cw_s = [], [], [], [], [], []
    for li in range(depth):
        kind, slot = li % 3, li // 3
        if kind == 0:
            xp, pp = _mixer_a_prompt(xp, b, g_mix, li, w_in_a, w_out_a, slot, tabs, far)
            xs, ps = _mixer_a_decode(xs, g_mix, li, w_in_a, w_out_a, slot, a_bufs, rel_bias)
            gw = 3 * A_HEADS * HEAD_DIM
            kvw = 2 * A_HEADS * HEAD_DIM
            for g, (window, _) in enumerate(A_PATTERNS):
                c0 = g * gw + A_HEADS * HEAD_DIM
                a_p[g].append(pp[:, s - min(window, s):, c0:c0 + kvw].reshape(b, min(window, s), 2, A_HEADS, HEAD_DIM))
                a_s[g].append(ps[:, c0:c0 + kvw].reshape(n_dec, 2, A_HEADS, HEAD_DIM))
        elif kind == 1:
            xp, pp = _mixer_b_prompt(xp, b, g_mix, li, w_in_b, w_out_b, slot, tabs, far)
            xs, ps = _mixer_b_decode(xs, g_mix, li, w_in_b, w_out_b, slot, cache_b_kv, page_table, bias_dec)
            nq = B_HEADS * HEAD_DIM
            b_p.append(pp[:, :, nq:].reshape(b, s, 2, B_KV_HEADS, HEAD_DIM))
            b_s.append(ps[:, nq:].reshape(n_dec, 1, 2, B_KV_HEADS, HEAD_DIM))
        else:
            cmp_w = _cmp_weights(cmp_pe, cmp_w1, cmp_w2, slot)
            xp, pp = _mixer_c_prompt(xp, b, g_mix, li, w_in_c, w_gate[slot:slot + 1], w_out_c, slot, cmp_w, tabs, far, rel_bias)
            xs, ps = _mixer_c_decode(xs, g_mix, li, w_in_c, w_gate[slot:slot + 1], w_out_c, slot, cmp_w, cache_c_kv,
                                     state_c_win, page_table, bias_dec, rel_bias)
            nq = C_HEADS * HEAD_DIM
            kvw = C_KV_HEADS * HEAD_DIM
            c_p.append(pp[:, :, nq:nq + 4 * kvw].reshape(b, s, 4, C_KV_HEADS, HEAD_DIM))
            c_s.append(ps[:, nq:nq + 4 * kvw].reshape(n_dec, 1, 4, C_KV_HEADS, HEAD_DIM))
            wlen = min(C_WINDOW, s)
            cw_p.append(pp[:, s - wlen:, nq + 4 * kvw:nq + 6 * kvw].reshape(b, wlen, 2, C_KV_HEADS, HEAD_DIM))
            cw_s.append(ps[:, nq + 4 * kvw:nq + 6 * kvw].reshape(n_dec, 2, C_KV_HEADS, HEAD_DIM))
        last = li == depth - 1
        xp = _mlp(xp, g_ffn, li, w_up, w_down, g_fin if last else None, _row_tile(b * s, 512), 512)
        xs = _mlp(xs, g_ffn, li, w_up, w_down, g_fin if last else None, n_dec, 512)

    outs = [xp.reshape(b, s, d), xs.reshape(n_dec, 1, d)]
    for g in range(len(A_PATTERNS)):
        outs.append(jnp.stack(a_p[g], axis=0))
        outs.append(_shifted(a_bufs[g], jnp.stack(a_s[g], axis=0)))
    outs += [jnp.stack(b_p, axis=2), jnp.stack(b_s, axis=2), jnp.stack(c_p, axis=2), jnp.stack(c_s, axis=2),
             jnp.stack(cw_p, axis=0), _shifted(state_c_win, jnp.stack(cw_s, axis=0))]
    return tuple(outs)
```

```python
import functools
import math

import jax
import jax.numpy as jnp
import numpy as np
from jax import lax
from jax.experimental import pallas as pl
from jax.experimental.pallas import tpu as pltpu

F32 = jnp.float32
BF16 = jnp.bfloat16

HEAD_DIM = 128
RMS_EPS = 1e-6
NEG_INF = -1e30
SOFTMAX_SCALE = HEAD_DIM ** -0.5
PAGE_SIZE = 128

NUM_BUCKETS = 32
MAX_DISTANCE = 128

A_PATTERNS = ((128, 1), (512, 4), (2048, 16))
A_HEADS = 8
B_HEADS = 16
B_KV_HEADS = 4
B_BLOCK = 256
B_TOPK = 3
C_HEADS = 16
C_KV_HEADS = 4
C_CMP_BLOCK = 32
C_CMP_STRIDE = 16
C_SLC_BLOCK = 64
C_SLC_TOPK = 16
C_WINDOW = 512
GQA_GROUP = 4

ATTN_TILE = 256
PROMPT_ROW_TILE = 1024
VMEM_LIMIT_BYTES = 56 * 1024 * 1024


def _params(n_axes, vmem=VMEM_LIMIT_BYTES):
    return pltpu.CompilerParams(dimension_semantics=("arbitrary",) * n_axes, vmem_limit_bytes=vmem)


def _bucket_np(dist):
    exact = NUM_BUCKETS // 2
    d = np.maximum(np.asarray(dist, np.int64), 0)
    far = exact + (np.log(np.maximum(d, exact).astype(np.float64) / exact)
                   / math.log(MAX_DISTANCE / exact) * (NUM_BUCKETS - exact)).astype(np.int64)
    return np.where(d < exact, d, np.minimum(far, NUM_BUCKETS - 1)).astype(np.int32)


def _bias_by_dist(rel_bias, dist_np):
    tab = rel_bias[jnp.asarray(_bucket_np(dist_np))]
    return jnp.moveaxis(tab, -1, 0)


def _toeplitz(w, t):
    return jnp.tile(w, (1, t))[:, :2 * t * t].reshape(w.shape[0], t, 2 * t)[:, :, :t]


def _tile_tables(rel_bias, t, window=None, dil=1, dist_scale=1):
    assert window is None or window < t or (window % t == 0 and window >= 2 * t)
    k = np.arange(2 * t + 1)
    e = np.where(k <= t, -k, 2 * t + 1 - k)

    def table(dist, ok):
        vals = rel_bias[jnp.asarray(_bucket_np(dist * dist_scale))].T
        return _toeplitz(jnp.where(jnp.asarray(ok)[None], vals, NEG_INF), t)

    def allowed(dist):
        ok = dist >= 0
        if window is not None:
            ok &= dist <= window
        if dil > 1:
            ok &= dist % dil == 0
        return ok

    tabs = [table(e, allowed(e)), table(t + e, allowed(t + e))]
    if dil > 1:
        tabs.append(table(2 * t + e, (2 * t + e) % dil == 0))
    if window is not None and window >= 2 * t:
        tabs.append(table(window + e, allowed(window + e)))
    return jnp.stack(tabs, axis=1)


def _rms(x, g):
    return x * lax.rsqrt(jnp.mean(x * x, axis=-1, keepdims=True) + RMS_EPS) * g


def _split_bf16(x, parts):
    out = []
    for _ in range(parts):
        p = x.astype(BF16)
        out.append(p)
        x = x - p.astype(F32)
    return out


def _dot(a, b, dims=(((1,), (0,)), ((), ()))):
    return lax.dot_general(a, b, dims, preferred_element_type=F32)


_NT = (((1,), (1,)), ((), ()))
_TN = (((0,), (0,)), ((), ()))


def _dot_f32_lhs(a, b_bf16, dims=(((1,), (0,)), ((), ()))):
    return sum(_dot(p, b_bf16, dims) for p in _split_bf16(a, 3))


def _dot_f32_rhs(a_bf16, b, dims=(((1,), (0,)), ((), ()))):
    return sum(_dot(a_bf16, p, dims) for p in _split_bf16(b, 3))


def _dot_f32_both(a, b, dims):
    a_hi, a_lo = _split_bf16(a, 2)
    b_hi, b_lo = _split_bf16(b, 2)
    return _dot(a_hi, b_hi, dims) + (_dot(a_hi, b_lo, dims) + _dot(a_lo, b_hi, dims))


def _rank_lanes(score, n_lanes):
    lane = lax.broadcasted_iota(jnp.int32, score.shape, 1)
    rank = jnp.zeros(score.shape, jnp.int32)
    for m in range(n_lanes):
        sm = score[:, m:m + 1]
        beats = (sm > score) | ((sm == score) & (lane > m))
        rank = rank + beats.astype(jnp.int32)
    return rank


def _rank_rows(score, n_rows):
    row = lax.broadcasted_iota(jnp.int32, score.shape, 0)
    rank = jnp.zeros(score.shape, jnp.int32)
    for m in range(n_rows):
        sm = score[m:m + 1, :]
        beats = (sm > score) | ((sm == score) & (row > m))
        rank = rank + beats.astype(jnp.int32)
    return rank


def _norm_matmul_kernel(x_ref, g_ref, w_ref, o_ref, h_ref):
    @pl.when(pl.program_id(1) == 0)
    def _():
        h_ref[...] = _rms(x_ref[...], g_ref[...]).astype(BF16)

    o_ref[...] = _dot(h_ref[...], w_ref[...].astype(BF16))


def _norm_matmul(x, g_all, li, w_all, slot, n_cols, tm, tn):
    m, k = x.shape
    return pl.pallas_call(
        _norm_matmul_kernel,
        grid=(m // tm, n_cols // tn),
        in_specs=[
            pl.BlockSpec((tm, k), lambda i, j: (i, 0), pipeline_mode=pl.Buffered(1)),
            pl.BlockSpec((None, 1, k), lambda i, j: (li, 0, 0)),
            pl.BlockSpec((None, k, tn), lambda i, j: (slot, 0, j)),
        ],
        out_specs=pl.BlockSpec((tm, tn), lambda i, j: (i, j)),
        out_shape=jax.ShapeDtypeStruct((m, n_cols), F32),
        scratch_shapes=[pltpu.VMEM((tm, k), BF16)],
        compiler_params=_params(2),
        name="norm_matmul",
    )(x, g_all, w_all)


def _outproj_kernel(*refs, mode, n_heads):
    if mode == "plain":
        o_refs, extra, (w_ref, x_ref, out_ref, a_ref) = refs[:1], (), refs[1:]
    elif mode == "lse3":
        o_refs, extra, (w_ref, x_ref, out_ref, a_ref) = refs[:3], refs[3:6], refs[6:]
    else:
        o_refs, extra, (w_ref, x_ref, out_ref, a_ref) = refs[:3], refs[3:4], refs[4:]

    @pl.when(pl.program_id(1) == 0)
    def _():
        if mode == "plain":
            a = o_refs[0][...]
        elif mode == "lse3":
            lse = [r[...] for r in extra]
            mx = jnp.maximum(jnp.maximum(lse[0], lse[1]), lse[2])
            e = [jnp.exp(l - mx) for l in lse]
            den = e[0] + e[1] + e[2]
            a = sum((e[g] / den) * o_refs[g][...] for g in range(3))
        else:
            gate = jax.nn.sigmoid(extra[0][...])
            tm = gate.shape[0]
            heads = []
            for h in range(n_heads):
                acc = jnp.zeros((tm, HEAD_DIM), F32)
                for br in range(3):
                    gcol = gate[:, 3 * h + br:3 * h + br + 1]
                    acc = acc + gcol * o_refs[br][:, h * HEAD_DIM:(h + 1) * HEAD_DIM]
                heads.append(acc)
            a = jnp.concatenate(heads, axis=1)
        a_ref[...] = a.astype(BF16)

    out_ref[...] = x_ref[...] + _dot(a_ref[...], w_ref[...].astype(BF16))


def _outproj(o_list, extra_list, w_all, slot, x, mode, tm, tn):
    m, k = o_list[0].shape
    n = x.shape[1]
    row_spec = lambda width: pl.BlockSpec((tm, width), lambda i, j: (i, 0))
    in_specs = [row_spec(k) for _ in o_list] + [row_spec(e.shape[1]) for e in extra_list]
    in_specs += [pl.BlockSpec((None, k, tn), lambda i, j: (slot, 0, j)),
                 pl.BlockSpec((tm, tn), lambda i, j: (i, j))]
    return pl.pallas_call(
        functools.partial(_outproj_kernel, mode=mode, n_heads=k // HEAD_DIM),
        grid=(m // tm, n // tn),
        in_specs=in_specs,
        out_specs=pl.BlockSpec((tm, tn), lambda i, j: (i, j)),
        out_shape=jax.ShapeDtypeStruct((m, n), F32),
        scratch_shapes=[pltpu.VMEM((tm, k), BF16)],
        compiler_params=_params(2),
        name="outproj_" + mode,
    )(*o_list, *extra_list, w_all, x)


def _mlp_kernel(*refs, final):
    if final:
        x_ref, g_ref, wu_ref, wd_ref, gf_ref, o_ref, h_ref = refs
    else:
        x_ref, g_ref, wu_ref, wd_ref, o_ref, h_ref = refs
    f = pl.program_id(1)

    @pl.when(f == 0)
    def _():
        x = x_ref[...]
        h_ref[...] = _rms(x, g_ref[...]).astype(BF16)
        o_ref[...] = x

    u = _dot(h_ref[...], wu_ref[...].astype(BF16))
    act = jnp.square(jnp.maximum(u, 0.0)).astype(BF16)
    o_ref[...] += _dot(act, wd_ref[...].astype(BF16))

    if final:
        @pl.when(f == pl.num_programs(1) - 1)
        def _():
            o_ref[...] = _rms(o_ref[...], gf_ref[...])


def _mlp(x, g_all, li, w_up, w_down, final_g, tm, tf):
    m, d = x.shape
    d_ff = w_up.shape[2]
    in_specs = [
        pl.BlockSpec((tm, d), lambda i, f: (i, 0), pipeline_mode=pl.Buffered(1)),
        pl.BlockSpec((None, 1, d), lambda i, f: (li, 0, 0)),
        pl.BlockSpec((None, d, tf), lambda i, f: (li, 0, f)),
        pl.BlockSpec((None, tf, d), lambda i, f: (li, f, 0)),
    ]
    args = [x, g_all, w_up, w_down]
    if final_g is not None:
        in_specs.append(pl.BlockSpec((1, d), lambda i, f: (0, 0)))
        args.append(final_g)
    return pl.pallas_call(
        functools.partial(_mlp_kernel, final=final_g is not None),
        grid=(m // tm, d_ff // tf),
        in_specs=in_specs,
        out_specs=pl.BlockSpec((tm, d), lambda i, f: (i, 0)),
        out_shape=jax.ShapeDtypeStruct((m, d), F32),
        scratch_shapes=[pltpu.VMEM((tm, d), BF16)],
        compiler_params=_params(2),
        name="mlp",
    )(*args)


def _flash_kernel_old(*refs, group, t, window, dil, sel_blk, sel_per_head, want_lse, head0):
    it = iter(refs)
    q_ref, k_ref, v_ref, tab_ref, far_ref = (next(it) for _ in range(5))
    sel_ref = next(it) if sel_blk else None
    o_ref = next(it)
    lse_ref = next(it) if want_lse else None
    kb_ref, vb_ref, qs_ref, m_ref, l_ref, acc_ref = (next(it) for _ in range(6))

    kvh = pl.program_id(1)
    qi = pl.program_id(2)
    rows = group * t

    @pl.when(qi == 0)
    def _():
        kb_ref[...] = k_ref[0].astype(BF16)
        vb_ref[...] = v_ref[0].astype(BF16)

    for j in range(group):
        qs_ref[j * t:(j + 1) * t, :] = q_ref[0, :, j * HEAD_DIM:(j + 1) * HEAD_DIM].astype(BF16)
    m_ref[...] = jnp.full((rows, HEAD_DIM), NEG_INF, F32)
    l_ref[...] = jnp.zeros((rows, HEAD_DIM), F32)
    acc_ref[...] = jnp.zeros((rows, HEAD_DIM), F32)

    delta = (lax.broadcasted_iota(jnp.int32, (t, t), 0) - lax.broadcasted_iota(jnp.int32, (t, t), 1))

    def tile(kj, kind):
        start = pl.multiple_of(kj * t, t)
        k = kb_ref[pl.ds(start, t), :]
        v = vb_ref[pl.ds(start, t), :]
        s = _dot(qs_ref[...], k, _NT) * SOFTMAX_SCALE
        dist = delta + (qi - kj) * t
        ok = None
        if kind == "diag":
            ok = dist >= 0
        if window is not None:
            w_ok = dist <= window
            ok = w_ok if ok is None else ok & w_ok
        if dil > 1:
            d_ok = (dist & (dil - 1)) == 0
            ok = d_ok if ok is None else ok & d_ok
        sel_ok = None
        if sel_blk:
            blk_of_key = (start + lax.broadcasted_iota(jnp.int32, (HEAD_DIM, t), 1)) // sel_blk
            expand = (lax.broadcasted_iota(jnp.int32, (HEAD_DIM, t), 0) == blk_of_key)
            expand = jnp.where(expand, 1.0, 0.0).astype(BF16)
            if sel_per_head:
                sel_rows = jnp.concatenate([sel_ref[0, j] for j in range(group)], axis=0)
            else:
                sel_rows = sel_ref[0, 0]
            sel_ok = _dot(sel_rows.astype(BF16), expand) > 0.5
        pieces = []
        for j in range(group):
            sj = s[j * t:(j + 1) * t]
            if kind == "far":
                sj = sj + far_ref[head0 + kvh * group + j]
            else:
                sj = sj + tab_ref[j, 0 if kind == "diag" else 1]
            mask = ok
            if sel_blk:
                sel_j = sel_ok[j * t:(j + 1) * t] if sel_per_head else sel_ok
                mask = sel_j if mask is None else mask & sel_j
            pieces.append((sj, mask))
        if pieces[0][1] is not None:
            s = jnp.concatenate([jnp.where(mk, sj, NEG_INF) for sj, mk in pieces], axis=0)
            mask_all = jnp.concatenate([mk for _, mk in pieces], axis=0)
        else:
            s = jnp.concatenate([sj for sj, _ in pieces], axis=0)
            mask_all = None
        m_prev = m_ref[...]
        m_next = jnp.maximum(m_prev, jnp.max(s, axis=1, keepdims=True))
        p = jnp.exp(s - jnp.concatenate([m_next] * (t // HEAD_DIM), axis=1))
        if mask_all is not None:
            p = jnp.where(mask_all, p, 0.0)
        alpha = jnp.exp(m_prev - m_next)
        l_ref[...] = alpha * l_ref[...] + jnp.sum(p, axis=1, keepdims=True)
        acc_ref[...] = alpha * acc_ref[...] + _dot(p.astype(BF16), v)
        m_ref[...] = m_next

    lo = 0 if window is None else jnp.maximum(qi * t - window, 0) // t

    def far_body(kj, carry):
        tile(kj, "far")
        return carry

    lax.fori_loop(lo, jnp.maximum(qi - 1, lo), far_body, 0)

    @pl.when(qi - 1 >= lo)
    def _():
        tile(qi - 1, "near")

    tile(qi, "diag")

    l = l_ref[...]
    out = acc_ref[...] / l
    for j in range(group):
        o_ref[0, :, j * HEAD_DIM:(j + 1) * HEAD_DIM] = out[j * t:(j + 1) * t]
    if want_lse:
        lse = m_ref[...] + jnp.log(l)
        for j in range(group):
            lse_ref[0, :, j * HEAD_DIM:(j + 1) * HEAD_DIM] = lse[j * t:(j + 1) * t]


def _flash_old(proj, tabs, far, sel, *, n_kv, group, q_col, k_col, v_col, head0=0,
           window=None, dil=1, sel_blk=0, sel_per_head=False, want_lse=False):
    b, s, _ = proj.shape
    t = ATTN_TILE
    gw = group * HEAD_DIM
    in_specs = [
        pl.BlockSpec((1, t, gw), lambda bi, h, qi: (bi, qi, q_col // group + h)),
        pl.BlockSpec((1, s, HEAD_DIM), lambda bi, h, qi: (bi, 0, k_col + h)),
        pl.BlockSpec((1, s, HEAD_DIM), lambda bi, h, qi: (bi, 0, v_col + h)),
        pl.BlockSpec((group, 2, t, t), lambda bi, h, qi: (head0 // group + h, 0, 0, 0)),
        pl.BlockSpec(memory_space=pltpu.SMEM),
    ]
    args = [proj, proj, proj, tabs, far]
    if sel_blk:
        if sel_per_head:
            in_specs.append(pl.BlockSpec((1, group, t, HEAD_DIM), lambda bi, h, qi: (bi, h, qi, 0)))
        else:
            in_specs.append(pl.BlockSpec((1, 1, t, HEAD_DIM), lambda bi, h, qi: (bi, h, qi, 0)))
        args.append(sel)
    out_spec = pl.BlockSpec((1, t, gw), lambda bi, h, qi: (bi, qi, h))
    out_sds = jax.ShapeDtypeStruct((b, s, n_kv * gw), F32)
    rows = group * t
    return pl.pallas_call(
        functools.partial(_flash_kernel, group=group, t=t, window=window, dil=dil, sel_blk=sel_blk,
                          sel_per_head=sel_per_head, want_lse=want_lse, head0=head0),
        grid=(b, n_kv, s // t),
        in_specs=in_specs,
        out_specs=[out_spec, out_spec] if want_lse else out_spec,
        out_shape=[out_sds, out_sds] if want_lse else out_sds,
        scratch_shapes=[pltpu.VMEM((s, HEAD_DIM), BF16), pltpu.VMEM((s, HEAD_DIM), BF16),
                        pltpu.VMEM((rows, HEAD_DIM), BF16), pltpu.VMEM((rows, HEAD_DIM), F32),
                        pltpu.VMEM((rows, HEAD_DIM), F32), pltpu.VMEM((rows, HEAD_DIM), F32)],
        compiler_params=_params(3),
        name="flash",
    )(*args)


def _flash_kernel(*refs, group, t, w_tiles, far_tab, edge_tab, sel_blk, sel_per_head, want_lse, head0):
    it = iter(refs)
    q_ref, k_ref, v_ref, tab_ref, far_ref = (next(it) for _ in range(5))
    sel_ref = next(it) if sel_blk else None
    o_ref = next(it)
    lse_ref = next(it) if want_lse else None
    kb_ref, vb_ref, qs_ref, s_ref, m_ref, l_ref, acc_ref = (next(it) for _ in range(7))

    kvh = pl.program_id(1)
    qi = pl.program_id(2)
    rows = group * t

    @pl.when(qi == 0)
    def _():
        kb_ref[...] = k_ref[0].astype(BF16)
        vb_ref[...] = v_ref[0].astype(BF16)

    for j in range(group):
        qs_ref[j * t:(j + 1) * t, :] = q_ref[0, :, j * HEAD_DIM:(j + 1) * HEAD_DIM].astype(BF16)

    def tile(kj, tab, first=False):
        start = pl.multiple_of(kj * t, t)
        k = kb_ref[pl.ds(start, t), :]
        v = vb_ref[pl.ds(start, t), :]
        sel_add = None
        if sel_blk and not sel_per_head:
            blk_of_key = (start + lax.broadcasted_iota(jnp.int32, (HEAD_DIM, t), 1)) // sel_blk
            expand = lax.broadcasted_iota(jnp.int32, (HEAD_DIM, t), 0) == blk_of_key
            expand = jnp.where(expand, 1.0, 0.0).astype(BF16)
            sel_add = _dot(sel_ref[0, 0].astype(BF16), expand)
        for j in range(group):
            hs = slice(j * t, (j + 1) * t)
            sj = _dot(qs_ref[hs, :], k, _NT) * SOFTMAX_SCALE
            sj = sj + (far_ref[head0 + kvh * group + j] if tab is None else tab_ref[j, tab])
            if sel_blk and sel_per_head:
                assert sel_blk == t
                lane = lax.broadcasted_iota(jnp.int32, (t, HEAD_DIM), 1)
                sj = sj + jnp.sum(jnp.where(lane == kj, sel_ref[0, j], 0.0), axis=1, keepdims=True)
            elif sel_blk:
                sj = sj + sel_add
            m_cur = jnp.max(jnp.maximum(sj[:, :HEAD_DIM], sj[:, HEAD_DIM:]), axis=1, keepdims=True)
            if first:
                m_next = jnp.broadcast_to(m_cur, (t, HEAD_DIM))
            else:
                m_prev = m_ref[hs, :]
                m_next = jnp.maximum(m_prev, m_cur)
            p = jnp.exp(sj - jnp.concatenate([m_next] * (t // HEAD_DIM), axis=1))
            p_sum = jnp.sum(p[:, :HEAD_DIM] + p[:, HEAD_DIM:], axis=1, keepdims=True)
            pv = _dot(p.astype(BF16), v)
            if first:
                l_ref[hs, :] = jnp.broadcast_to(p_sum, (t, HEAD_DIM))
                acc_ref[hs, :] = pv
            else:
                alpha = jnp.exp(m_prev - m_next)
                l_ref[hs, :] = alpha * l_ref[hs, :] + p_sum
                acc_ref[hs, :] = alpha * acc_ref[hs, :] + pv
            m_ref[hs, :] = m_next

    tile(qi, 0, first=True)

    @pl.when(qi >= 1)
    def _():
        tile(qi - 1, 1)

    def far_body(kj, carry):
        tile(kj, far_tab)
        return carry

    if w_tiles is None:
        lax.fori_loop(0, qi - 1, far_body, 0)
    elif w_tiles >= 2:
        lax.fori_loop(jnp.maximum(qi - w_tiles + 1, 0), qi - 1, far_body, 0)

        @pl.when(qi >= w_tiles)
        def _():
            tile(qi - w_tiles, edge_tab)

    l = l_ref[...]
    out = acc_ref[...] / l
    for j in range(group):
        o_ref[0, :, j * HEAD_DIM:(j + 1) * HEAD_DIM] = out[j * t:(j + 1) * t]
    if want_lse:
        lse = m_ref[...] + jnp.log(l)
        for j in range(group):
            lse_ref[0, :, j * HEAD_DIM:(j + 1) * HEAD_DIM] = lse[j * t:(j + 1) * t]


def _flash(proj, tabs, far, sel, *, n_kv, group, q_col, k_col, v_col, head0=0,
           window=None, dil=1, sel_blk=0, sel_per_head=False, want_lse=False):
    b, s, _ = proj.shape
    t = ATTN_TILE
    gw = group * HEAD_DIM
    n_tab = tabs.shape[1]
    w_tiles = None if window is None else window // t
    far_tab = 2 if dil > 1 else None
    edge_tab = n_tab - 1 if (w_tiles is not None and w_tiles >= 2) else None
    in_specs = [
        pl.BlockSpec((1, t, gw), lambda bi, h, qi: (bi, qi, q_col // group + h)),
        pl.BlockSpec((1, s, HEAD_DIM), lambda bi, h, qi: (bi, 0, k_col + h)),
        pl.BlockSpec((1, s, HEAD_DIM), lambda bi, h, qi: (bi, 0, v_col + h)),
        pl.BlockSpec((group, n_tab, t, t), lambda bi, h, qi: (head0 // group + h, 0, 0, 0)),
        pl.BlockSpec(memory_space=pltpu.SMEM),
    ]
    args = [proj, proj, proj, tabs, far]
    if sel_blk:
        if sel_per_head:
            in_specs.append(pl.BlockSpec((1, group, t, HEAD_DIM), lambda bi, h, qi: (bi, h, qi, 0)))
        else:
            in_specs.append(pl.BlockSpec((1, 1, t, HEAD_DIM), lambda bi, h, qi: (bi, h, qi, 0)))
        args.append(sel)
    out_spec = pl.BlockSpec((1, t, gw), lambda bi, h, qi: (bi, qi, h))
    out_sds = jax.ShapeDtypeStruct((b, s, n_kv * gw), F32)
    rows = group * t
    return pl.pallas_call(
        functools.partial(_flash_kernel, group=group, t=t, w_tiles=w_tiles, far_tab=far_tab, edge_tab=edge_tab,
                          sel_blk=sel_blk, sel_per_head=sel_per_head, want_lse=want_lse, head0=head0),
        grid=(b, n_kv, s // t),
        in_specs=in_specs,
        out_specs=[out_spec, out_spec] if want_lse else out_spec,
        out_shape=[out_sds, out_sds] if want_lse else out_sds,
        scratch_shapes=[pltpu.VMEM((s, HEAD_DIM), BF16), pltpu.VMEM((s, HEAD_DIM), BF16),
                        pltpu.VMEM((rows, HEAD_DIM), BF16), pltpu.VMEM((rows, t), F32),
                        pltpu.VMEM((rows, HEAD_DIM), F32), pltpu.VMEM((rows, HEAD_DIM), F32),
                        pltpu.VMEM((rows, HEAD_DIM), F32)],
        compiler_params=_params(3),
        name="flash",
    )(*args)


def _dilated_kernel(q_ref, k_ref, v_ref, tab_ref, o_ref, lse_ref, m_ref, l_ref, acc_ref, *, t, dil, n_q):
    def rows(r, tile_idx):
        return pl.ds(r + tile_idx * (t * dil), t, stride=dil)

    def attend(q, r, kj, tab, first):
        k = k_ref[0, rows(r, kj), :].astype(BF16)
        v = v_ref[0, rows(r, kj), :].astype(BF16)
        s = _dot(q, k, _NT) * SOFTMAX_SCALE + tab_ref[0, tab]
        m_cur = jnp.max(jnp.maximum(s[:, :HEAD_DIM], s[:, HEAD_DIM:]), axis=1, keepdims=True)
        if first:
            m_next = jnp.broadcast_to(m_cur, (t, HEAD_DIM))
        else:
            m_prev = m_ref[...]
            m_next = jnp.maximum(m_prev, m_cur)
        p = jnp.exp(s - jnp.concatenate([m_next] * (t // HEAD_DIM), axis=1))
        p_sum = jnp.sum(p[:, :HEAD_DIM] + p[:, HEAD_DIM:], axis=1, keepdims=True)
        pv = _dot(p.astype(BF16), v)
        if first:
            l_ref[...] = jnp.broadcast_to(p_sum, (t, HEAD_DIM))
            acc_ref[...] = pv
        else:
            alpha = jnp.exp(m_prev - m_next)
            l_ref[...] = alpha * l_ref[...] + p_sum
            acc_ref[...] = alpha * acc_ref[...] + pv
        m_ref[...] = m_next

    def body(idx, carry):
        r = idx // n_q
        qi = idx % n_q
        q = q_ref[0, rows(r, qi), :].astype(BF16)
        attend(q, r, qi, 0, True)

        @pl.when(qi >= 1)
        def _():
            attend(q, r, qi - 1, 1, False)

        l = l_ref[...]
        o_ref[0, rows(r, qi), :] = acc_ref[...] / l
        lse_ref[0, rows(r, qi), :] = m_ref[...] + jnp.log(l)
        return carry

    lax.fori_loop(0, dil * n_q, body, 0)


def _dilated_attention(proj, tabs, *, n_heads, q_col, k_col, v_col, dil):
    b, s, _ = proj.shape
    t = ATTN_TILE
    assert t == 2 * HEAD_DIM and (s // dil) % t == 0
    spec = lambda col: pl.BlockSpec((1, s, HEAD_DIM), lambda bi, h: (bi, 0, col + h))
    out_sds = jax.ShapeDtypeStruct((b, s, n_heads * HEAD_DIM), F32)
    return pl.pallas_call(
        functools.partial(_dilated_kernel, t=t, dil=dil, n_q=s // dil // t),
        grid=(b, n_heads),
        in_specs=[spec(q_col), spec(k_col), spec(v_col),
                  pl.BlockSpec((1, 2, t, t), lambda bi, h: (h, 0, 0, 0))],
        out_specs=[spec(0), spec(0)],
        out_shape=[out_sds, out_sds],
        scratch_shapes=[pltpu.VMEM((t, HEAD_DIM), F32)] * 3,
        compiler_params=_params(2),
        name="dilated_attention",
    )(proj, proj, proj, tabs)


def _moba_select_kernel(q_ref, k_ref, sel_ref, kmean_ref, *, group, t, n_blk):
    qi = pl.program_id(2)

    @pl.when(qi == 0)
    def _():
        kmean_ref[...] = jnp.zeros_like(kmean_ref)
        s = k_ref.shape[1]
        km = jnp.sum(k_ref[0].reshape(n_blk, s // n_blk, HEAD_DIM), axis=1) * (1.0 / B_BLOCK)
        kmean_ref[0:n_blk, :] = km

    lane = lax.broadcasted_iota(jnp.int32, (t, HEAD_DIM), 1)
    pos = qi * t + lax.broadcasted_iota(jnp.int32, (t, HEAD_DIM), 0)
    own = pos // B_BLOCK
    past = lane < own
    kmean = kmean_ref[...]
    for j in range(group):
        q = q_ref[0, :, j * HEAD_DIM:(j + 1) * HEAD_DIM]
        gate = _dot_f32_both(q, kmean, _NT)
        gate = jnp.where(past, gate, NEG_INF)
        rank = _rank_lanes(gate, n_blk)
        sel = (past & (rank < B_TOPK)) | (lane == own)
        sel_ref[0, j] = jnp.where(sel, 0.0, NEG_INF)


def _moba_select(proj, *, n_kv, group, k_col):
    b, s, _ = proj.shape
    t = ATTN_TILE
    n_blk = s // B_BLOCK
    return pl.pallas_call(
        functools.partial(_moba_select_kernel, group=group, t=t, n_blk=n_blk),
        grid=(b, n_kv, s // t),
        in_specs=[pl.BlockSpec((1, t, group * HEAD_DIM), lambda bi, h, qi: (bi, qi, h)),
                  pl.BlockSpec((1, s, HEAD_DIM), lambda bi, h, qi: (bi, 0, k_col + h))],
        out_specs=pl.BlockSpec((1, group, t, HEAD_DIM), lambda bi, h, qi: (bi, h, qi, 0)),
        out_shape=jax.ShapeDtypeStruct((b, n_kv * group, s, HEAD_DIM), F32),
        scratch_shapes=[pltpu.VMEM((HEAD_DIM, HEAD_DIM), F32)],
        compiler_params=_params(3),
        name="moba_select",
    )(proj, proj)


def _cmp1_prompt_kernel(x_ref, w_ref, y_ref, *, n_seg):
    x2 = jnp.concatenate([x_ref[pl.ds(s, n_seg, stride=C_CMP_STRIDE), :] for s in range(C_CMP_STRIDE)], axis=1)
    y_ref[...] = _dot(x2.astype(BF16), w_ref[...])


def _cmp1_prompt(proj, w1cat, *, col0):
    b, s, _ = proj.shape
    n_seg = s // C_CMP_STRIDE
    return pl.pallas_call(
        functools.partial(_cmp1_prompt_kernel, n_seg=n_seg),
        grid=(b, 2, C_KV_HEADS),
        in_specs=[pl.BlockSpec((None, s, HEAD_DIM), lambda bi, c, g: (bi, 0, col0 + c * C_KV_HEADS + g)),
                  pl.BlockSpec((None, C_CMP_STRIDE * HEAD_DIM, 2 * HEAD_DIM), lambda bi, c, g: (c, 0, 0))],
        out_specs=pl.BlockSpec((None, n_seg, 2 * HEAD_DIM), lambda bi, c, g: ((bi * 2 + c) * C_KV_HEADS + g, 0, 0)),
        out_shape=jax.ShapeDtypeStruct((b * 2 * C_KV_HEADS, n_seg, 2 * HEAD_DIM), F32),
        compiler_params=_params(3),
        name="cmp1_prompt",
    )(proj, w1cat)


def _cmp2_kernel(y_ref, pe_ref, w1_ref, w2_ref, o_ref, *, n_seg):
    y = y_ref[0]
    first = y[:, :HEAD_DIM]
    second = pltpu.roll(y[:, HEAD_DIM:], n_seg - 1, 0)
    pe_term = _dot(pe_ref[0].astype(BF16), w1_ref[0].astype(BF16))[0:1]
    pre = first + second + pe_term
    tok = _dot(jnp.maximum(pre, 0.0).astype(BF16), w2_ref[0].astype(BF16))
    row = lax.broadcasted_iota(jnp.int32, tok.shape, 0)
    o_ref[0] = jnp.where(row < n_seg - 1, tok, 0.0)


def _cmp2(y, pe8, w1, w2):
    r, n_seg, _ = y.shape
    comp = lambda i: ((i // C_KV_HEADS) % 2, 0, 0)
    return pl.pallas_call(
        functools.partial(_cmp2_kernel, n_seg=n_seg),
        grid=(r,),
        in_specs=[pl.BlockSpec((1, n_seg, 2 * HEAD_DIM), lambda i: (i, 0, 0)),
                  pl.BlockSpec((1, 8, C_CMP_BLOCK * HEAD_DIM), comp),
                  pl.BlockSpec((1, C_CMP_BLOCK * HEAD_DIM, HEAD_DIM), comp),
                  pl.BlockSpec((1, HEAD_DIM, HEAD_DIM), comp)],
        out_specs=pl.BlockSpec((1, n_seg, HEAD_DIM), lambda i: (i, 0, 0)),
        out_shape=jax.ShapeDtypeStruct((r, n_seg, HEAD_DIM), F32),
        compiler_params=_params(1),
        name="cmp2",
    )(y, pe8, w1, w2)


def _nsa_cmp_select_kernel(q_ref, kc_ref, vc_ref, bias_ref, map_ref, oc_ref, sel_ref, *, group, t, n_cmp, n_sel):
    qi = pl.program_id(2)
    n_pad = kc_ref.shape[1]
    pos = qi * t + lax.broadcasted_iota(jnp.int32, (t, n_pad), 0)
    tok = lax.broadcasted_iota(jnp.int32, (t, n_pad), 1)
    valid = (pos - (tok * C_CMP_STRIDE + (C_CMP_BLOCK - 1)) >= 0) & (tok < n_cmp)
    kc = kc_ref[0].astype(BF16)
    vc = vc_ref[0].astype(BF16)
    p_sum = jnp.zeros((t, n_pad), F32)
    for j in range(group):
        q = q_ref[0, :, j * HEAD_DIM:(j + 1) * HEAD_DIM].astype(BF16)
        s = _dot(q, kc, _NT) * SOFTMAX_SCALE + bias_ref[j]
        s = jnp.where(valid, s, NEG_INF)
        m = jnp.max(s, axis=1, keepdims=True)
        e = jnp.where(valid, jnp.exp(s - m), 0.0)
        p = e / jnp.maximum(jnp.sum(e, axis=1, keepdims=True), 1e-30)
        oc_ref[0, :, j * HEAD_DIM:(j + 1) * HEAD_DIM] = _dot(p.astype(BF16), vc)
        p_sum = p_sum + p
    imp = _dot_f32_lhs(p_sum, map_ref[...])
    blk = lax.broadcasted_iota(jnp.int32, (t, HEAD_DIM), 1)
    cur = (qi * t + lax.broadcasted_iota(jnp.int32, (t, HEAD_DIM), 0)) // C_SLC_BLOCK
    avail = (blk <= cur) & (blk < n_sel)
    forced = (blk == 0) | (blk == cur) | (blk == cur - 1)
    score = jnp.where(forced, jnp.inf, imp)
    score = jnp.where(avail, score, -jnp.inf)
    rank = _rank_lanes(score, n_sel)
    sel = avail & (rank < min(C_SLC_TOPK, n_sel))
    sel_ref[0, 0] = jnp.where(sel, 0.0, NEG_INF)


def _cmp_bias_table(rel_bias, s, n_seg):
    assert s == n_seg * C_CMP_STRIDE
    h = rel_bias.shape[1]
    k = np.arange(2 * n_seg + 1)
    e = np.where(k <= n_seg, -k, 2 * n_seg + 1 - k)
    dist = C_CMP_STRIDE * e[None, :] + np.arange(C_CMP_STRIDE)[:, None] - (C_CMP_BLOCK - 1)
    w = jnp.transpose(rel_bias[jnp.asarray(_bucket_np(dist))], (2, 0, 1))
    tab = _toeplitz(w.reshape(h * C_CMP_STRIDE, -1), n_seg).reshape(h, C_CMP_STRIDE, n_seg, n_seg)
    return jnp.transpose(tab, (0, 2, 1, 3)).reshape(h, s, n_seg)


def _cmp_map_np(n_rows, n_cmp, n_cols):
    ratio = C_SLC_BLOCK // C_CMP_STRIDE
    m = np.zeros((n_rows, n_cols), np.float32)
    for i in range(n_cmp):
        for r in range(C_CMP_BLOCK // C_CMP_STRIDE):
            m[i, (i + r) // ratio] += 1.0
    return m


def _nsa_cmp_select(proj, kvc, bias_c, *, n_kv, group):
    b, s, _ = proj.shape
    t = ATTN_TILE
    n_seg = kvc.shape[1]
    n_cmp = n_seg - 1
    n_sel = -(-s // C_SLC_BLOCK)
    cmap = jnp.asarray(_cmp_map_np(n_seg, n_cmp, HEAD_DIM), BF16)
    gw = group * HEAD_DIM
    return pl.pallas_call(
        functools.partial(_nsa_cmp_select_kernel, group=group, t=t, n_cmp=n_cmp, n_sel=n_sel),
        grid=(b, n_kv, s // t),
        in_specs=[pl.BlockSpec((1, t, gw), lambda bi, h, qi: (bi, qi, h)),
                  pl.BlockSpec((1, n_seg, HEAD_DIM), lambda bi, h, qi: (bi * 8 + h, 0, 0)),
                  pl.BlockSpec((1, n_seg, HEAD_DIM), lambda bi, h, qi: (bi * 8 + 4 + h, 0, 0)),
                  pl.BlockSpec((group, t, n_seg), lambda bi, h, qi: (h, qi, 0)),
                  pl.BlockSpec((n_seg, HEAD_DIM), lambda bi, h, qi: (0, 0))],
        out_specs=[pl.BlockSpec((1, t, gw), lambda bi, h, qi: (bi, qi, h)),
                   pl.BlockSpec((1, 1, t, HEAD_DIM), lambda bi, h, qi: (bi, h, qi, 0))],
        out_shape=[jax.ShapeDtypeStruct((b, s, n_kv * gw), F32),
                   jax.ShapeDtypeStruct((b, n_kv, s, HEAD_DIM), F32)],
        compiler_params=_params(3),
        name="nsa_cmp_select",
    )(proj, kvc, kvc, bias_c, cmap)


def _row_tile(m, pref):
    return pref if m % pref == 0 else m


def _mixer_a_prompt(x, b, g_mix, li, w_in, w_out, slot, tabs, far):
    m = x.shape[0]
    s = m // b
    n_in = w_in.shape[2]
    proj = _norm_matmul(x, g_mix, li, w_in, slot, n_in, _row_tile(m, PROMPT_ROW_TILE), 512)
    proj3 = proj.reshape(b, s, n_in)
    outs, lses = [], []
    for g, (window, dil) in enumerate(A_PATTERNS):
        c0 = g * 3 * A_HEADS
        o, lse = _dilated_attention(proj3, tabs[g], n_heads=A_HEADS, q_col=c0, k_col=c0 + A_HEADS,
                                    v_col=c0 + 2 * A_HEADS, dil=dil)
        outs.append(o.reshape(m, -1))
        lses.append(lse.reshape(m, -1))
    x = _outproj(outs, lses, w_out, slot, x, "lse3", _row_tile(m, 512), 512)
    return x, proj3


def _mixer_b_prompt(x, b, g_mix, li, w_in, w_out, slot, tabs, far):
    m = x.shape[0]
    s = m // b
    n_in = w_in.shape[2]
    proj = _norm_matmul(x, g_mix, li, w_in, slot, n_in, _row_tile(m, PROMPT_ROW_TILE), 512)
    proj3 = proj.reshape(b, s, n_in)
    sel = _moba_select(proj3, n_kv=B_KV_HEADS, group=GQA_GROUP, k_col=B_HEADS)
    o = _flash(proj3, tabs, far, sel, n_kv=B_KV_HEADS, group=GQA_GROUP, q_col=0, k_col=B_HEADS,
               v_col=B_HEADS + B_KV_HEADS, sel_blk=B_BLOCK, sel_per_head=True)
    x = _outproj([o.reshape(m, -1)], [], w_out, slot, x, "plain", _row_tile(m, 512), 512)
    return x, proj3


def _cmp_weights(cmp_pe, cmp_w1, cmp_w2, slot):
    half = C_CMP_STRIDE * HEAD_DIM
    w1 = cmp_w1[slot]
    w1cat = jnp.concatenate([w1[:, :half], w1[:, half:]], axis=2).astype(BF16)
    pe8 = jnp.broadcast_to(cmp_pe[slot].reshape(2, 1, C_CMP_BLOCK * HEAD_DIM), (2, 8, C_CMP_BLOCK * HEAD_DIM))
    return w1cat, pe8, w1, cmp_w2[slot]


def _mixer_c_prompt(x, b, g_mix, li, w_in, w_gate, w_out, slot, cmp_w, tabs, tabs_win, far, rel_bias):
    m = x.shape[0]
    s = m // b
    n_main = (C_HEADS + 6 * C_KV_HEADS) * HEAD_DIM
    proj = _norm_matmul(x, g_mix, li, w_in, slot, n_main, _row_tile(m, PROMPT_ROW_TILE), 512)
    gate = _norm_matmul(x, g_mix, li, w_gate, 0, HEAD_DIM, _row_tile(m, 512), HEAD_DIM)
    proj3 = proj.reshape(b, s, n_main)
    w1cat, pe8, w1, w2 = cmp_w
    y = _cmp1_prompt(proj3, w1cat, col0=C_HEADS)
    kvc = _cmp2(y, pe8, w1, w2)
    n_seg = kvc.shape[1]
    bias_c = _cmp_bias_table(rel_bias, s, n_seg)
    o_cmp, sel = _nsa_cmp_select(proj3, kvc, bias_c, n_kv=C_KV_HEADS, group=GQA_GROUP)
    kv0 = C_HEADS
    o_slc = _flash(proj3, tabs, far, sel, n_kv=C_KV_HEADS, group=GQA_GROUP, q_col=0, k_col=kv0 + 2 * C_KV_HEADS,
                   v_col=kv0 + 3 * C_KV_HEADS, sel_blk=C_SLC_BLOCK)
    o_win = _flash(proj3, tabs_win, far, None, n_kv=C_KV_HEADS, group=GQA_GROUP, q_col=0, k_col=kv0 + 4 * C_KV_HEADS,
                   v_col=kv0 + 5 * C_KV_HEADS, window=C_WINDOW)
    x = _outproj([o_cmp.reshape(m, -1), o_slc.reshape(m, -1), o_win.reshape(m, -1)], [gate], w_out, slot, x,
                 "gate3", _row_tile(m, 512), 512)
    return x, proj3


def _a_decode_kernel(qkv_ref, buf0_ref, buf1_ref, buf2_ref, bias_ref, o_ref):
    outs, lses = [], []
    for g, buf in enumerate((buf0_ref, buf1_ref, buf2_ref)):
        q, k_new, v_new = qkv_ref[0, g, 0], qkv_ref[0, g, 1], qkv_ref[0, g, 2]
        kb, vb = buf[:, 0], buf[:, 1]
        n = kb.shape[0]
        s = jnp.sum(q[None] * kb, axis=-1, keepdims=True) * SOFTMAX_SCALE + bias_ref[g, 0:n]
        s_new = jnp.sum(q * k_new, axis=-1, keepdims=True) * SOFTMAX_SCALE + bias_ref[g, n]
        m = jnp.maximum(jnp.max(s, axis=0), s_new)
        e = jnp.exp(s - m[None])
        e_new = jnp.exp(s_new - m)
        l = jnp.sum(e, axis=0) + e_new
        outs.append((jnp.sum(e * vb, axis=0) + e_new * v_new) / l)
        lses.append(m + jnp.log(l))
    mx = jnp.maximum(jnp.maximum(lses[0], lses[1]), lses[2])
    w = [jnp.exp(l - mx) for l in lses]
    o_ref[0] = (w[0] * outs[0] + w[1] * outs[1] + w[2] * outs[2]) / (w[0] + w[1] + w[2])


def _a_decode(qkv, bufs, slot, bias):
    n_dec = qkv.shape[0]
    in_specs = [pl.BlockSpec((1, 3, 3, A_HEADS, HEAD_DIM), lambda i: (i, 0, 0, 0, 0))]
    views = []
    for (window, dil), buf in zip(A_PATTERNS, bufs):
        n = window // dil
        assert buf.shape[2] == window, "window buffers must hold exactly `window` past rows"
        views.append(buf.reshape(buf.shape[0], n_dec, n, dil, 2, A_HEADS, HEAD_DIM))
        in_specs.append(pl.BlockSpec((None, None, n, None, 2, A_HEADS, HEAD_DIM),
                                     lambda i: (slot, i, 0, 0, 0, 0, 0)))
    in_specs.append(pl.BlockSpec(bias.shape, lambda i: (0, 0, 0, 0)))
    return pl.pallas_call(
        _a_decode_kernel,
        grid=(n_dec,),
        in_specs=in_specs,
        out_specs=pl.BlockSpec((1, A_HEADS, HEAD_DIM), lambda i: (i, 0, 0)),
        out_shape=jax.ShapeDtypeStruct((n_dec, A_HEADS, HEAD_DIM), F32),
        compiler_params=_params(1),
        name="a_decode",
    )(qkv, *views, bias)


DECODE_CHUNK = 2048


def _head_rows_store(o_ref, o_all, l_row):
    l_col = jnp.broadcast_to(l_row, (HEAD_DIM, HEAD_DIM)).T
    for g in range(B_KV_HEADS):
        rows = slice(g * GQA_GROUP, (g + 1) * GQA_GROUP)
        o_ref[0, rows, :] = o_all[rows, g * HEAD_DIM:(g + 1) * HEAD_DIM] / l_col[rows, :]


def _paged_decode_kernel(pt_ref, *refs, n_pg, n_sub, k_sub, v_sub, blk, topk, n_pages):
    page_refs = refs[:n_pg]
    qbd_ref, new_ref, bias_ref = refs[n_pg:n_pg + 3]
    rest = refs[n_pg + 3:]
    if topk is None:
        sel_ref, o_ref, st_ref, vs_ref, sel_s = rest
    else:
        o_ref, st_ref, vs_ref, sel_s = rest
    j = pl.program_id(1)
    past = n_pages * PAGE_SIZE
    n_blk = past // blk
    ch = DECODE_CHUNK

    if topk is not None:
        @pl.when(j == 0)
        def _():
            sel_s[...] = jnp.zeros_like(sel_s)

    for p in range(n_pg):
        page = j * n_pg + p
        row0 = pl.multiple_of(page * PAGE_SIZE, PAGE_SIZE)
        s_acc = jnp.zeros((PAGE_SIZE, HEAD_DIM), F32)
        for g in range(B_KV_HEADS):
            kg = page_refs[p][pl.ds(k_sub + g, PAGE_SIZE, stride=n_sub), :].astype(BF16)
            s_acc = s_acc + _dot(kg, qbd_ref[0, g])
            vg = page_refs[p][pl.ds(v_sub + g, PAGE_SIZE, stride=n_sub), :].astype(BF16)
            vs_ref[pl.ds(row0, PAGE_SIZE), g * HEAD_DIM:(g + 1) * HEAD_DIM] = vg
        st_ref[pl.ds(row0, PAGE_SIZE), :] = s_acc
        if topk is not None:
            b_idx = page // (blk // PAGE_SIZE)
            sel_s[pl.ds(b_idx, 1), :] = sel_s[pl.ds(b_idx, 1), :] + jnp.sum(s_acc, axis=0, keepdims=True)

    @pl.when(j == pl.num_programs(1) - 1)
    def _():
        s_new = jnp.zeros((8, HEAD_DIM), F32)
        for g in range(B_KV_HEADS):
            k_new = jnp.broadcast_to(new_ref[0, g:g + 1, :], (8, HEAD_DIM)).astype(BF16)
            s_new = s_new + _dot(k_new, qbd_ref[0, g])
            v_new = jnp.broadcast_to(new_ref[0, B_KV_HEADS + g:B_KV_HEADS + g + 1, :], (8, HEAD_DIM))
            vs_ref[past:past + 8, g * HEAD_DIM:(g + 1) * HEAD_DIM] = v_new.astype(BF16)
        l_new = s_new[0:1] * SOFTMAX_SCALE + bias_ref[ch + 1:ch + 2, :]

        if topk is not None:
            gate = sel_s[...] * (1.0 / blk)
            rank = _rank_rows(gate, n_blk)
            sel_s[...] = jnp.where(rank < topk, 1.0, 0.0)
        else:
            sel_s[...] = sel_ref[0, 0:n_blk, :]

        def masked_logits(c, bias):
            start = pl.multiple_of(c * ch, ch)
            lg = st_ref[pl.ds(start, ch), :] * SOFTMAX_SCALE + bias
            b0 = pl.multiple_of(c * (ch // blk), 8)
            sl = sel_s[pl.ds(b0, ch // blk), :]
            mk = jnp.broadcast_to(sl[:, None, :], (ch // blk, blk, HEAD_DIM)).reshape(ch, HEAD_DIM) > 0.5
            return jnp.where(mk, lg, NEG_INF), mk, start

        n_ch = past // ch
        far = bias_ref[ch:ch + 1, :]
        tail = bias_ref[0:ch, :]

        def max_body(c, m):
            lg, _, _ = masked_logits(c, far)
            return jnp.maximum(m, jnp.max(lg, axis=0, keepdims=True))

        m = lax.fori_loop(0, n_ch - 1, max_body, l_new)
        lg_t, mk_t, start_t = masked_logits(n_ch - 1, tail)
        m = jnp.maximum(m, jnp.max(lg_t, axis=0, keepdims=True))

        def acc_chunk(lg, mk, start, carry):
            l_sum, o_all = carry
            e = jnp.where(mk, jnp.exp(lg - m), 0.0)
            o_all = o_all + _dot(e.astype(BF16), vs_ref[pl.ds(start, ch), :], _TN)
            return l_sum + jnp.sum(e, axis=0, keepdims=True), o_all

        def acc_body(c, carry):
            lg, mk, start = masked_logits(c, far)
            return acc_chunk(lg, mk, start, carry)

        e_new = jnp.exp(l_new - m)
        row8 = lax.broadcasted_iota(jnp.int32, (8, HEAD_DIM), 0)
        e_new8 = jnp.where(row8 == 0, jnp.broadcast_to(e_new, (8, HEAD_DIM)), 0.0)
        o0 = _dot(e_new8.astype(BF16), vs_ref[past:past + 8, :], _TN)
        carry = lax.fori_loop(0, n_ch - 1, acc_body, (e_new, o0))
        l_sum, o_all = acc_chunk(lg_t, mk_t, start_t, carry)
        _head_rows_store(o_ref, o_all, l_sum)


def _paged_decode(page_table, pool2d, qbd, new_kv, bias, sel, *, n_pg, n_sub, k_sub, v_sub, blk, topk):
    n_dec, n_pages = page_table.shape
    past = n_pages * PAGE_SIZE
    assert past % DECODE_CHUNK == 0 and n_pages % n_pg == 0 and past % blk == 0
    n_blk = past // blk
    page_rows = PAGE_SIZE * n_sub
    in_specs = [pl.BlockSpec((page_rows, HEAD_DIM), functools.partial(lambda n, j, pt, p: (pt[n, j * n_pg + p], 0), p=p))
                for p in range(n_pg)]
    in_specs += [pl.BlockSpec((1, B_KV_HEADS, HEAD_DIM, HEAD_DIM), lambda n, j, pt: (n, 0, 0, 0)),
                 pl.BlockSpec((1, 2 * B_KV_HEADS, HEAD_DIM), lambda n, j, pt: (n, 0, 0)),
                 pl.BlockSpec(bias.shape, lambda n, j, pt: (0, 0))]
    args = [pool2d] * n_pg + [qbd, new_kv, bias]
    if topk is None:
        in_specs.append(pl.BlockSpec((1,) + sel.shape[1:], lambda n, j, pt: (n, 0, 0)))
        args.append(sel)
    return pl.pallas_call(
        functools.partial(_paged_decode_kernel, n_pg=n_pg, n_sub=n_sub, k_sub=k_sub, v_sub=v_sub, blk=blk,
                          topk=topk, n_pages=n_pages),
        grid_spec=pltpu.PrefetchScalarGridSpec(
            num_scalar_prefetch=1,
            grid=(n_dec, n_pages // n_pg),
            in_specs=in_specs,
            out_specs=pl.BlockSpec((1, B_HEADS, HEAD_DIM), lambda n, j, pt: (n, 0, 0)),
            scratch_shapes=[pltpu.VMEM((past + 8, HEAD_DIM), F32),
                            pltpu.VMEM((past + 8, B_KV_HEADS * HEAD_DIM), BF16),
                            pltpu.VMEM((n_blk, HEAD_DIM), F32)],
        ),
        out_shape=jax.ShapeDtypeStruct((n_dec, B_HEADS, HEAD_DIM), F32),
        compiler_params=_params(2),
        name="paged_decode",
    )(page_table, *args)


def _block_diag_q(q):
    n, h, d = q.shape
    qt = jnp.pad(jnp.swapaxes(q, 1, 2), ((0, 0), (0, 0), (0, HEAD_DIM - h)))
    grp = (np.arange(HEAD_DIM)[None, :] // GQA_GROUP == np.arange(B_KV_HEADS)[:, None]) & (np.arange(HEAD_DIM)[None, :] < h)
    return jnp.where(jnp.asarray(grp)[None, :, None, :], qt[:, None], 0.0).astype(BF16)


def _decode_bias(rel_bias, past, n_tail):
    dist = np.concatenate([n_tail - np.arange(n_tail), [MAX_DISTANCE * 8, 0], np.zeros(6, np.int64)])
    tab = rel_bias[jnp.asarray(_bucket_np(dist))]
    return jnp.pad(tab, ((0, 0), (0, HEAD_DIM - tab.shape[1])))


def _cmp1_paged_kernel(pt_ref, *refs, n_pg):
    page_refs = refs[:n_pg]
    w_ref, y_ref, ys_ref = refs[n_pg:]
    segs_per_page = PAGE_SIZE // C_CMP_STRIDE
    rows = segs_per_page * 8
    x2 = jnp.concatenate(
        [jnp.concatenate([ref[:, s, :, :].reshape(rows, HEAD_DIM) for s in range(C_CMP_STRIDE)], axis=1)
         for ref in page_refs], axis=0)
    y4 = _dot(x2.astype(BF16), w_ref[...])
    is_v = (lax.broadcasted_iota(jnp.int32, (x2.shape[0], 2 * HEAD_DIM), 0) % 8) >= C_KV_HEADS
    y = jnp.where(is_v, y4[:, 2 * HEAD_DIM:], y4[:, :2 * HEAD_DIM])
    ys_ref[0] = y[:, :HEAD_DIM]
    ys_ref[1] = y[:, HEAD_DIM:]
    for sub in range(8):
        y_ref[sub] = jnp.concatenate(
            [ys_ref[half, pl.ds(sub, n_pg * segs_per_page, stride=8), :] for half in range(2)], axis=1)


def _cmp1_paged(page_table, pool4d, w1cat4, *, sub_blk, n_pg):
    n_dec, n_pages = page_table.shape
    segs_per_page = PAGE_SIZE // C_CMP_STRIDE
    n_seg = n_pages * segs_per_page
    in_specs = [pl.BlockSpec((segs_per_page, C_CMP_STRIDE, 8, HEAD_DIM),
                             functools.partial(lambda n, j, pt, p: (pt[n, j * n_pg + p], 0, sub_blk, 0), p=p))
                for p in range(n_pg)]
    in_specs.append(pl.BlockSpec(w1cat4.shape, lambda n, j, pt: (0, 0)))
    return pl.pallas_call(
        functools.partial(_cmp1_paged_kernel, n_pg=n_pg),
        grid_spec=pltpu.PrefetchScalarGridSpec(
            num_scalar_prefetch=1,
            grid=(n_dec, n_pages // n_pg),
            in_specs=in_specs,
            out_specs=pl.BlockSpec((8, n_pg * segs_per_page, 2 * HEAD_DIM), lambda n, j, pt: (n, j, 0)),
            scratch_shapes=[pltpu.VMEM((2, n_pg * segs_per_page * 8, HEAD_DIM), F32)],
        ),
        out_shape=jax.ShapeDtypeStruct((n_dec * 8, n_seg, 2 * HEAD_DIM), F32),
        compiler_params=_params(2),
        name="cmp1_paged",
    )(page_table, *([pool4d] * n_pg), w1cat4)


def _nsa_decode_kernel(kvc_ref, qbd_ref, bias_c_ref, map_ref, gsum_ref, win_ref, new_ref, bias_w_ref,
                       oc_ref, ow_ref, sel_ref, *, n_cmp, n_blk, n_sub_w, k_sub_w):
    n_seg = kvc_ref.shape[1]
    sc = jnp.zeros((n_seg, HEAD_DIM), F32)
    for g in range(C_KV_HEADS):
        sc = sc + _dot(kvc_ref[g].astype(BF16), qbd_ref[0, g])
    valid = lax.broadcasted_iota(jnp.int32, (n_seg, HEAD_DIM), 0) < n_cmp
    lg = jnp.where(valid, sc * SOFTMAX_SCALE + bias_c_ref[...], NEG_INF)
    m = jnp.max(lg, axis=0, keepdims=True)
    e = jnp.where(valid, jnp.exp(lg - m), 0.0)
    p = e / jnp.maximum(jnp.sum(e, axis=0, keepdims=True), 1e-30)
    vc = jnp.concatenate([kvc_ref[C_KV_HEADS + g].astype(BF16) for g in range(C_KV_HEADS)], axis=1)
    _head_rows_store(oc_ref, _dot(p.astype(BF16), vc, _TN), jnp.ones((1, HEAD_DIM), F32))
    imp_h = _dot_f32_rhs(map_ref[...], p)
    imp = _dot_f32_lhs(imp_h, gsum_ref[...])
    blk = lax.broadcasted_iota(jnp.int32, imp.shape, 0)
    cur = n_blk - 1
    avail = blk <= cur
    forced = (blk == 0) | (blk == cur) | (blk == cur - 1)
    score = jnp.where(forced, jnp.inf, imp)
    score = jnp.where(avail, score, -jnp.inf)
    rank = _rank_rows(score, n_blk)
    sel_ref[0] = jnp.where(avail & (rank < min(C_SLC_TOPK, n_blk)), 1.0, 0.0)
    n_win = bias_w_ref.shape[0] - 8
    sw = jnp.zeros((n_win, HEAD_DIM), F32)
    s_new = jnp.zeros((8, HEAD_DIM), F32)
    v_parts, vn_parts = [], []
    for g in range(C_KV_HEADS):
        kg = win_ref[pl.ds(k_sub_w + g, n_win, stride=n_sub_w), :].astype(BF16)
        sw = sw + _dot(kg, qbd_ref[0, g])
        v_parts.append(win_ref[pl.ds(k_sub_w + C_KV_HEADS + g, n_win, stride=n_sub_w), :].astype(BF16))
        k_new = jnp.broadcast_to(new_ref[0, g:g + 1, :], (8, HEAD_DIM)).astype(BF16)
        s_new = s_new + _dot(k_new, qbd_ref[0, g])
        vn_parts.append(jnp.broadcast_to(new_ref[0, C_KV_HEADS + g:C_KV_HEADS + g + 1, :], (8, HEAD_DIM)).astype(BF16))
    lw = sw * SOFTMAX_SCALE + bias_w_ref[0:n_win, :]
    l_new = s_new[0:1] * SOFTMAX_SCALE + bias_w_ref[n_win + 1:n_win + 2, :]
    mw = jnp.maximum(jnp.max(lw, axis=0, keepdims=True), l_new)
    ew = jnp.exp(lw - mw)
    e_new = jnp.exp(l_new - mw)
    row8 = lax.broadcasted_iota(jnp.int32, (8, HEAD_DIM), 0)
    e_new8 = jnp.where(row8 == 0, jnp.broadcast_to(e_new, (8, HEAD_DIM)), 0.0)
    o_all = _dot(ew.astype(BF16), jnp.concatenate(v_parts, axis=1), _TN)
    o_all = o_all + _dot(e_new8.astype(BF16), jnp.concatenate(vn_parts, axis=1), _TN)
    _head_rows_store(ow_ref, o_all, jnp.sum(ew, axis=0, keepdims=True) + e_new)


def _nsa_decode(kvc, qbd, bias_c, cmap_t, gsum, win2d, new_win, bias_w, *, slot, n_blk):
    n_dec = qbd.shape[0]
    n_seg = kvc.shape[1]
    n_win = bias_w.shape[0] - 8
    n_blk_pad = cmap_t.shape[0]
    win_rows = n_win * 8
    o_sds = jax.ShapeDtypeStruct((n_dec, C_HEADS, HEAD_DIM), F32)
    o_spec = pl.BlockSpec((1, C_HEADS, HEAD_DIM), lambda n: (n, 0, 0))
    return pl.pallas_call(
        functools.partial(_nsa_decode_kernel, n_cmp=n_seg - 1, n_blk=n_blk, n_sub_w=8, k_sub_w=0),
        grid=(n_dec,),
        in_specs=[pl.BlockSpec((8, n_seg, HEAD_DIM), lambda n: (n, 0, 0)),
                  pl.BlockSpec((1, C_KV_HEADS, HEAD_DIM, HEAD_DIM), lambda n: (n, 0, 0, 0)),
                  pl.BlockSpec(bias_c.shape, lambda n: (0, 0)),
                  pl.BlockSpec(cmap_t.shape, lambda n: (0, 0)),
                  pl.BlockSpec(gsum.shape, lambda n: (0, 0)),
                  pl.BlockSpec((win_rows, HEAD_DIM), lambda n: (slot * n_dec + n, 0)),
                  pl.BlockSpec((1, 2 * C_KV_HEADS, HEAD_DIM), lambda n: (n, 0, 0)),
                  pl.BlockSpec(bias_w.shape, lambda n: (0, 0))],
        out_specs=[o_spec, o_spec, pl.BlockSpec((1, n_blk_pad, HEAD_DIM), lambda n: (n, 0, 0))],
        out_shape=[o_sds, o_sds, jax.ShapeDtypeStruct((n_dec, n_blk_pad, HEAD_DIM), F32)],
        compiler_params=_params(1),
        name="nsa_decode",
    )(kvc, qbd, bias_c, cmap_t, gsum, win2d, new_win, bias_w)


def _mixer_a_decode(x, g_mix, li, w_in, w_out, slot, bufs, rel_bias):
    n_dec = x.shape[0]
    n_in = w_in.shape[2]
    proj = _norm_matmul(x, g_mix, li, w_in, slot, n_in, n_dec, 512)
    qkv = proj.reshape(n_dec, len(A_PATTERNS), 3, A_HEADS, HEAD_DIM)
    dist = np.stack([np.concatenate([(w // d - np.arange(w // d)) * d, [0]]) for w, d in A_PATTERNS])
    bias = _bias_by_dist(rel_bias[:, :A_HEADS], dist)
    bias = jnp.broadcast_to(jnp.transpose(bias, (1, 2, 0))[..., None], dist.shape + (A_HEADS, HEAD_DIM))
    o = _a_decode(qkv, bufs, slot, bias)
    x = _outproj([o.reshape(n_dec, -1)], [], w_out, slot, x, "plain", n_dec, 512)
    return x, proj


def _mixer_b_decode(x, g_mix, li, w_in, w_out, slot, pool, page_table, bias_dec):
    n_dec = x.shape[0]
    n_in = w_in.shape[2]
    proj = _norm_matmul(x, g_mix, li, w_in, slot, n_in, n_dec, 512)
    nq = B_HEADS * HEAD_DIM
    qbd = _block_diag_q(proj[:, :nq].reshape(n_dec, B_HEADS, HEAD_DIM))
    new_kv = proj[:, nq:].reshape(n_dec, 2 * B_KV_HEADS, HEAD_DIM)
    n_sub = pool.shape[2] * 2 * B_KV_HEADS
    o = _paged_decode(page_table, pool.reshape(-1, HEAD_DIM), qbd, new_kv, bias_dec, None, n_pg=8, n_sub=n_sub,
                      k_sub=slot * 2 * B_KV_HEADS, v_sub=slot * 2 * B_KV_HEADS + B_KV_HEADS, blk=B_BLOCK, topk=B_TOPK)
    x = _outproj([o.reshape(n_dec, -1)], [], w_out, slot, x, "plain", n_dec, 512)
    return x, proj


def _mixer_c_decode(x, g_mix, li, w_in, w_gate, w_out, slot, cmp_w, pool, win_buf, page_table, bias_dec, rel_bias):
    n_dec = x.shape[0]
    n_pages = page_table.shape[1]
    past = n_pages * PAGE_SIZE
    assert win_buf.shape[2] == C_WINDOW, "window buffer must hold exactly C_WINDOW past rows"
    n_main = (C_HEADS + 6 * C_KV_HEADS) * HEAD_DIM
    proj = _norm_matmul(x, g_mix, li, w_in, slot, n_main, n_dec, 512)
    gate = _norm_matmul(x, g_mix, li, w_gate, 0, HEAD_DIM, n_dec, HEAD_DIM)
    nq = C_HEADS * HEAD_DIM
    kvw = C_KV_HEADS * HEAD_DIM
    qbd = _block_diag_q(proj[:, :nq].reshape(n_dec, C_HEADS, HEAD_DIM))
    new_slc = proj[:, nq + 2 * kvw:nq + 4 * kvw].reshape(n_dec, 2 * C_KV_HEADS, HEAD_DIM)
    new_win = proj[:, nq + 4 * kvw:nq + 6 * kvw].reshape(n_dec, 2 * C_KV_HEADS, HEAD_DIM)
    w1cat, pe8, w1, w2 = cmp_w
    w1cat4 = jnp.concatenate([w1cat[0], w1cat[1]], axis=1)
    n_layers = pool.shape[2]
    n_sub = n_layers * 4 * C_KV_HEADS
    pool4d = pool.reshape(-1, C_CMP_STRIDE, n_sub, HEAD_DIM)
    y = _cmp1_paged(page_table, pool4d, w1cat4, sub_blk=slot * 2, n_pg=8)
    kvc = _cmp2(y, pe8, w1, w2)
    n_seg = kvc.shape[1]
    n_cmp = n_seg - 1
    n_blk = past // C_SLC_BLOCK + 1
    n_blk_pad = -(-n_blk // 8) * 8
    dist_c = past - (np.arange(n_seg) * C_CMP_STRIDE + (C_CMP_BLOCK - 1))
    bias_c = jnp.pad(rel_bias[jnp.asarray(_bucket_np(dist_c))], ((0, 0), (0, HEAD_DIM - C_HEADS)))
    cmap_t = jnp.asarray(_cmp_map_np(n_seg, n_cmp, n_blk_pad).T, BF16)
    lanes = np.arange(HEAD_DIM)
    gsum = jnp.asarray((lanes[:, None] // GQA_GROUP == lanes[None, :] // GQA_GROUP)
                       & (lanes[:, None] < C_HEADS) & (lanes[None, :] < C_HEADS), BF16)
    bias_w = _decode_bias(rel_bias, past, C_WINDOW)
    o_cmp, o_win, sel = _nsa_decode(kvc, qbd, bias_c, cmap_t, gsum, win_buf.reshape(-1, HEAD_DIM), new_win, bias_w,
                                    slot=slot, n_blk=n_blk)
    o_slc = _paged_decode(page_table, pool.reshape(-1, HEAD_DIM), qbd, new_slc, bias_dec, sel, n_pg=4, n_sub=n_sub,
                          k_sub=slot * 16 + 2 * C_KV_HEADS, v_sub=slot * 16 + 3 * C_KV_HEADS, blk=C_SLC_BLOCK, topk=None)
    x = _outproj([o_cmp.reshape(n_dec, -1), o_slc.reshape(n_dec, -1), o_win.reshape(n_dec, -1)], [gate], w_out, slot, x,
                 "gate3", n_dec, 512)
    return x, proj


def _shifted(buf, new_rows):
    return jnp.concatenate([buf[:, :, 1:], new_rows[:, :, None]], axis=2)


def kernel(x_prompt, x_sample, state_a_w128, state_a_w512, state_a_w2048, cache_b_kv, cache_c_kv, state_c_win, page_table, rel_bias, norm_mix, norm_ffn, norm_final, w_in_a, w_out_a, w_in_b, w_out_b, w_in_c, w_out_c, cmp_pe, cmp_w1, cmp_w2, w_up, w_down):
    b, s, d = x_prompt.shape
    n_dec, t_dec, _ = x_sample.shape
    depth = norm_mix.shape[0]
    assert t_dec == 1 and s % B_BLOCK == 0 and s % ATTN_TILE == 0
    past = page_table.shape[1] * PAGE_SIZE
    assert past % B_BLOCK == 0 and past % C_SLC_BLOCK == 0

    xp = x_prompt.reshape(b * s, d)
    xs = x_sample.reshape(n_dec, d)
    g_mix = norm_mix.reshape(depth, 1, d)
    g_ffn = norm_ffn.reshape(depth, 1, d)
    g_fin = norm_final.reshape(1, d)
    tabs = _tile_tables(rel_bias, ATTN_TILE)
    tabs_win = _tile_tables(rel_bias, ATTN_TILE, window=C_WINDOW)
    tabs_a = [_tile_tables(rel_bias[:, :A_HEADS], ATTN_TILE, window=w // dl, dist_scale=dl) for w, dl in A_PATTERNS]
    far = rel_bias[NUM_BUCKETS - 1]
    bias_dec = _decode_bias(rel_bias, past, DECODE_CHUNK)
    n_gate = 3 * C_HEADS
    n_main_c = (C_HEADS + 6 * C_KV_HEADS) * HEAD_DIM
    w_gate = jnp.pad(w_in_c[:, :, n_main_c:], ((0, 0), (0, 0), (0, HEAD_DIM - n_gate)))
    a_bufs = (state_a_w128, state_a_w512, state_a_w2048)

    a_p, a_s = [[], [], []], [[], [], []]
    b_p, b_s, c_p, c_s, cw_p, cw_s = [], [], [], [], [], []
    for li in range(depth):
        kind, slot = li % 3, li // 3
        if kind == 0:
            xp, pp = _mixer_a_prompt(xp, b, g_mix, li, w_in_a, w_out_a, slot, tabs_a, far)
            xs, ps = _mixer_a_decode(xs, g_mix, li, w_in_a, w_out_a, slot, a_bufs, rel_bias)
            gw = 3 * A_HEADS * HEAD_DIM
            kvw = 2 * A_HEADS * HEAD_DIM
            for g, (window, _) in enumerate(A_PATTERNS):
                c0 = g * gw + A_HEADS * HEAD_DIM
                a_p[g].append(pp[:, s - min(window, s):, c0:c0 + kvw].reshape(b, min(window, s), 2, A_HEADS, HEAD_DIM))
                a_s[g].append(ps[:, c0:c0 + kvw].reshape(n_dec, 2, A_HEADS, HEAD_DIM))
        elif kind == 1:
            xp, pp = _mixer_b_prompt(xp, b, g_mix, li, w_in_b, w_out_b, slot, tabs, far)
            xs, ps = _mixer_b_decode(xs, g_mix, li, w_in_b, w_out_b, slot, cache_b_kv, page_table, bias_dec)
            nq = B_HEADS * HEAD_DIM
            b_p.append(pp[:, :, nq:].reshape(b, s, 2, B_KV_HEADS, HEAD_DIM))
            b_s.append(ps[:, nq:].reshape(n_dec, 1, 2, B_KV_HEADS, HEAD_DIM))
        else:
            cmp_w = _cmp_weights(cmp_pe, cmp_w1, cmp_w2, slot)
            xp, pp = _mixer_c_prompt(xp, b, g_mix, li, w_in_c, w_gate[slot:slot + 1], w_out_c, slot, cmp_w, tabs, tabs_win,
                                     far, rel_bias)
            xs, ps = _mixer_c_decode(xs, g_mix, li, w_in_c, w_gate[slot:slot + 1], w_out_c, slot, cmp_w, cache_c_kv,
                                     state_c_win, page_table, bias_dec, rel_bias)
            nq = C_HEADS * HEAD_DIM
            kvw = C_KV_HEADS * HEAD_DIM
            c_p.append(pp[:, :, nq:nq + 4 * kvw].reshape(b, s, 4, C_KV_HEADS, HEAD_DIM))
            c_s.append(ps[:, nq:nq + 4 * kvw].reshape(n_dec, 1, 4, C_KV_HEADS, HEAD_DIM))
            wlen = min(C_WINDOW, s)
            cw_p.append(pp[:, s - wlen:, nq + 4 * kvw:nq + 6 * kvw].reshape(b, wlen, 2, C_KV_HEADS, HEAD_DIM))
            cw_s.append(ps[:, nq + 4 * kvw:nq + 6 * kvw].reshape(n_dec, 2, C_KV_HEADS, HEAD_DIM))
        last = li == depth - 1
        xp = _mlp(xp, g_ffn, li, w_up, w_down, g_fin if last else None, _row_tile(b * s, PROMPT_ROW_TILE), 512)
        xs = _mlp(xs, g_ffn, li, w_up, w_down, g_fin if last else None, n_dec, 512)

    outs = [xp.reshape(b, s, d), xs.reshape(n_dec, 1, d)]
    for g in range(len(A_PATTERNS)):
        outs.append(jnp.stack(a_p[g], axis=0))
        outs.append(_shifted(a_bufs[g], jnp.stack(a_s[g], axis=0)))
    outs += [jnp.stack(b_p, axis=2), jnp.stack(b_s, axis=2), jnp.stack(c_p, axis=2), jnp.stack(c_s, axis=2),
             jnp.stack(cw_p, axis=0), _shifted(state_c_win, jnp.stack(cw_s, axis=0))]
    return tuple(outs)
```

```python
import functools
import math

import jax
import jax.numpy as jnp
import numpy as np
from jax import lax
from jax.experimental import pallas as pl
from jax.experimental.pallas import tpu as pltpu

F32 = jnp.float32
BF16 = jnp.bfloat16

HEAD_DIM = 128
RMS_EPS = 1e-6
NEG_INF = -1e30
SOFTMAX_SCALE = HEAD_DIM ** -0.5
PAGE_SIZE = 128

NUM_BUCKETS = 32
MAX_DISTANCE = 128

A_PATTERNS = ((128, 1), (512, 4), (2048, 16))
A_HEADS = 8
B_HEADS = 16
B_KV_HEADS = 4
B_BLOCK = 256
B_TOPK = 3
C_HEADS = 16
C_KV_HEADS = 4
C_CMP_BLOCK = 32
C_CMP_STRIDE = 16
C_SLC_BLOCK = 64
C_SLC_TOPK = 16
C_WINDOW = 512
GQA_GROUP = 4

ATTN_TILE = 256
PROMPT_ROW_TILE = 1024
VMEM_LIMIT_BYTES = 56 * 1024 * 1024


def _params(n_axes, vmem=VMEM_LIMIT_BYTES):
    return pltpu.CompilerParams(dimension_semantics=("arbitrary",) * n_axes, vmem_limit_bytes=vmem)


def _bucket_np(dist):
    exact = NUM_BUCKETS // 2
    d = np.maximum(np.asarray(dist, np.int64), 0)
    far = exact + (np.log(np.maximum(d, exact).astype(np.float64) / exact)
                   / math.log(MAX_DISTANCE / exact) * (NUM_BUCKETS - exact)).astype(np.int64)
    return np.where(d < exact, d, np.minimum(far, NUM_BUCKETS - 1)).astype(np.int32)


def _bias_by_dist(rel_bias, dist_np):
    tab = rel_bias[jnp.asarray(_bucket_np(dist_np))]
    return jnp.moveaxis(tab, -1, 0)


def _toeplitz(w, t):
    return jnp.tile(w, (1, t))[:, :2 * t * t].reshape(w.shape[0], t, 2 * t)[:, :, :t]


def _tile_tables(rel_bias, t, window=None, dil=1, dist_scale=1):
    assert window is None or window < t or (window % t == 0 and window >= 2 * t)
    k = np.arange(2 * t + 1)
    e = np.where(k <= t, -k, 2 * t + 1 - k)

    def table(dist, ok):
        vals = rel_bias[jnp.asarray(_bucket_np(dist * dist_scale))].T
        return _toeplitz(jnp.where(jnp.asarray(ok)[None], vals, NEG_INF), t)

    def allowed(dist):
        ok = dist >= 0
        if window is not None:
            ok &= dist <= window
        if dil > 1:
            ok &= dist % dil == 0
        return ok

    tabs = [table(e, allowed(e)), table(t + e, allowed(t + e))]
    if dil > 1:
        tabs.append(table(2 * t + e, (2 * t + e) % dil == 0))
    if window is not None and window >= 2 * t:
        tabs.append(table(window + e, allowed(window + e)))
    return jnp.stack(tabs, axis=1)


def _rms(x, g):
    return x * lax.rsqrt(jnp.mean(x * x, axis=-1, keepdims=True) + RMS_EPS) * g


def _split_bf16(x, parts):
    out = []
    for _ in range(parts):
        p = x.astype(BF16)
        out.append(p)
        x = x - p.astype(F32)
    return out


def _dot(a, b, dims=(((1,), (0,)), ((), ()))):
    return lax.dot_general(a, b, dims, preferred_element_type=F32)


_NT = (((1,), (1,)), ((), ()))
_TN = (((0,), (0,)), ((), ()))


def _dot_f32_lhs(a, b_bf16, dims=(((1,), (0,)), ((), ()))):
    return sum(_dot(p, b_bf16, dims) for p in _split_bf16(a, 3))


def _dot_f32_rhs(a_bf16, b, dims=(((1,), (0,)), ((), ()))):
    return sum(_dot(a_bf16, p, dims) for p in _split_bf16(b, 3))


def _dot_f32_both(a, b, dims):
    a_hi, a_lo = _split_bf16(a, 2)
    b_hi, b_lo = _split_bf16(b, 2)
    return _dot(a_hi, b_hi, dims) + (_dot(a_hi, b_lo, dims) + _dot(a_lo, b_hi, dims))


def _rank_rows(score, n_rows):
    row = lax.broadcasted_iota(jnp.int32, score.shape, 0)
    rank = jnp.zeros(score.shape, jnp.int32)
    for m in range(n_rows):
        sm = score[m:m + 1, :]
        beats = (sm > score) | ((sm == score) & (row > m))
        rank = rank + beats.astype(jnp.int32)
    return rank


def _rank_lanes(score, n_lanes):
    n_rows, width = score.shape
    n_pad = -(-n_lanes // 8) * 8
    rank_t = _rank_rows(score.T[:n_pad], n_lanes).astype(F32)
    if n_pad < width:
        rank_t = jnp.concatenate([rank_t, jnp.zeros((width - n_pad, n_rows), F32)], axis=0)
    return rank_t.T


def _norm_matmul_kernel(x_ref, g_ref, w_ref, o_ref, h_ref):
    @pl.when(pl.program_id(1) == 0)
    def _():
        h_ref[...] = _rms(x_ref[...], g_ref[...]).astype(BF16)

    o_ref[...] = _dot(h_ref[...], w_ref[...].astype(BF16))


def _norm_matmul(x, g_all, li, w_all, slot, n_cols, tm, tn):
    m, k = x.shape
    return pl.pallas_call(
        _norm_matmul_kernel,
        grid=(m // tm, n_cols // tn),
        in_specs=[
            pl.BlockSpec((tm, k), lambda i, j: (i, 0), pipeline_mode=pl.Buffered(1)),
            pl.BlockSpec((None, 1, k), lambda i, j: (li, 0, 0)),
            pl.BlockSpec((None, k, tn), lambda i, j: (slot, 0, j)),
        ],
        out_specs=pl.BlockSpec((tm, tn), lambda i, j: (i, j)),
        out_shape=jax.ShapeDtypeStruct((m, n_cols), F32),
        scratch_shapes=[pltpu.VMEM((tm, k), BF16)],
        compiler_params=_params(2),
        name="norm_matmul",
    )(x, g_all, w_all)


def _outproj_kernel(*refs, mode, n_heads):
    if mode == "plain":
        o_refs, extra, (w_ref, x_ref, out_ref, a_ref) = refs[:1], (), refs[1:]
    elif mode == "lse3":
        o_refs, extra, (w_ref, x_ref, out_ref, a_ref) = refs[:3], refs[3:6], refs[6:]
    else:
        o_refs, extra, (w_ref, x_ref, out_ref, a_ref) = refs[:3], refs[3:4], refs[4:]

    @pl.when(pl.program_id(1) == 0)
    def _():
        if mode == "plain":
            a = o_refs[0][...]
        elif mode == "lse3":
            lse = [r[...] for r in extra]
            mx = jnp.maximum(jnp.maximum(lse[0], lse[1]), lse[2])
            e = [jnp.exp(l - mx) for l in lse]
            den = e[0] + e[1] + e[2]
            a = sum((e[g] / den) * o_refs[g][...] for g in range(3))
        else:
            gate = jax.nn.sigmoid(extra[0][...])
            tm = gate.shape[0]
            heads = []
            for h in range(n_heads):
                acc = jnp.zeros((tm, HEAD_DIM), F32)
                for br in range(3):
                    gcol = gate[:, 3 * h + br:3 * h + br + 1]
                    acc = acc + gcol * o_refs[br][:, h * HEAD_DIM:(h + 1) * HEAD_DIM]
                heads.append(acc)
            a = jnp.concatenate(heads, axis=1)
        a_ref[...] = a.astype(BF16)

    out_ref[...] = x_ref[...] + _dot(a_ref[...], w_ref[...].astype(BF16))


def _outproj(o_list, extra_list, w_all, slot, x, mode, tm, tn):
    m, k = o_list[0].shape
    n = x.shape[1]
    row_spec = lambda width: pl.BlockSpec((tm, width), lambda i, j: (i, 0))
    in_specs = [row_spec(k) for _ in o_list] + [row_spec(e.shape[1]) for e in extra_list]
    in_specs += [pl.BlockSpec((None, k, tn), lambda i, j: (slot, 0, j)),
                 pl.BlockSpec((tm, tn), lambda i, j: (i, j))]
    return pl.pallas_call(
        functools.partial(_outproj_kernel, mode=mode, n_heads=k // HEAD_DIM),
        grid=(m // tm, n // tn),
        in_specs=in_specs,
        out_specs=pl.BlockSpec((tm, tn), lambda i, j: (i, j)),
        out_shape=jax.ShapeDtypeStruct((m, n), F32),
        scratch_shapes=[pltpu.VMEM((tm, k), BF16)],
        compiler_params=_params(2),
        name="outproj_" + mode,
    )(*o_list, *extra_list, w_all, x)


def _mlp_kernel(*refs, final):
    if final:
        x_ref, g_ref, wu_ref, wd_ref, gf_ref, o_ref, h_ref = refs
    else:
        x_ref, g_ref, wu_ref, wd_ref, o_ref, h_ref = refs
    f = pl.program_id(1)

    @pl.when(f == 0)
    def _():
        x = x_ref[...]
        h_ref[...] = _rms(x, g_ref[...]).astype(BF16)
        o_ref[...] = x

    u = _dot(h_ref[...], wu_ref[...].astype(BF16))
    act = jnp.square(jnp.maximum(u, 0.0)).astype(BF16)
    o_ref[...] += _dot(act, wd_ref[...].astype(BF16))

    if final:
        @pl.when(f == pl.num_programs(1) - 1)
        def _():
            o_ref[...] = _rms(o_ref[...], gf_ref[...])


def _mlp(x, g_all, li, w_up, w_down, final_g, tm, tf):
    m, d = x.shape
    d_ff = w_up.shape[2]
    in_specs = [
        pl.BlockSpec((tm, d), lambda i, f: (i, 0), pipeline_mode=pl.Buffered(1)),
        pl.BlockSpec((None, 1, d), lambda i, f: (li, 0, 0)),
        pl.BlockSpec((None, d, tf), lambda i, f: (li, 0, f)),
        pl.BlockSpec((None, tf, d), lambda i, f: (li, f, 0)),
    ]
    args = [x, g_all, w_up, w_down]
    if final_g is not None:
        in_specs.append(pl.BlockSpec((1, d), lambda i, f: (0, 0)))
        args.append(final_g)
    return pl.pallas_call(
        functools.partial(_mlp_kernel, final=final_g is not None),
        grid=(m // tm, d_ff // tf),
        in_specs=in_specs,
        out_specs=pl.BlockSpec((tm, d), lambda i, f: (i, 0)),
        out_shape=jax.ShapeDtypeStruct((m, d), F32),
        scratch_shapes=[pltpu.VMEM((tm, d), BF16)],
        compiler_params=_params(2),
        name="mlp",
    )(*args)


def _flash_kernel_old(*refs, group, t, window, dil, sel_blk, sel_per_head, want_lse, head0):
    it = iter(refs)
    q_ref, k_ref, v_ref, tab_ref, far_ref = (next(it) for _ in range(5))
    sel_ref = next(it) if sel_blk else None
    o_ref = next(it)
    lse_ref = next(it) if want_lse else None
    kb_ref, vb_ref, qs_ref, m_ref, l_ref, acc_ref = (next(it) for _ in range(6))

    kvh = pl.program_id(1)
    qi = pl.program_id(2)
    rows = group * t

    @pl.when(qi == 0)
    def _():
        kb_ref[...] = k_ref[0].astype(BF16)
        vb_ref[...] = v_ref[0].astype(BF16)

    for j in range(group):
        qs_ref[j * t:(j + 1) * t, :] = q_ref[0, :, j * HEAD_DIM:(j + 1) * HEAD_DIM].astype(BF16)
    m_ref[...] = jnp.full((rows, HEAD_DIM), NEG_INF, F32)
    l_ref[...] = jnp.zeros((rows, HEAD_DIM), F32)
    acc_ref[...] = jnp.zeros((rows, HEAD_DIM), F32)

    delta = (lax.broadcasted_iota(jnp.int32, (t, t), 0) - lax.broadcasted_iota(jnp.int32, (t, t), 1))

    def tile(kj, kind):
        start = pl.multiple_of(kj * t, t)
        k = kb_ref[pl.ds(start, t), :]
        v = vb_ref[pl.ds(start, t), :]
        s = _dot(qs_ref[...], k, _NT) * SOFTMAX_SCALE
        dist = delta + (qi - kj) * t
        ok = None
        if kind == "diag":
            ok = dist >= 0
        if window is not None:
            w_ok = dist <= window
            ok = w_ok if ok is None else ok & w_ok
        if dil > 1:
            d_ok = (dist & (dil - 1)) == 0
            ok = d_ok if ok is None else ok & d_ok
        sel_ok = None
        if sel_blk:
            blk_of_key = (start + lax.broadcasted_iota(jnp.int32, (HEAD_DIM, t), 1)) // sel_blk
            expand = (lax.broadcasted_iota(jnp.int32, (HEAD_DIM, t), 0) == blk_of_key)
            expand = jnp.where(expand, 1.0, 0.0).astype(BF16)
            if sel_per_head:
                sel_rows = jnp.concatenate([sel_ref[0, j] for j in range(group)], axis=0)
            else:
                sel_rows = sel_ref[0, 0]
            sel_ok = _dot(sel_rows.astype(BF16), expand) > 0.5
        pieces = []
        for j in range(group):
            sj = s[j * t:(j + 1) * t]
            if kind == "far":
                sj = sj + far_ref[head0 + kvh * group + j]
            else:
                sj = sj + tab_ref[j, 0 if kind == "diag" else 1]
            mask = ok
            if sel_blk:
                sel_j = sel_ok[j * t:(j + 1) * t] if sel_per_head else sel_ok
                mask = sel_j if mask is None else mask & sel_j
            pieces.append((sj, mask))
        if pieces[0][1] is not None:
            s = jnp.concatenate([jnp.where(mk, sj, NEG_INF) for sj, mk in pieces], axis=0)
            mask_all = jnp.concatenate([mk for _, mk in pieces], axis=0)
        else:
            s = jnp.concatenate([sj for sj, _ in pieces], axis=0)
            mask_all = None
        m_prev = m_ref[...]
        m_next = jnp.maximum(m_prev, jnp.max(s, axis=1, keepdims=True))
        p = jnp.exp(s - jnp.concatenate([m_next] * (t // HEAD_DIM), axis=1))
        if mask_all is not None:
            p = jnp.where(mask_all, p, 0.0)
        alpha = jnp.exp(m_prev - m_next)
        l_ref[...] = alpha * l_ref[...] + jnp.sum(p, axis=1, keepdims=True)
        acc_ref[...] = alpha * acc_ref[...] + _dot(p.astype(BF16), v)
        m_ref[...] = m_next

    lo = 0 if window is None else jnp.maximum(qi * t - window, 0) // t

    def far_body(kj, carry):
        tile(kj, "far")
        return carry

    lax.fori_loop(lo, jnp.maximum(qi - 1, lo), far_body, 0)

    @pl.when(qi - 1 >= lo)
    def _():
        tile(qi - 1, "near")

    tile(qi, "diag")

    l = l_ref[...]
    out = acc_ref[...] / l
    for j in range(group):
        o_ref[0, :, j * HEAD_DIM:(j + 1) * HEAD_DIM] = out[j * t:(j + 1) * t]
    if want_lse:
        lse = m_ref[...] + jnp.log(l)
        for j in range(group):
            lse_ref[0, :, j * HEAD_DIM:(j + 1) * HEAD_DIM] = lse[j * t:(j + 1) * t]


def _flash_old(proj, tabs, far, sel, *, n_kv, group, q_col, k_col, v_col, head0=0,
           window=None, dil=1, sel_blk=0, sel_per_head=False, want_lse=False):
    b, s, _ = proj.shape
    t = ATTN_TILE
    gw = group * HEAD_DIM
    in_specs = [
        pl.BlockSpec((1, t, gw), lambda bi, h, qi: (bi, qi, q_col // group + h)),
        pl.BlockSpec((1, s, HEAD_DIM), lambda bi, h, qi: (bi, 0, k_col + h)),
        pl.BlockSpec((1, s, HEAD_DIM), lambda bi, h, qi: (bi, 0, v_col + h)),
        pl.BlockSpec((group, 2, t, t), lambda bi, h, qi: (head0 // group + h, 0, 0, 0)),
        pl.BlockSpec(memory_space=pltpu.SMEM),
    ]
    args = [proj, proj, proj, tabs, far]
    if sel_blk:
        if sel_per_head:
            in_specs.append(pl.BlockSpec((1, group, t, HEAD_DIM), lambda bi, h, qi: (bi, h, qi, 0)))
        else:
            in_specs.append(pl.BlockSpec((1, 1, t, HEAD_DIM), lambda bi, h, qi: (bi, h, qi, 0)))
        args.append(sel)
    out_spec = pl.BlockSpec((1, t, gw), lambda bi, h, qi: (bi, qi, h))
    out_sds = jax.ShapeDtypeStruct((b, s, n_kv * gw), F32)
    rows = group * t
    return pl.pallas_call(
        functools.partial(_flash_kernel, group=group, t=t, window=window, dil=dil, sel_blk=sel_blk,
                          sel_per_head=sel_per_head, want_lse=want_lse, head0=head0),
        grid=(b, n_kv, s // t),
        in_specs=in_specs,
        out_specs=[out_spec, out_spec] if want_lse else out_spec,
        out_shape=[out_sds, out_sds] if want_lse else out_sds,
        scratch_shapes=[pltpu.VMEM((s, HEAD_DIM), BF16), pltpu.VMEM((s, HEAD_DIM), BF16),
                        pltpu.VMEM((rows, HEAD_DIM), BF16), pltpu.VMEM((rows, HEAD_DIM), F32),
                        pltpu.VMEM((rows, HEAD_DIM), F32), pltpu.VMEM((rows, HEAD_DIM), F32)],
        compiler_params=_params(3),
        name="flash",
    )(*args)


def _flash_kernel(*refs, group, t, w_tiles, far_tab, edge_tab, sel_blk, sel_per_head, want_lse, head0):
    it = iter(refs)
    q_ref, k_ref, v_ref, tab_ref, far_ref = (next(it) for _ in range(5))
    sel_ref = next(it) if sel_blk else None
    o_ref = next(it)
    lse_ref = next(it) if want_lse else None
    kb_ref, vb_ref, qs_ref, s_ref, m_ref, l_ref, acc_ref = (next(it) for _ in range(7))

    kvh = pl.program_id(1)
    qi = pl.program_id(2)
    rows = group * t

    @pl.when(qi == 0)
    def _():
        kb_ref[...] = k_ref[0].astype(BF16)
        vb_ref[...] = v_ref[0].astype(BF16)

    for j in range(group):
        qs_ref[j * t:(j + 1) * t, :] = q_ref[0, :, j * HEAD_DIM:(j + 1) * HEAD_DIM].astype(BF16)

    def tile(kj, tab, first=False):
        start = pl.multiple_of(kj * t, t)
        k = kb_ref[pl.ds(start, t), :]
        v = vb_ref[pl.ds(start, t), :]
        sel_add = None
        if sel_blk and not sel_per_head:
            blk_of_key = (start + lax.broadcasted_iota(jnp.int32, (HEAD_DIM, t), 1)) // sel_blk
            expand = lax.broadcasted_iota(jnp.int32, (HEAD_DIM, t), 0) == blk_of_key
            expand = jnp.where(expand, 1.0, 0.0).astype(BF16)
            sel_add = _dot(sel_ref[0, 0].astype(BF16), expand)
        for j in range(group):
            hs = slice(j * t, (j + 1) * t)
            sj = _dot(qs_ref[hs, :], k, _NT) * SOFTMAX_SCALE
            sj = sj + (far_ref[head0 + kvh * group + j] if tab is None else tab_ref[j, tab])
            if sel_blk and sel_per_head:
                assert sel_blk == t
                lane = lax.broadcasted_iota(jnp.int32, (t, HEAD_DIM), 1)
                sj = sj + jnp.sum(jnp.where(lane == kj, sel_ref[0, j], 0.0), axis=1, keepdims=True)
            elif sel_blk:
                sj = sj + sel_add
            m_cur = jnp.max(jnp.maximum(sj[:, :HEAD_DIM], sj[:, HEAD_DIM:]), axis=1, keepdims=True)
            if first:
                m_next = jnp.broadcast_to(m_cur, (t, HEAD_DIM))
            else:
                m_prev = m_ref[hs, :]
                m_next = jnp.maximum(m_prev, m_cur)
            p = jnp.exp(sj - jnp.concatenate([m_next] * (t // HEAD_DIM), axis=1))
            p_sum = jnp.sum(p[:, :HEAD_DIM] + p[:, HEAD_DIM:], axis=1, keepdims=True)
            pv = _dot(p.astype(BF16), v)
            if first:
                l_ref[hs, :] = jnp.broadcast_to(p_sum, (t, HEAD_DIM))
                acc_ref[hs, :] = pv
            else:
                alpha = jnp.exp(m_prev - m_next)
                l_ref[hs, :] = alpha * l_ref[hs, :] + p_sum
                acc_ref[hs, :] = alpha * acc_ref[hs, :] + pv
            m_ref[hs, :] = m_next

    tile(qi, 0, first=True)

    @pl.when(qi >= 1)
    def _():
        tile(qi - 1, 1)

    def far_body(kj, carry):
        tile(kj, far_tab)
        return carry

    if w_tiles is None:
        lax.fori_loop(0, qi - 1, far_body, 0)
    elif w_tiles >= 2:
        lax.fori_loop(jnp.maximum(qi - w_tiles + 1, 0), qi - 1, far_body, 0)

        @pl.when(qi >= w_tiles)
        def _():
            tile(qi - w_tiles, edge_tab)

    l = l_ref[...]
    out = acc_ref[...] / l
    for j in range(group):
        o_ref[0, :, j * HEAD_DIM:(j + 1) * HEAD_DIM] = out[j * t:(j + 1) * t]
    if want_lse:
        lse = m_ref[...] + jnp.log(l)
        for j in range(group):
            lse_ref[0, :, j * HEAD_DIM:(j + 1) * HEAD_DIM] = lse[j * t:(j + 1) * t]


def _flash(proj, tabs, far, sel, *, n_kv, group, q_col, k_col, v_col, head0=0,
           window=None, dil=1, sel_blk=0, sel_per_head=False, want_lse=False):
    b, s, _ = proj.shape
    t = ATTN_TILE
    gw = group * HEAD_DIM
    n_tab = tabs.shape[1]
    w_tiles = None if window is None else window // t
    far_tab = 2 if dil > 1 else None
    edge_tab = n_tab - 1 if (w_tiles is not None and w_tiles >= 2) else None
    in_specs = [
        pl.BlockSpec((1, t, gw), lambda bi, h, qi: (bi, qi, q_col // group + h)),
        pl.BlockSpec((1, s, HEAD_DIM), lambda bi, h, qi: (bi, 0, k_col + h)),
        pl.BlockSpec((1, s, HEAD_DIM), lambda bi, h, qi: (bi, 0, v_col + h)),
        pl.BlockSpec((group, n_tab, t, t), lambda bi, h, qi: (head0 // group + h, 0, 0, 0)),
        pl.BlockSpec(memory_space=pltpu.SMEM),
    ]
    args = [proj, proj, proj, tabs, far]
    if sel_blk:
        if sel_per_head:
            in_specs.append(pl.BlockSpec((1, group, t, HEAD_DIM), lambda bi, h, qi: (bi, h, qi, 0)))
        else:
            in_specs.append(pl.BlockSpec((1, 1, t, HEAD_DIM), lambda bi, h, qi: (bi, h, qi, 0)))
        args.append(sel)
    out_spec = pl.BlockSpec((1, t, gw), lambda bi, h, qi: (bi, qi, h))
    out_sds = jax.ShapeDtypeStruct((b, s, n_kv * gw), F32)
    rows = group * t
    return pl.pallas_call(
        functools.partial(_flash_kernel, group=group, t=t, w_tiles=w_tiles, far_tab=far_tab, edge_tab=edge_tab,
                          sel_blk=sel_blk, sel_per_head=sel_per_head, want_lse=want_lse, head0=head0),
        grid=(b, n_kv, s // t),
        in_specs=in_specs,
        out_specs=[out_spec, out_spec] if want_lse else out_spec,
        out_shape=[out_sds, out_sds] if want_lse else out_sds,
        scratch_shapes=[pltpu.VMEM((s, HEAD_DIM), BF16), pltpu.VMEM((s, HEAD_DIM), BF16),
                        pltpu.VMEM((rows, HEAD_DIM), BF16), pltpu.VMEM((rows, t), F32),
                        pltpu.VMEM((rows, HEAD_DIM), F32), pltpu.VMEM((rows, HEAD_DIM), F32),
                        pltpu.VMEM((rows, HEAD_DIM), F32)],
        compiler_params=_params(3),
        name="flash",
    )(*args)


def _flash2_kernel(*refs, group, t, w_tiles, far_tab, edge_tab, sel_blk, sel_per_head, head0):
    it = iter(refs)
    q_ref, k_ref, v_ref, tab_ref, far_ref = (next(it) for _ in range(5))
    sel_ref = next(it) if sel_blk else None
    o_ref = next(it)
    kb_ref, vb_ref, qs_ref, s_ref, mx_ref, l_ref, acc_ref = (next(it) for _ in range(7))

    kvh = pl.program_id(1)
    qi = pl.program_id(2)
    rows = group * t

    @pl.when(qi == 0)
    def _():
        kb_ref[...] = k_ref[0].astype(BF16)
        vb_ref[...] = v_ref[0].astype(BF16)

    for j in range(group):
        qs_ref[j * t:(j + 1) * t, :] = q_ref[0, :, j * HEAD_DIM:(j + 1) * HEAD_DIM].astype(BF16)

    def scores(kj, tab, first=False):
        start = pl.multiple_of(kj * t, t)
        k = kb_ref[pl.ds(start, t), :]
        sel_add = None
        if sel_blk and not sel_per_head:
            blk_of_key = (start + lax.broadcasted_iota(jnp.int32, (HEAD_DIM, t), 1)) // sel_blk
            expand = lax.broadcasted_iota(jnp.int32, (HEAD_DIM, t), 0) == blk_of_key
            expand = jnp.where(expand, 1.0, 0.0).astype(BF16)
            sel_add = _dot(sel_ref[0, 0].astype(BF16), expand)
        for j in range(group):
            hs = slice(j * t, (j + 1) * t)
            sj = _dot(qs_ref[hs, :], k, _NT) * SOFTMAX_SCALE
            sj = sj + (far_ref[head0 + kvh * group + j] if tab is None else tab_ref[j, tab])
            if sel_blk and sel_per_head:
                assert sel_blk == t
                lane = lax.broadcasted_iota(jnp.int32, (t, HEAD_DIM), 1)
                sj = sj + jnp.sum(jnp.where(lane == kj, sel_ref[0, j], 0.0), axis=1, keepdims=True)
            elif sel_blk:
                sj = sj + sel_add
            s_ref[kj, hs, :] = sj
            mj = jnp.maximum(sj[:, :HEAD_DIM], sj[:, HEAD_DIM:])
            mx_ref[hs, :] = mj if first else jnp.maximum(mx_ref[hs, :], mj)

    def accumulate(kj, first=False):
        start = pl.multiple_of(kj * t, t)
        v = vb_ref[pl.ds(start, t), :]
        for j in range(group):
            hs = slice(j * t, (j + 1) * t)
            m = mx_ref[hs, :]
            p = jnp.exp(s_ref[kj, hs, :] - jnp.concatenate([m, m], axis=1))
            p_lane = p[:, :HEAD_DIM] + p[:, HEAD_DIM:]
            pv = _dot(p.astype(BF16), v)
            l_ref[hs, :] = p_lane if first else l_ref[hs, :] + p_lane
            acc_ref[hs, :] = pv if first else acc_ref[hs, :] + pv

    def sweep(visit, visit_first):
        visit_first(qi)

        @pl.when(qi >= 1)
        def _():
            visit(qi - 1, 1)

        def far_body(kj, carry):
            visit(kj, far_tab)
            return carry

        if w_tiles is None:
            lax.fori_loop(0, qi - 1, far_body, 0)
        elif w_tiles >= 2:
            lax.fori_loop(jnp.maximum(qi - w_tiles + 1, 0), qi - 1, far_body, 0)

            @pl.when(qi >= w_tiles)
            def _():
                visit(qi - w_tiles, edge_tab)

    sweep(lambda kj, tab: scores(kj, tab), lambda kj: scores(kj, 0, first=True))
    mx_ref[...] = jnp.broadcast_to(jnp.max(mx_ref[...], axis=1, keepdims=True), (rows, HEAD_DIM))
    sweep(lambda kj, tab: accumulate(kj), lambda kj: accumulate(kj, first=True))

    out = acc_ref[...] / jnp.sum(l_ref[...], axis=1, keepdims=True)
    for j in range(group):
        o_ref[0, :, j * HEAD_DIM:(j + 1) * HEAD_DIM] = out[j * t:(j + 1) * t]


def _flash2(proj, tabs, far, sel, *, n_kv, group, q_col, k_col, v_col, head0=0,
            window=None, sel_blk=0, sel_per_head=False):
    b, s, _ = proj.shape
    t = ATTN_TILE
    assert t == 2 * HEAD_DIM
    gw = group * HEAD_DIM
    n_tab = tabs.shape[1]
    w_tiles = None if window is None else window // t
    edge_tab = n_tab - 1 if (w_tiles is not None and w_tiles >= 2) else None
    in_specs = [
        pl.BlockSpec((1, t, gw), lambda bi, h, qi: (bi, qi, q_col // group + h)),
        pl.BlockSpec((1, s, HEAD_DIM), lambda bi, h, qi: (bi, 0, k_col + h)),
        pl.BlockSpec((1, s, HEAD_DIM), lambda bi, h, qi: (bi, 0, v_col + h)),
        pl.BlockSpec((group, n_tab, t, t), lambda bi, h, qi: (head0 // group + h, 0, 0, 0)),
        pl.BlockSpec(memory_space=pltpu.SMEM),
    ]
    args = [proj, proj, proj, tabs, far]
    if sel_blk:
        if sel_per_head:
            in_specs.append(pl.BlockSpec((1, group, t, HEAD_DIM), lambda bi, h, qi: (bi, h, qi, 0)))
        else:
            in_specs.append(pl.BlockSpec((1, 1, t, HEAD_DIM), lambda bi, h, qi: (bi, h, qi, 0)))
        args.append(sel)
    rows = group * t
    return pl.pallas_call(
        functools.partial(_flash2_kernel, group=group, t=t, w_tiles=w_tiles, far_tab=None, edge_tab=edge_tab,
                          sel_blk=sel_blk, sel_per_head=sel_per_head, head0=head0),
        grid=(b, n_kv, s // t),
        in_specs=in_specs,
        out_specs=pl.BlockSpec((1, t, gw), lambda bi, h, qi: (bi, qi, h)),
        out_shape=jax.ShapeDtypeStruct((b, s, n_kv * gw), F32),
        scratch_shapes=[pltpu.VMEM((s, HEAD_DIM), BF16), pltpu.VMEM((s, HEAD_DIM), BF16),
                        pltpu.VMEM((rows, HEAD_DIM), BF16), pltpu.VMEM((s // t, rows, t), F32),
                        pltpu.VMEM((rows, HEAD_DIM), F32), pltpu.VMEM((rows, HEAD_DIM), F32),
                        pltpu.VMEM((rows, HEAD_DIM), F32)],
        compiler_params=_params(3),
        name="flash2",
    )(*args)


def _dilated_kernel(q_ref, k_ref, v_ref, tab_ref, o_ref, lse_ref, m_ref, l_ref, acc_ref, *, t, dil, n_q):
    def rows(r, tile_idx):
        return pl.ds(r + tile_idx * (t * dil), t, stride=dil)

    def lane_pair(op, x):
        return op(x[:, :HEAD_DIM], x[:, HEAD_DIM:])

    def one(idx):
        r = idx // n_q
        qi = idx % n_q
        q = q_ref[0, rows(r, qi), :].astype(BF16)
        tiles = [(qi, 0, None)]
        if n_q > 1:
            tiles.append((jnp.maximum(qi - 1, 0), 1, jnp.where(qi >= 1, 0.0, NEG_INF)))
        scores, values = [], []
        for kj, tab, extra in tiles:
            k = k_ref[0, rows(r, kj), :].astype(BF16)
            values.append(v_ref[0, rows(r, kj), :].astype(BF16))
            s = _dot(q, k, _NT) * SOFTMAX_SCALE + tab_ref[0, tab]
            scores.append(s if extra is None else s + extra)
        m_lane = lane_pair(jnp.maximum, scores[0])
        for s in scores[1:]:
            m_lane = jnp.maximum(m_lane, lane_pair(jnp.maximum, s))
        m = jnp.max(m_lane, axis=1, keepdims=True)
        l_lane, acc = None, None
        for s, v in zip(scores, values):
            p = jnp.exp(s - m)
            pl_ = lane_pair(jnp.add, p)
            pv = _dot(p.astype(BF16), v)
            l_lane = pl_ if l_lane is None else l_lane + pl_
            acc = pv if acc is None else acc + pv
        l = jnp.sum(l_lane, axis=1, keepdims=True)
        o_ref[0, rows(r, qi), :] = acc / l
        lse_ref[0, rows(r, qi), :] = jnp.broadcast_to(m + jnp.log(l), (t, HEAD_DIM))

    n_it = dil * n_q
    chains = 4 if n_it % 4 == 0 else 1

    def body(it, carry):
        for u in range(chains):
            one(it * chains + u)
        return carry

    lax.fori_loop(0, n_it // chains, body, 0)


def _dilated_attention(proj, tabs, *, n_heads, q_col, k_col, v_col, dil):
    b, s, _ = proj.shape
    t = ATTN_TILE
    assert t == 2 * HEAD_DIM and (s // dil) % t == 0
    spec = lambda col: pl.BlockSpec((1, s, HEAD_DIM), lambda bi, h: (bi, 0, col + h))
    out_sds = jax.ShapeDtypeStruct((b, s, n_heads * HEAD_DIM), F32)
    return pl.pallas_call(
        functools.partial(_dilated_kernel, t=t, dil=dil, n_q=s // dil // t),
        grid=(b, n_heads),
        in_specs=[spec(q_col), spec(k_col), spec(v_col),
                  pl.BlockSpec((1, 2, t, t), lambda bi, h: (h, 0, 0, 0))],
        out_specs=[spec(0), spec(0)],
        out_shape=[out_sds, out_sds],
        scratch_shapes=[pltpu.VMEM((t, HEAD_DIM), F32)] * 3,
        compiler_params=_params(2),
        name="dilated_attention",
    )(proj, proj, proj, tabs)


def _moba_select_kernel(q_ref, k_ref, sel_ref, kmean_ref, *, group, t, n_blk):
    qi = pl.program_id(2)

    @pl.when(qi == 0)
    def _():
        kmean_ref[...] = jnp.zeros_like(kmean_ref)
        s = k_ref.shape[1]
        km = jnp.sum(k_ref[0].reshape(n_blk, s // n_blk, HEAD_DIM), axis=1) * (1.0 / B_BLOCK)
        kmean_ref[0:n_blk, :] = km

    lane = lax.broadcasted_iota(jnp.int32, (t, HEAD_DIM), 1)
    pos = qi * t + lax.broadcasted_iota(jnp.int32, (t, HEAD_DIM), 0)
    own = pos // B_BLOCK
    past = lane < own
    kmean = kmean_ref[...]
    for j in range(group):
        q = q_ref[0, :, j * HEAD_DIM:(j + 1) * HEAD_DIM]
        gate = _dot_f32_both(q, kmean, _NT)
        gate = jnp.where(past, gate, NEG_INF)
        rank = _rank_lanes(gate, n_blk)
        sel = (past & (rank < B_TOPK)) | (lane == own)
        sel_ref[0, j] = jnp.where(sel, 0.0, NEG_INF)


def _moba_select(proj, *, n_kv, group, k_col):
    b, s, _ = proj.shape
    t = ATTN_TILE
    n_blk = s // B_BLOCK
    return pl.pallas_call(
        functools.partial(_moba_select_kernel, group=group, t=t, n_blk=n_blk),
        grid=(b, n_kv, s // t),
        in_specs=[pl.BlockSpec((1, t, group * HEAD_DIM), lambda bi, h, qi: (bi, qi, h)),
                  pl.BlockSpec((1, s, HEAD_DIM), lambda bi, h, qi: (bi, 0, k_col + h))],
        out_specs=pl.BlockSpec((1, group, t, HEAD_DIM), lambda bi, h, qi: (bi, h, qi, 0)),
        out_shape=jax.ShapeDtypeStruct((b, n_kv * group, s, HEAD_DIM), F32),
        scratch_shapes=[pltpu.VMEM((HEAD_DIM, HEAD_DIM), F32)],
        compiler_params=_params(3),
        name="moba_select",
    )(proj, proj)


def _cmp1_prompt_kernel(x_ref, w_ref, y_ref, *, n_seg):
    x2 = jnp.concatenate([x_ref[pl.ds(s, n_seg, stride=C_CMP_STRIDE), :] for s in range(C_CMP_STRIDE)], axis=1)
    y_ref[...] = _dot(x2.astype(BF16), w_ref[...])


def _cmp1_prompt(proj, w1cat, *, col0):
    b, s, _ = proj.shape
    n_seg = s // C_CMP_STRIDE
    return pl.pallas_call(
        functools.partial(_cmp1_prompt_kernel, n_seg=n_seg),
        grid=(b, 2, C_KV_HEADS),
        in_specs=[pl.BlockSpec((None, s, HEAD_DIM), lambda bi, c, g: (bi, 0, col0 + c * C_KV_HEADS + g)),
                  pl.BlockSpec((None, C_CMP_STRIDE * HEAD_DIM, 2 * HEAD_DIM), lambda bi, c, g: (c, 0, 0))],
        out_specs=pl.BlockSpec((None, n_seg, 2 * HEAD_DIM), lambda bi, c, g: ((bi * 2 + c) * C_KV_HEADS + g, 0, 0)),
        out_shape=jax.ShapeDtypeStruct((b * 2 * C_KV_HEADS, n_seg, 2 * HEAD_DIM), F32),
        compiler_params=_params(3),
        name="cmp1_prompt",
    )(proj, w1cat)


def _cmp2_kernel(y_ref, pe_ref, w1_ref, w2_ref, o_ref, *, n_seg):
    y = y_ref[0]
    first = y[:, :HEAD_DIM]
    second = pltpu.roll(y[:, HEAD_DIM:], n_seg - 1, 0)
    pe_term = _dot(pe_ref[0].astype(BF16), w1_ref[0].astype(BF16))[0:1]
    pre = first + second + pe_term
    tok = _dot(jnp.maximum(pre, 0.0).astype(BF16), w2_ref[0].astype(BF16))
    row = lax.broadcasted_iota(jnp.int32, tok.shape, 0)
    o_ref[0] = jnp.where(row < n_seg - 1, tok, 0.0)


def _cmp2(y, pe8, w1, w2):
    r, n_seg, _ = y.shape
    comp = lambda i: ((i // C_KV_HEADS) % 2, 0, 0)
    return pl.pallas_call(
        functools.partial(_cmp2_kernel, n_seg=n_seg),
        grid=(r,),
        in_specs=[pl.BlockSpec((1, n_seg, 2 * HEAD_DIM), lambda i: (i, 0, 0)),
                  pl.BlockSpec((1, 8, C_CMP_BLOCK * HEAD_DIM), comp),
                  pl.BlockSpec((1, C_CMP_BLOCK * HEAD_DIM, HEAD_DIM), comp),
                  pl.BlockSpec((1, HEAD_DIM, HEAD_DIM), comp)],
        out_specs=pl.BlockSpec((1, n_seg, HEAD_DIM), lambda i: (i, 0, 0)),
        out_shape=jax.ShapeDtypeStruct((r, n_seg, HEAD_DIM), F32),
        compiler_params=_params(1),
        name="cmp2",
    )(y, pe8, w1, w2)


CMP_BAND = 8


def _nsa_cmp_select_kernel(q_ref, kc_ref, vc_ref, bias_ref, far_ref, map_ref, oc_ref, sel_ref, *,
                           group, t, n_cmp, n_sel):
    kvh = pl.program_id(1)
    qi = pl.program_id(2)
    n_pad = kc_ref.shape[1]
    pos = qi * t + lax.broadcasted_iota(jnp.int32, (t, n_pad), 0)
    tok = lax.broadcasted_iota(jnp.int32, (t, n_pad), 1)
    valid = (pos - (tok * C_CMP_STRIDE + (C_CMP_BLOCK - 1)) >= 0) & (tok < n_cmp)
    band0 = qi * (t // C_CMP_STRIDE) - CMP_BAND
    shift = jnp.where(band0 < 0, band0 + n_pad, band0)
    is_far = tok < band0
    kc = kc_ref[0].astype(BF16)
    vc = vc_ref[0].astype(BF16)
    p_sum = jnp.zeros((t, n_pad), F32)
    for j in range(group):
        q = q_ref[0, :, j * HEAD_DIM:(j + 1) * HEAD_DIM].astype(BF16)
        bias = jnp.where(is_far, far_ref[kvh * group + j], pltpu.roll(bias_ref[j], shift, 1))
        s = _dot(q, kc, _NT) * SOFTMAX_SCALE + bias
        s = jnp.where(valid, s, NEG_INF)
        m = jnp.max(s, axis=1, keepdims=True)
        e = jnp.where(valid, jnp.exp(s - m), 0.0)
        p = e / jnp.maximum(jnp.sum(e, axis=1, keepdims=True), 1e-30)
        oc_ref[0, :, j * HEAD_DIM:(j + 1) * HEAD_DIM] = _dot(p.astype(BF16), vc)
        p_sum = p_sum + p
    imp = _dot_f32_lhs(p_sum, map_ref[...])
    blk = lax.broadcasted_iota(jnp.int32, (t, HEAD_DIM), 1)
    cur = (qi * t + lax.broadcasted_iota(jnp.int32, (t, HEAD_DIM), 0)) // C_SLC_BLOCK
    avail = (blk <= cur) & (blk < n_sel)
    forced = (blk == 0) | (blk == cur) | (blk == cur - 1)
    score = jnp.where(forced, jnp.inf, imp)
    score = jnp.where(avail, score, -jnp.inf)
    rank = _rank_lanes(score, n_sel)
    sel = avail & (rank < min(C_SLC_TOPK, n_sel))
    sel_ref[0, 0] = jnp.where(sel, 0.0, NEG_INF)


def _cmp_bias_table(rel_bias, s, n_seg):
    assert s == n_seg * C_CMP_STRIDE
    h = rel_bias.shape[1]
    k = np.arange(2 * n_seg + 1)
    e = np.where(k <= n_seg, -k, 2 * n_seg + 1 - k)
    dist = C_CMP_STRIDE * e[None, :] + np.arange(C_CMP_STRIDE)[:, None] - (C_CMP_BLOCK - 1)
    w = jnp.transpose(rel_bias[jnp.asarray(_bucket_np(dist))], (2, 0, 1))
    tab = _toeplitz(w.reshape(h * C_CMP_STRIDE, -1), n_seg).reshape(h, C_CMP_STRIDE, n_seg, n_seg)
    return jnp.transpose(tab, (0, 2, 1, 3)).reshape(h, s, n_seg)


def _cmp_map_np(n_rows, n_cmp, n_cols):
    ratio = C_SLC_BLOCK // C_CMP_STRIDE
    m = np.zeros((n_rows, n_cols), np.float32)
    for i in range(n_cmp):
        for r in range(C_CMP_BLOCK // C_CMP_STRIDE):
            m[i, (i + r) // ratio] += 1.0
    return m


def _nsa_cmp_select(proj, kvc, rel_bias, far, *, n_kv, group):
    b, s, _ = proj.shape
    t = ATTN_TILE
    n_seg = kvc.shape[1]
    n_cmp = n_seg - 1
    n_sel = -(-s // C_SLC_BLOCK)
    cmap = jnp.asarray(_cmp_map_np(n_seg, n_cmp, HEAD_DIM), BF16)
    gw = group * HEAD_DIM
    n_band = CMP_BAND + 2 * (t // C_CMP_STRIDE) - (C_CMP_BLOCK - 1) // C_CMP_STRIDE
    assert n_band <= n_seg and C_CMP_STRIDE * (CMP_BAND + 1) - (C_CMP_BLOCK - 1) >= 113
    dist = np.arange(t)[:, None] - C_CMP_STRIDE * (np.arange(n_band)[None, :] - CMP_BAND) - (C_CMP_BLOCK - 1)
    bias_c = jnp.pad(_bias_by_dist(rel_bias, np.maximum(dist, 0)), ((0, 0), (0, 0), (0, n_seg - n_band)))
    return pl.pallas_call(
        functools.partial(_nsa_cmp_select_kernel, group=group, t=t, n_cmp=n_cmp, n_sel=n_sel),
        grid=(b, n_kv, s // t),
        in_specs=[pl.BlockSpec((1, t, gw), lambda bi, h, qi: (bi, qi, h)),
                  pl.BlockSpec((1, n_seg, HEAD_DIM), lambda bi, h, qi: (bi * 8 + h, 0, 0)),
                  pl.BlockSpec((1, n_seg, HEAD_DIM), lambda bi, h, qi: (bi * 8 + 4 + h, 0, 0)),
                  pl.BlockSpec((group, t, n_seg), lambda bi, h, qi: (h, 0, 0)),
                  pl.BlockSpec(memory_space=pltpu.SMEM),
                  pl.BlockSpec((n_seg, HEAD_DIM), lambda bi, h, qi: (0, 0))],
        out_specs=[pl.BlockSpec((1, t, gw), lambda bi, h, qi: (bi, qi, h)),
                   pl.BlockSpec((1, 1, t, HEAD_DIM), lambda bi, h, qi: (bi, h, qi, 0))],
        out_shape=[jax.ShapeDtypeStruct((b, s, n_kv * gw), F32),
                   jax.ShapeDtypeStruct((b, n_kv, s, HEAD_DIM), F32)],
        compiler_params=_params(3),
        name="nsa_cmp_select",
    )(proj, kvc, kvc, bias_c, far, cmap)


def _row_tile(m, pref):
    return pref if m % pref == 0 else m


def _mixer_a_prompt(x, b, g_mix, li, w_in, w_out, slot, tabs, far):
    m = x.shape[0]
    s = m // b
    n_in = w_in.shape[2]
    proj = _norm_matmul(x, g_mix, li, w_in, slot, n_in, _row_tile(m, PROMPT_ROW_TILE), 512)
    proj3 = proj.reshape(b, s, n_in)
    outs, lses = [], []
    for g, (window, dil) in enumerate(A_PATTERNS):
        c0 = g * 3 * A_HEADS
        o, lse = _dilated_attention(proj3, tabs[g], n_heads=A_HEADS, q_col=c0, k_col=c0 + A_HEADS,
                                    v_col=c0 + 2 * A_HEADS, dil=dil)
        outs.append(o.reshape(m, -1))
        lses.append(lse.reshape(m, -1))
    x = _outproj(outs, lses, w_out, slot, x, "lse3", _row_tile(m, 512), 512)
    return x, proj3


def _mixer_b_prompt(x, b, g_mix, li, w_in, w_out, slot, tabs, far):
    m = x.shape[0]
    s = m // b
    n_in = w_in.shape[2]
    proj = _norm_matmul(x, g_mix, li, w_in, slot, n_in, _row_tile(m, PROMPT_ROW_TILE), 512)
    proj3 = proj.reshape(b, s, n_in)
    sel = _moba_select(proj3, n_kv=B_KV_HEADS, group=GQA_GROUP, k_col=B_HEADS)
    o = _flash2(proj3, tabs, far, sel, n_kv=B_KV_HEADS, group=GQA_GROUP, q_col=0, k_col=B_HEADS,
               v_col=B_HEADS + B_KV_HEADS, sel_blk=B_BLOCK, sel_per_head=True)
    x = _outproj([o.reshape(m, -1)], [], w_out, slot, x, "plain", _row_tile(m, 512), 512)
    return x, proj3


def _cmp_weights(cmp_pe, cmp_w1, cmp_w2, slot):
    half = C_CMP_STRIDE * HEAD_DIM
    w1 = cmp_w1[slot]
    w1cat = jnp.concatenate([w1[:, :half], w1[:, half:]], axis=2).astype(BF16)
    pe8 = jnp.broadcast_to(cmp_pe[slot].reshape(2, 1, C_CMP_BLOCK * HEAD_DIM), (2, 8, C_CMP_BLOCK * HEAD_DIM))
    return w1cat, pe8, w1, cmp_w2[slot]


def _mixer_c_prompt(x, b, g_mix, li, w_in, w_gate, w_out, slot, cmp_w, tabs, tabs_win, far, rel_bias):
    m = x.shape[0]
    s = m // b
    n_main = (C_HEADS + 6 * C_KV_HEADS) * HEAD_DIM
    proj = _norm_matmul(x, g_mix, li, w_in, slot, n_main, _row_tile(m, PROMPT_ROW_TILE), 512)
    gate = _norm_matmul(x, g_mix, li, w_gate, 0, HEAD_DIM, _row_tile(m, 512), HEAD_DIM)
    proj3 = proj.reshape(b, s, n_main)
    w1cat, pe8, w1, w2 = cmp_w
    y = _cmp1_prompt(proj3, w1cat, col0=C_HEADS)
    kvc = _cmp2(y, pe8, w1, w2)
    n_seg = kvc.shape[1]
    o_cmp, sel = _nsa_cmp_select(proj3, kvc, rel_bias, far, n_kv=C_KV_HEADS, group=GQA_GROUP)
    kv0 = C_HEADS
    o_slc = _flash2(proj3, tabs, far, sel, n_kv=C_KV_HEADS, group=GQA_GROUP, q_col=0, k_col=kv0 + 2 * C_KV_HEADS,
                   v_col=kv0 + 3 * C_KV_HEADS, sel_blk=C_SLC_BLOCK)
    o_win = _flash2(proj3, tabs_win, far, None, n_kv=C_KV_HEADS, group=GQA_GROUP, q_col=0, k_col=kv0 + 4 * C_KV_HEADS,
                   v_col=kv0 + 5 * C_KV_HEADS, window=C_WINDOW)
    x = _outproj([o_cmp.reshape(m, -1), o_slc.reshape(m, -1), o_win.reshape(m, -1)], [gate], w_out, slot, x,
                 "gate3", _row_tile(m, 512), 512)
    return x, proj3


def _a_decode_kernel(qkv_ref, buf0_ref, buf1_ref, buf2_ref, bias_ref, o_ref):
    outs, lses = [], []
    for g, buf in enumerate((buf0_ref, buf1_ref, buf2_ref)):
        q, k_new, v_new = qkv_ref[0, g, 0], qkv_ref[0, g, 1], qkv_ref[0, g, 2]
        kb, vb = buf[:, 0], buf[:, 1]
        n = kb.shape[0]
        s = jnp.sum(q[None] * kb, axis=-1, keepdims=True) * SOFTMAX_SCALE + bias_ref[g, 0:n]
        s_new = jnp.sum(q * k_new, axis=-1, keepdims=True) * SOFTMAX_SCALE + bias_ref[g, n]
        m = jnp.maximum(jnp.max(s, axis=0), s_new)
        e = jnp.exp(s - m[None])
        e_new = jnp.exp(s_new - m)
        l = jnp.sum(e, axis=0) + e_new
        outs.append((jnp.sum(e * vb, axis=0) + e_new * v_new) / l)
        lses.append(m + jnp.log(l))
    mx = jnp.maximum(jnp.maximum(lses[0], lses[1]), lses[2])
    w = [jnp.exp(l - mx) for l in lses]
    o_ref[0] = (w[0] * outs[0] + w[1] * outs[1] + w[2] * outs[2]) / (w[0] + w[1] + w[2])


def _a_decode(qkv, bufs, slot, bias):
    n_dec = qkv.shape[0]
    in_specs = [pl.BlockSpec((1, 3, 3, A_HEADS, HEAD_DIM), lambda i: (i, 0, 0, 0, 0))]
    views = []
    for (window, dil), buf in zip(A_PATTERNS, bufs):
        n = window // dil
        assert buf.shape[2] == window, "window buffers must hold exactly `window` past rows"
        views.append(buf.reshape(buf.shape[0], n_dec, n, dil, 2, A_HEADS, HEAD_DIM))
        in_specs.append(pl.BlockSpec((None, None, n, None, 2, A_HEADS, HEAD_DIM),
                                     lambda i: (slot, i, 0, 0, 0, 0, 0)))
    in_specs.append(pl.BlockSpec(bias.shape, lambda i: (0, 0, 0, 0)))
    return pl.pallas_call(
        _a_decode_kernel,
        grid=(n_dec,),
        in_specs=in_specs,
        out_specs=pl.BlockSpec((1, A_HEADS, HEAD_DIM), lambda i: (i, 0, 0)),
        out_shape=jax.ShapeDtypeStruct((n_dec, A_HEADS, HEAD_DIM), F32),
        compiler_params=_params(1),
        name="a_decode",
    )(qkv, *views, bias)


DECODE_CHUNK = 2048


def _head_rows_store(o_ref, o_all, l_row):
    l_col = jnp.broadcast_to(l_row, (HEAD_DIM, HEAD_DIM)).T
    for g in range(B_KV_HEADS):
        rows = slice(g * GQA_GROUP, (g + 1) * GQA_GROUP)
        o_ref[0, rows, :] = o_all[rows, g * HEAD_DIM:(g + 1) * HEAD_DIM] / l_col[rows, :]


def _paged_decode_kernel(pt_ref, *refs, n_pg, n_sub, k_sub, v_sub, blk, topk, n_pages):
    page_refs = refs[:n_pg]
    qbd_ref, new_ref, bias_ref = refs[n_pg:n_pg + 3]
    rest = refs[n_pg + 3:]
    if topk is None:
        sel_ref, o_ref, st_ref, vs_ref, sel_s = rest
    else:
        o_ref, st_ref, vs_ref, sel_s = rest
    j = pl.program_id(1)
    past = n_pages * PAGE_SIZE
    n_blk = past // blk
    ch = DECODE_CHUNK

    if topk is not None:
        @pl.when(j == 0)
        def _():
            sel_s[...] = jnp.zeros_like(sel_s)

    for p in range(n_pg):
        page = j * n_pg + p
        row0 = pl.multiple_of(page * PAGE_SIZE, PAGE_SIZE)
        s_acc = jnp.zeros((PAGE_SIZE, HEAD_DIM), F32)
        for g in range(B_KV_HEADS):
            kg = page_refs[p][pl.ds(k_sub + g, PAGE_SIZE, stride=n_sub), :].astype(BF16)
            s_acc = s_acc + _dot(kg, qbd_ref[0, g])
            vg = page_refs[p][pl.ds(v_sub + g, PAGE_SIZE, stride=n_sub), :].astype(BF16)
            vs_ref[pl.ds(row0, PAGE_SIZE), g * HEAD_DIM:(g + 1) * HEAD_DIM] = vg
        st_ref[pl.ds(row0, PAGE_SIZE), :] = s_acc
        if topk is not None:
            b_idx = page // (blk // PAGE_SIZE)
            sel_s[pl.ds(b_idx, 1), :] = sel_s[pl.ds(b_idx, 1), :] + jnp.sum(s_acc, axis=0, keepdims=True)

    @pl.when(j == pl.num_programs(1) - 1)
    def _():
        s_new = jnp.zeros((8, HEAD_DIM), F32)
        for g in range(B_KV_HEADS):
            k_new = jnp.broadcast_to(new_ref[0, g:g + 1, :], (8, HEAD_DIM)).astype(BF16)
            s_new = s_new + _dot(k_new, qbd_ref[0, g])
            v_new = jnp.broadcast_to(new_ref[0, B_KV_HEADS + g:B_KV_HEADS + g + 1, :], (8, HEAD_DIM))
            vs_ref[past:past + 8, g * HEAD_DIM:(g + 1) * HEAD_DIM] = v_new.astype(BF16)
        l_new = s_new[0:1] * SOFTMAX_SCALE + bias_ref[ch + 1:ch + 2, :]

        if topk is not None:
            gate = sel_s[...] * (1.0 / blk)
            rank = _rank_rows(gate, n_blk)
            sel_s[...] = jnp.where(rank < topk, 1.0, 0.0)
        else:
            sel_s[...] = sel_ref[0, 0:n_blk, :]

        def masked_logits(c, bias):
            start = pl.multiple_of(c * ch, ch)
            lg = st_ref[pl.ds(start, ch), :] * SOFTMAX_SCALE + bias
            b0 = pl.multiple_of(c * (ch // blk), 8)
            sl = sel_s[pl.ds(b0, ch // blk), :]
            mk = jnp.broadcast_to(sl[:, None, :], (ch // blk, blk, HEAD_DIM)).reshape(ch, HEAD_DIM) > 0.5
            return jnp.where(mk, lg, NEG_INF), mk, start

        n_ch = past // ch
        far = bias_ref[ch:ch + 1, :]
        tail = bias_ref[0:ch, :]

        def max_body(c, m):
            lg, _, _ = masked_logits(c, far)
            return jnp.maximum(m, jnp.max(lg, axis=0, keepdims=True))

        m = lax.fori_loop(0, n_ch - 1, max_body, l_new)
        lg_t, mk_t, start_t = masked_logits(n_ch - 1, tail)
        m = jnp.maximum(m, jnp.max(lg_t, axis=0, keepdims=True))

        def acc_chunk(lg, mk, start, carry):
            l_sum, o_all = carry
            e = jnp.where(mk, jnp.exp(lg - m), 0.0)
            o_all = o_all + _dot(e.astype(BF16), vs_ref[pl.ds(start, ch), :], _TN)
            return l_sum + jnp.sum(e, axis=0, keepdims=True), o_all

        def acc_body(c, carry):
            lg, mk, start = masked_logits(c, far)
            return acc_chunk(lg, mk, start, carry)

        e_new = jnp.exp(l_new - m)
        row8 = lax.broadcasted_iota(jnp.int32, (8, HEAD_DIM), 0)
        e_new8 = jnp.where(row8 == 0, jnp.broadcast_to(e_new, (8, HEAD_DIM)), 0.0)
        o0 = _dot(e_new8.astype(BF16), vs_ref[past:past + 8, :], _TN)
        carry = lax.fori_loop(0, n_ch - 1, acc_body, (e_new, o0))
        l_sum, o_all = acc_chunk(lg_t, mk_t, start_t, carry)
        _head_rows_store(o_ref, o_all, l_sum)


def _paged_decode(page_table, pool2d, qbd, new_kv, bias, sel, *, n_pg, n_sub, k_sub, v_sub, blk, topk):
    n_dec, n_pages = page_table.shape
    past = n_pages * PAGE_SIZE
    assert past % DECODE_CHUNK == 0 and n_pages % n_pg == 0 and past % blk == 0
    n_blk = past // blk
    page_rows = PAGE_SIZE * n_sub
    in_specs = [pl.BlockSpec((page_rows, HEAD_DIM), functools.partial(lambda n, j, pt, p: (pt[n, j * n_pg + p], 0), p=p))
                for p in range(n_pg)]
    in_specs += [pl.BlockSpec((1, B_KV_HEADS, HEAD_DIM, HEAD_DIM), lambda n, j, pt: (n, 0, 0, 0)),
                 pl.BlockSpec((1, 2 * B_KV_HEADS, HEAD_DIM), lambda n, j, pt: (n, 0, 0)),
                 pl.BlockSpec(bias.shape, lambda n, j, pt: (0, 0))]
    args = [pool2d] * n_pg + [qbd, new_kv, bias]
    if topk is None:
        in_specs.append(pl.BlockSpec((1,) + sel.shape[1:], lambda n, j, pt: (n, 0, 0)))
        args.append(sel)
    return pl.pallas_call(
        functools.partial(_paged_decode_kernel, n_pg=n_pg, n_sub=n_sub, k_sub=k_sub, v_sub=v_sub, blk=blk,
                          topk=topk, n_pages=n_pages),
        grid_spec=pltpu.PrefetchScalarGridSpec(
            num_scalar_prefetch=1,
            grid=(n_dec, n_pages // n_pg),
            in_specs=in_specs,
            out_specs=pl.BlockSpec((1, B_HEADS, HEAD_DIM), lambda n, j, pt: (n, 0, 0)),
            scratch_shapes=[pltpu.VMEM((past + 8, HEAD_DIM), F32),
                            pltpu.VMEM((past + 8, B_KV_HEADS * HEAD_DIM), BF16),
                            pltpu.VMEM((n_blk, HEAD_DIM), F32)],
        ),
        out_shape=jax.ShapeDtypeStruct((n_dec, B_HEADS, HEAD_DIM), F32),
        compiler_params=_params(2),
        name="paged_decode",
    )(page_table, *args)


def _block_diag_q(q):
    n, h, d = q.shape
    qt = jnp.pad(jnp.swapaxes(q, 1, 2), ((0, 0), (0, 0), (0, HEAD_DIM - h)))
    grp = (np.arange(HEAD_DIM)[None, :] // GQA_GROUP == np.arange(B_KV_HEADS)[:, None]) & (np.arange(HEAD_DIM)[None, :] < h)
    return jnp.where(jnp.asarray(grp)[None, :, None, :], qt[:, None], 0.0).astype(BF16)


def _decode_bias(rel_bias, past, n_tail):
    dist = np.concatenate([n_tail - np.arange(n_tail), [MAX_DISTANCE * 8, 0], np.zeros(6, np.int64)])
    tab = rel_bias[jnp.asarray(_bucket_np(dist))]
    return jnp.pad(tab, ((0, 0), (0, HEAD_DIM - tab.shape[1])))


def _cmp1_paged_kernel(pt_ref, *refs, n_pg):
    page_refs = refs[:n_pg]
    w_ref, y_ref, ys_ref = refs[n_pg:]
    segs_per_page = PAGE_SIZE // C_CMP_STRIDE
    rows = segs_per_page * 8
    x2 = jnp.concatenate(
        [jnp.concatenate([ref[:, s, :, :].reshape(rows, HEAD_DIM) for s in range(C_CMP_STRIDE)], axis=1)
         for ref in page_refs], axis=0)
    y4 = _dot(x2.astype(BF16), w_ref[...])
    is_v = (lax.broadcasted_iota(jnp.int32, (x2.shape[0], 2 * HEAD_DIM), 0) % 8) >= C_KV_HEADS
    y = jnp.where(is_v, y4[:, 2 * HEAD_DIM:], y4[:, :2 * HEAD_DIM])
    ys_ref[0] = y[:, :HEAD_DIM]
    ys_ref[1] = y[:, HEAD_DIM:]
    for sub in range(8):
        y_ref[sub] = jnp.concatenate(
            [ys_ref[half, pl.ds(sub, n_pg * segs_per_page, stride=8), :] for half in range(2)], axis=1)


def _cmp1_paged(page_table, pool4d, w1cat4, *, sub_blk, n_pg):
    n_dec, n_pages = page_table.shape
    segs_per_page = PAGE_SIZE // C_CMP_STRIDE
    n_seg = n_pages * segs_per_page
    in_specs = [pl.BlockSpec((segs_per_page, C_CMP_STRIDE, 8, HEAD_DIM),
                             functools.partial(lambda n, j, pt, p: (pt[n, j * n_pg + p], 0, sub_blk, 0), p=p))
                for p in range(n_pg)]
    in_specs.append(pl.BlockSpec(w1cat4.shape, lambda n, j, pt: (0, 0)))
    return pl.pallas_call(
        functools.partial(_cmp1_paged_kernel, n_pg=n_pg),
        grid_spec=pltpu.PrefetchScalarGridSpec(
            num_scalar_prefetch=1,
            grid=(n_dec, n_pages // n_pg),
            in_specs=in_specs,
            out_specs=pl.BlockSpec((8, n_pg * segs_per_page, 2 * HEAD_DIM), lambda n, j, pt: (n, j, 0)),
            scratch_shapes=[pltpu.VMEM((2, n_pg * segs_per_page * 8, HEAD_DIM), F32)],
        ),
        out_shape=jax.ShapeDtypeStruct((n_dec * 8, n_seg, 2 * HEAD_DIM), F32),
        compiler_params=_params(2),
        name="cmp1_paged",
    )(page_table, *([pool4d] * n_pg), w1cat4)


def _nsa_decode_kernel(kvc_ref, qbd_ref, bias_c_ref, map_ref, gsum_ref, win_ref, new_ref, bias_w_ref,
                       oc_ref, ow_ref, sel_ref, *, n_cmp, n_blk, n_sub_w, k_sub_w):
    n_seg = kvc_ref.shape[1]
    sc = jnp.zeros((n_seg, HEAD_DIM), F32)
    for g in range(C_KV_HEADS):
        sc = sc + _dot(kvc_ref[g].astype(BF16), qbd_ref[0, g])
    valid = lax.broadcasted_iota(jnp.int32, (n_seg, HEAD_DIM), 0) < n_cmp
    lg = jnp.where(valid, sc * SOFTMAX_SCALE + bias_c_ref[...], NEG_INF)
    m = jnp.max(lg, axis=0, keepdims=True)
    e = jnp.where(valid, jnp.exp(lg - m), 0.0)
    p = e / jnp.maximum(jnp.sum(e, axis=0, keepdims=True), 1e-30)
    vc = jnp.concatenate([kvc_ref[C_KV_HEADS + g].astype(BF16) for g in range(C_KV_HEADS)], axis=1)
    _head_rows_store(oc_ref, _dot(p.astype(BF16), vc, _TN), jnp.ones((1, HEAD_DIM), F32))
    imp_h = _dot_f32_rhs(map_ref[...], p)
    imp = _dot_f32_lhs(imp_h, gsum_ref[...])
    blk = lax.broadcasted_iota(jnp.int32, imp.shape, 0)
    cur = n_blk - 1
    avail = blk <= cur
    forced = (blk == 0) | (blk == cur) | (blk == cur - 1)
    score = jnp.where(forced, jnp.inf, imp)
    score = jnp.where(avail, score, -jnp.inf)
    rank = _rank_rows(score, n_blk)
    sel_ref[0] = jnp.where(avail & (rank < min(C_SLC_TOPK, n_blk)), 1.0, 0.0)
    n_win = bias_w_ref.shape[0] - 8
    sw = jnp.zeros((n_win, HEAD_DIM), F32)
    s_new = jnp.zeros((8, HEAD_DIM), F32)
    v_parts, vn_parts = [], []
    for g in range(C_KV_HEADS):
        kg = win_ref[pl.ds(k_sub_w + g, n_win, stride=n_sub_w), :].astype(BF16)
        sw = sw + _dot(kg, qbd_ref[0, g])
        v_parts.append(win_ref[pl.ds(k_sub_w + C_KV_HEADS + g, n_win, stride=n_sub_w), :].astype(BF16))
        k_new = jnp.broadcast_to(new_ref[0, g:g + 1, :], (8, HEAD_DIM)).astype(BF16)
        s_new = s_new + _dot(k_new, qbd_ref[0, g])
        vn_parts.append(jnp.broadcast_to(new_ref[0, C_KV_HEADS + g:C_KV_HEADS + g + 1, :], (8, HEAD_DIM)).astype(BF16))
    lw = sw * SOFTMAX_SCALE + bias_w_ref[0:n_win, :]
    l_new = s_new[0:1] * SOFTMAX_SCALE + bias_w_ref[n_win + 1:n_win + 2, :]
    mw = jnp.maximum(jnp.max(lw, axis=0, keepdims=True), l_new)
    ew = jnp.exp(lw - mw)
    e_new = jnp.exp(l_new - mw)
    row8 = lax.broadcasted_iota(jnp.int32, (8, HEAD_DIM), 0)
    e_new8 = jnp.where(row8 == 0, jnp.broadcast_to(e_new, (8, HEAD_DIM)), 0.0)
    o_all = _dot(ew.astype(BF16), jnp.concatenate(v_parts, axis=1), _TN)
    o_all = o_all + _dot(e_new8.astype(BF16), jnp.concatenate(vn_parts, axis=1), _TN)
    _head_rows_store(ow_ref, o_all, jnp.sum(ew, axis=0, keepdims=True) + e_new)


def _nsa_decode(kvc, qbd, bias_c, cmap_t, gsum, win2d, new_win, bias_w, *, slot, n_blk):
    n_dec = qbd.shape[0]
    n_seg = kvc.shape[1]
    n_win = bias_w.shape[0] - 8
    n_blk_pad = cmap_t.shape[0]
    win_rows = n_win * 8
    o_sds = jax.ShapeDtypeStruct((n_dec, C_HEADS, HEAD_DIM), F32)
    o_spec = pl.BlockSpec((1, C_HEADS, HEAD_DIM), lambda n: (n, 0, 0))
    return pl.pallas_call(
        functools.partial(_nsa_decode_kernel, n_cmp=n_seg - 1, n_blk=n_blk, n_sub_w=8, k_sub_w=0),
        grid=(n_dec,),
        in_specs=[pl.BlockSpec((8, n_seg, HEAD_DIM), lambda n: (n, 0, 0)),
                  pl.BlockSpec((1, C_KV_HEADS, HEAD_DIM, HEAD_DIM), lambda n: (n, 0, 0, 0)),
                  pl.BlockSpec(bias_c.shape, lambda n: (0, 0)),
                  pl.BlockSpec(cmap_t.shape, lambda n: (0, 0)),
                  pl.BlockSpec(gsum.shape, lambda n: (0, 0)),
                  pl.BlockSpec((win_rows, HEAD_DIM), lambda n: (slot * n_dec + n, 0)),
                  pl.BlockSpec((1, 2 * C_KV_HEADS, HEAD_DIM), lambda n: (n, 0, 0)),
                  pl.BlockSpec(bias_w.shape, lambda n: (0, 0))],
        out_specs=[o_spec, o_spec, pl.BlockSpec((1, n_blk_pad, HEAD_DIM), lambda n: (n, 0, 0))],
        out_shape=[o_sds, o_sds, jax.ShapeDtypeStruct((n_dec, n_blk_pad, HEAD_DIM), F32)],
        compiler_params=_params(1),
        name="nsa_decode",
    )(kvc, qbd, bias_c, cmap_t, gsum, win2d, new_win, bias_w)


def _mixer_a_decode(x, g_mix, li, w_in, w_out, slot, bufs, rel_bias):
    n_dec = x.shape[0]
    n_in = w_in.shape[2]
    proj = _norm_matmul(x, g_mix, li, w_in, slot, n_in, n_dec, 512)
    qkv = proj.reshape(n_dec, len(A_PATTERNS), 3, A_HEADS, HEAD_DIM)
    dist = np.stack([np.concatenate([(w // d - np.arange(w // d)) * d, [0]]) for w, d in A_PATTERNS])
    bias = _bias_by_dist(rel_bias[:, :A_HEADS], dist)
    bias = jnp.broadcast_to(jnp.transpose(bias, (1, 2, 0))[..., None], dist.shape + (A_HEADS, HEAD_DIM))
    o = _a_decode(qkv, bufs, slot, bias)
    x = _outproj([o.reshape(n_dec, -1)], [], w_out, slot, x, "plain", n_dec, 512)
    return x, proj


def _mixer_b_decode(x, g_mix, li, w_in, w_out, slot, pool, page_table, bias_dec):
    n_dec = x.shape[0]
    n_in = w_in.shape[2]
    proj = _norm_matmul(x, g_mix, li, w_in, slot, n_in, n_dec, 512)
    nq = B_HEADS * HEAD_DIM
    qbd = _block_diag_q(proj[:, :nq].reshape(n_dec, B_HEADS, HEAD_DIM))
    new_kv = proj[:, nq:].reshape(n_dec, 2 * B_KV_HEADS, HEAD_DIM)
    n_sub = pool.shape[2] * 2 * B_KV_HEADS
    o = _paged_decode(page_table, pool.reshape(-1, HEAD_DIM), qbd, new_kv, bias_dec, None, n_pg=8, n_sub=n_sub,
                      k_sub=slot * 2 * B_KV_HEADS, v_sub=slot * 2 * B_KV_HEADS + B_KV_HEADS, blk=B_BLOCK, topk=B_TOPK)
    x = _outproj([o.reshape(n_dec, -1)], [], w_out, slot, x, "plain", n_dec, 512)
    return x, proj


def _mixer_c_decode(x, g_mix, li, w_in, w_gate, w_out, slot, cmp_w, pool, win_buf, page_table, bias_dec, rel_bias):
    n_dec = x.shape[0]
    n_pages = page_table.shape[1]
    past = n_pages * PAGE_SIZE
    assert win_buf.shape[2] == C_WINDOW, "window buffer must hold exactly C_WINDOW past rows"
    n_main = (C_HEADS + 6 * C_KV_HEADS) * HEAD_DIM
    proj = _norm_matmul(x, g_mix, li, w_in, slot, n_main, n_dec, 512)
    gate = _norm_matmul(x, g_mix, li, w_gate, 0, HEAD_DIM, n_dec, HEAD_DIM)
    nq = C_HEADS * HEAD_DIM
    kvw = C_KV_HEADS * HEAD_DIM
    qbd = _block_diag_q(proj[:, :nq].reshape(n_dec, C_HEADS, HEAD_DIM))
    new_slc = proj[:, nq + 2 * kvw:nq + 4 * kvw].reshape(n_dec, 2 * C_KV_HEADS, HEAD_DIM)
    new_win = proj[:, nq + 4 * kvw:nq + 6 * kvw].reshape(n_dec, 2 * C_KV_HEADS, HEAD_DIM)
    w1cat, pe8, w1, w2 = cmp_w
    w1cat4 = jnp.concatenate([w1cat[0], w1cat[1]], axis=1)
    n_layers = pool.shape[2]
    n_sub = n_layers * 4 * C_KV_HEADS
    pool4d = pool.reshape(-1, C_CMP_STRIDE, n_sub, HEAD_DIM)
    y = _cmp1_paged(page_table, pool4d, w1cat4, sub_blk=slot * 2, n_pg=8)
    kvc = _cmp2(y, pe8, w1, w2)
    n_seg = kvc.shape[1]
    n_cmp = n_seg - 1
    n_blk = past // C_SLC_BLOCK + 1
    n_blk_pad = -(-n_blk // 8) * 8
    dist_c = past - (np.arange(n_seg) * C_CMP_STRIDE + (C_CMP_BLOCK - 1))
    bias_c = jnp.pad(rel_bias[jnp.asarray(_bucket_np(dist_c))], ((0, 0), (0, HEAD_DIM - C_HEADS)))
    cmap_t = jnp.asarray(_cmp_map_np(n_seg, n_cmp, n_blk_pad).T, BF16)
    lanes = np.arange(HEAD_DIM)
    gsum = jnp.asarray((lanes[:, None] // GQA_GROUP == lanes[None, :] // GQA_GROUP)
                       & (lanes[:, None] < C_HEADS) & (lanes[None, :] < C_HEADS), BF16)
    bias_w = _decode_bias(rel_bias, past, C_WINDOW)
    o_cmp, o_win, sel = _nsa_decode(kvc, qbd, bias_c, cmap_t, gsum, win_buf.reshape(-1, HEAD_DIM), new_win, bias_w,
                                    slot=slot, n_blk=n_blk)
    o_slc = _paged_decode(page_table, pool.reshape(-1, HEAD_DIM), qbd, new_slc, bias_dec, sel, n_pg=4, n_sub=n_sub,
                          k_sub=slot * 16 + 2 * C_KV_HEADS, v_sub=slot * 16 + 3 * C_KV_HEADS, blk=C_SLC_BLOCK, topk=None)
    x = _outproj([o_cmp.reshape(n_dec, -1), o_slc.reshape(n_dec, -1), o_win.reshape(n_dec, -1)], [gate], w_out, slot, x,
                 "gate3", n_dec, 512)
    return x, proj


def _shifted(buf, new_rows):
    return jnp.concatenate([buf[:, :, 1:], new_rows[:, :, None]], axis=2)


def kernel(x_prompt, x_sample, state_a_w128, state_a_w512, state_a_w2048, cache_b_kv, cache_c_kv, state_c_win, page_table, rel_bias, norm_mix, norm_ffn, norm_final, w_in_a, w_out_a, w_in_b, w_out_b, w_in_c, w_out_c, cmp_pe, cmp_w1, cmp_w2, w_up, w_down):
    b, s, d = x_prompt.shape
    n_dec, t_dec, _ = x_sample.shape
    depth = norm_mix.shape[0]
    assert t_dec == 1 and s % B_BLOCK == 0 and s % ATTN_TILE == 0
    past = page_table.shape[1] * PAGE_SIZE
    assert past % B_BLOCK == 0 and past % C_SLC_BLOCK == 0

    xp = x_prompt.reshape(b * s, d)
    xs = x_sample.reshape(n_dec, d)
    g_mix = norm_mix.reshape(depth, 1, d)
    g_ffn = norm_ffn.reshape(depth, 1, d)
    g_fin = norm_final.reshape(1, d)
    tabs = _tile_tables(rel_bias, ATTN_TILE)
    tabs_win = _tile_tables(rel_bias, ATTN_TILE, window=C_WINDOW)
    tabs_a = [_tile_tables(rel_bias[:, :A_HEADS], ATTN_TILE, window=w // dl, dist_scale=dl) for w, dl in A_PATTERNS]
    far = rel_bias[NUM_BUCKETS - 1]
    bias_dec = _decode_bias(rel_bias, past, DECODE_CHUNK)
    n_gate = 3 * C_HEADS
    n_main_c = (C_HEADS + 6 * C_KV_HEADS) * HEAD_DIM
    w_gate = jnp.pad(w_in_c[:, :, n_main_c:], ((0, 0), (0, 0), (0, HEAD_DIM - n_gate)))
    a_bufs = (state_a_w128, state_a_w512, state_a_w2048)
    w_in_a, w_in_b, w_in_c = w_in_a.astype(BF16), w_in_b.astype(BF16), w_in_c.astype(BF16)

    a_p, a_s = [[], [], []], [[], [], []]
    b_p, b_s, c_p, c_s, cw_p, cw_s = [], [], [], [], [], []
    for li in range(depth):
        kind, slot = li % 3, li // 3
        if kind == 0:
            xp, pp = _mixer_a_prompt(xp, b, g_mix, li, w_in_a, w_out_a, slot, tabs_a, far)
            xs, ps = _mixer_a_decode(xs, g_mix, li, w_in_a, w_out_a, slot, a_bufs, rel_bias)
            gw = 3 * A_HEADS * HEAD_DIM
            kvw = 2 * A_HEADS * HEAD_DIM
            for g, (window, _) in enumerate(A_PATTERNS):
                c0 = g * gw + A_HEADS * HEAD_DIM
                a_p[g].append(pp[:, s - min(window, s):, c0:c0 + kvw].reshape(b, min(window, s), 2, A_HEADS, HEAD_DIM))
                a_s[g].append(ps[:, c0:c0 + kvw].reshape(n_dec, 2, A_HEADS, HEAD_DIM))
        elif kind == 1:
            xp, pp = _mixer_b_prompt(xp, b, g_mix, li, w_in_b, w_out_b, slot, tabs, far)
            xs, ps = _mixer_b_decode(xs, g_mix, li, w_in_b, w_out_b, slot, cache_b_kv, page_table, bias_dec)
            nq = B_HEADS * HEAD_DIM
            b_p.append(pp[:, :, nq:].reshape(b, s, 2, B_KV_HEADS, HEAD_DIM))
            b_s.append(ps[:, nq:].reshape(n_dec, 1, 2, B_KV_HEADS, HEAD_DIM))
        else:
            cmp_w = _cmp_weights(cmp_pe, cmp_w1, cmp_w2, slot)
            xp, pp = _mixer_c_prompt(xp, b, g_mix, li, w_in_c, w_gate[slot:slot + 1], w_out_c, slot, cmp_w, tabs, tabs_win,
                                     far, rel_bias)
            xs, ps = _mixer_c_decode(xs, g_mix, li, w_in_c, w_gate[slot:slot + 1], w_out_c, slot, cmp_w, cache_c_kv,
                                     state_c_win, page_table, bias_dec, rel_bias)
            nq = C_HEADS * HEAD_DIM
            kvw = C_KV_HEADS * HEAD_DIM
            c_p.append(pp[:, :, nq:nq + 4 * kvw].reshape(b, s, 4, C_KV_HEADS, HEAD_DIM))
            c_s.append(ps[:, nq:nq + 4 * kvw].reshape(n_dec, 1, 4, C_KV_HEADS, HEAD_DIM))
            wlen = min(C_WINDOW, s)
            cw_p.append(pp[:, s - wlen:, nq + 4 * kvw:nq + 6 * kvw].reshape(b, wlen, 2, C_KV_HEADS, HEAD_DIM))
            cw_s.append(ps[:, nq + 4 * kvw:nq + 6 * kvw].reshape(n_dec, 2, C_KV_HEADS, HEAD_DIM))
        last = li == depth - 1
        xp = _mlp(xp, g_ffn, li, w_up, w_down, g_fin if last else None, _row_tile(b * s, PROMPT_ROW_TILE), 512)
        xs = _mlp(xs, g_ffn, li, w_up, w_down, g_fin if last else None, n_dec, 512)

    outs = [xp.reshape(b, s, d), xs.reshape(n_dec, 1, d)]
    for g in range(len(A_PATTERNS)):
        outs.append(jnp.stack(a_p[g], axis=0))
        outs.append(_shifted(a_bufs[g], jnp.stack(a_s[g], axis=0)))
    outs += [jnp.stack(b_p, axis=2), jnp.stack(b_s, axis=2), jnp.stack(c_p, axis=2), jnp.stack(c_s, axis=2),
             jnp.stack(cw_p, axis=0), _shifted(state_c_win, jnp.stack(cw_s, axis=0))]
    return tuple(outs)
```

```python
import functools
import math

import jax
import jax.numpy as jnp
import numpy as np
from jax import lax
from jax.experimental import pallas as pl
from jax.experimental.pallas import tpu as pltpu

F32 = jnp.float32
BF16 = jnp.bfloat16

HEAD_DIM = 128
RMS_EPS = 1e-6
NEG_INF = -1e30
SOFTMAX_SCALE = HEAD_DIM ** -0.5
LOG2E = math.log2(math.e)
PAGE_SIZE = 128

NUM_BUCKETS = 32
MAX_DISTANCE = 128

A_PATTERNS = ((128, 1), (512, 4), (2048, 16))
A_HEADS = 8
B_HEADS = 16
B_KV_HEADS = 4
B_BLOCK = 256
B_TOPK = 3
C_HEADS = 16
C_KV_HEADS = 4
C_CMP_BLOCK = 32
C_CMP_STRIDE = 16
C_SLC_BLOCK = 64
C_SLC_TOPK = 16
C_WINDOW = 512
GQA_GROUP = 4

ATTN_TILE = 256
PROMPT_ROW_TILE = 1024
VMEM_LIMIT_BYTES = 56 * 1024 * 1024


def _params(n_axes, vmem=VMEM_LIMIT_BYTES):
    return pltpu.CompilerParams(dimension_semantics=("arbitrary",) * n_axes, vmem_limit_bytes=vmem)


def _bucket_np(dist):
    exact = NUM_BUCKETS // 2
    d = np.maximum(np.asarray(dist, np.int64), 0)
    far = exact + (np.log(np.maximum(d, exact).astype(np.float64) / exact)
                   / math.log(MAX_DISTANCE / exact) * (NUM_BUCKETS - exact)).astype(np.int64)
    return np.where(d < exact, d, np.minimum(far, NUM_BUCKETS - 1)).astype(np.int32)


def _bias_by_dist(rel_bias, dist_np):
    tab = rel_bias[jnp.asarray(_bucket_np(dist_np))]
    return jnp.moveaxis(tab, -1, 0)


def _toeplitz(w, t):
    return jnp.tile(w, (1, t))[:, :2 * t * t].reshape(w.shape[0], t, 2 * t)[:, :, :t]


def _tile_tables(rel_bias, t, window=None, dil=1, dist_scale=1):
    assert window is None or window < t or (window % t == 0 and window >= 2 * t)
    k = np.arange(2 * t + 1)
    e = np.where(k <= t, -k, 2 * t + 1 - k)

    def table(dist, ok):
        vals = rel_bias[jnp.asarray(_bucket_np(dist * dist_scale))].T
        return _toeplitz(jnp.where(jnp.asarray(ok)[None], vals, NEG_INF), t)

    def allowed(dist):
        ok = dist >= 0
        if window is not None:
            ok &= dist <= window
        if dil > 1:
            ok &= dist % dil == 0
        return ok

    tabs = [table(e, allowed(e)), table(t + e, allowed(t + e))]
    if dil > 1:
        tabs.append(table(2 * t + e, (2 * t + e) % dil == 0))
    if window is not None and window >= 2 * t:
        tabs.append(table(window + e, allowed(window + e)))
    return jnp.stack(tabs, axis=1)


def _rms(x, g):
    return x * lax.rsqrt(jnp.mean(x * x, axis=-1, keepdims=True) + RMS_EPS) * g


def _split_bf16(x, parts):
    out = []
    for _ in range(parts):
        p = x.astype(BF16)
        out.append(p)
        x = x - p.astype(F32)
    return out


def _dot(a, b, dims=(((1,), (0,)), ((), ()))):
    return lax.dot_general(a, b, dims, preferred_element_type=F32)


_NT = (((1,), (1,)), ((), ()))
_TN = (((0,), (0,)), ((), ()))


def _dot_f32_lhs(a, b_bf16, dims=(((1,), (0,)), ((), ()))):
    return sum(_dot(p, b_bf16, dims) for p in _split_bf16(a, 3))


def _dot_f32_rhs(a_bf16, b, dims=(((1,), (0,)), ((), ()))):
    return sum(_dot(a_bf16, p, dims) for p in _split_bf16(b, 3))


def _dot_f32_both(a, b, dims):
    a_hi, a_lo = _split_bf16(a, 2)
    b_hi, b_lo = _split_bf16(b, 2)
    return _dot(a_hi, b_hi, dims) + (_dot(a_hi, b_lo, dims) + _dot(a_lo, b_hi, dims))


def _rank_rows(score, n_rows):
    row = lax.broadcasted_iota(jnp.int32, score.shape, 0)
    rank = jnp.zeros(score.shape, jnp.int32)
    for m in range(n_rows):
        sm = score[m:m + 1, :]
        beats = (sm > score) | ((sm == score) & (row > m))
        rank = rank + beats.astype(jnp.int32)
    return rank


def _rank_lanes(score, n_lanes):
    n_rows, width = score.shape
    n_pad = -(-n_lanes // 8) * 8
    rank_t = _rank_rows(score.T[:n_pad], n_lanes).astype(F32)
    if n_pad < width:
        rank_t = jnp.concatenate([rank_t, jnp.zeros((width - n_pad, n_rows), F32)], axis=0)
    return rank_t.T


def _norm_matmul_kernel(x_ref, g_ref, w_ref, o_ref, h_ref):
    @pl.when(pl.program_id(1) == 0)
    def _():
        h_ref[...] = _rms(x_ref[...], g_ref[...]).astype(BF16)

    o_ref[...] = _dot(h_ref[...], w_ref[...].astype(BF16))


def _norm_matmul2_kernel(x_ref, xs_ref, g_ref, w_ref, o_ref, os_ref, h_ref, hs_ref):
    i = pl.program_id(0)
    j = pl.program_id(1)

    @pl.when(j == 0)
    def _():
        h_ref[...] = _rms(x_ref[...], g_ref[...]).astype(BF16)

    @pl.when((i == 0) & (j == 0))
    def _():
        hs_ref[...] = _rms(xs_ref[...], g_ref[...]).astype(BF16)

    w = w_ref[...].astype(BF16)
    o_ref[...] = _dot(h_ref[...], w)

    @pl.when(i == 0)
    def _():
        os_ref[j] = _dot(hs_ref[...], w)


def _norm_matmul(x, g_all, li, w_all, slot, n_cols, tm, tn, xs=None):
    m, k = x.shape
    n_j = n_cols // tn
    x_spec = pl.BlockSpec((tm, k), lambda i, j: (i, 0), pipeline_mode=pl.Buffered(1))
    g_spec = pl.BlockSpec((None, 1, k), lambda i, j: (li, 0, 0))
    w_spec = pl.BlockSpec((None, k, tn), lambda i, j: (slot, 0, j))
    o_spec = pl.BlockSpec((tm, tn), lambda i, j: (i, j))
    if xs is None:
        return pl.pallas_call(
            _norm_matmul_kernel,
            grid=(m // tm, n_j),
            in_specs=[x_spec, g_spec, w_spec],
            out_specs=o_spec,
            out_shape=jax.ShapeDtypeStruct((m, n_cols), F32),
            scratch_shapes=[pltpu.VMEM((tm, k), BF16)],
            compiler_params=_params(2),
            name="norm_matmul",
        )(x, g_all, w_all)
    ms = xs.shape[0]
    out, out_s = pl.pallas_call(
        _norm_matmul2_kernel,
        grid=(m // tm, n_j),
        in_specs=[x_spec, pl.BlockSpec((ms, k), lambda i, j: (0, 0)), g_spec, w_spec],
        out_specs=[o_spec, pl.BlockSpec((n_j, ms, tn), lambda i, j: (0, 0, 0))],
        out_shape=[jax.ShapeDtypeStruct((m, n_cols), F32), jax.ShapeDtypeStruct((n_j, ms, tn), F32)],
        scratch_shapes=[pltpu.VMEM((tm, k), BF16), pltpu.VMEM((ms, k), BF16)],
        compiler_params=_params(2),
        name="norm_matmul2",
    )(x, xs, g_all, w_all)
    return out, jnp.swapaxes(out_s, 0, 1).reshape(ms, n_cols)


def _outproj_kernel(*refs, mode, n_heads):
    if mode == "plain":
        o_refs, extra, (w_ref, x_ref, out_ref, a_ref) = refs[:1], (), refs[1:]
    elif mode == "lse3":
        o_refs, extra, (w_ref, x_ref, out_ref, a_ref) = refs[:3], refs[3:6], refs[6:]
    else:
        o_refs, extra, (w_ref, x_ref, out_ref, a_ref) = refs[:3], refs[3:4], refs[4:]

    @pl.when(pl.program_id(1) == 0)
    def _():
        if mode == "plain":
            a = o_refs[0][...]
        elif mode == "lse3":
            lse = [r[...] for r in extra]
            mx = jnp.maximum(jnp.maximum(lse[0], lse[1]), lse[2])
            e = [jnp.exp(l - mx) for l in lse]
            den = e[0] + e[1] + e[2]
            a = sum((e[g] / den) * o_refs[g][...] for g in range(3))
        else:
            gate = jax.nn.sigmoid(extra[0][...])
            tm = gate.shape[0]
            heads = []
            for h in range(n_heads):
                acc = jnp.zeros((tm, HEAD_DIM), F32)
                for br in range(3):
                    gcol = gate[:, 3 * h + br:3 * h + br + 1]
                    acc = acc + gcol * o_refs[br][:, h * HEAD_DIM:(h + 1) * HEAD_DIM]
                heads.append(acc)
            a = jnp.concatenate(heads, axis=1)
        a_ref[...] = a.astype(BF16)

    out_ref[...] = x_ref[...] + _dot(a_ref[...], w_ref[...].astype(BF16))


def _outproj(o_list, extra_list, w_all, slot, x, mode, tm, tn):
    m, k = o_list[0].shape
    n = x.shape[1]
    row_spec = lambda width: pl.BlockSpec((tm, width), lambda i, j: (i, 0))
    in_specs = [row_spec(k) for _ in o_list] + [row_spec(e.shape[1]) for e in extra_list]
    in_specs += [pl.BlockSpec((None, k, tn), lambda i, j: (slot, 0, j)),
                 pl.BlockSpec((tm, tn), lambda i, j: (i, j))]
    return pl.pallas_call(
        functools.partial(_outproj_kernel, mode=mode, n_heads=k // HEAD_DIM),
        grid=(m // tm, n // tn),
        in_specs=in_specs,
        out_specs=pl.BlockSpec((tm, tn), lambda i, j: (i, j)),
        out_shape=jax.ShapeDtypeStruct((m, n), F32),
        scratch_shapes=[pltpu.VMEM((tm, k), BF16)],
        compiler_params=_params(2),
        name="outproj_" + mode,
    )(*o_list, *extra_list, w_all, x)


def _mlp_kernel(*refs, final):
    if final:
        x_ref, xs_ref, g_ref, wu_ref, wd_ref, gf_ref, o_ref, os_ref, h_ref, hs_ref = refs
    else:
        x_ref, xs_ref, g_ref, wu_ref, wd_ref, o_ref, os_ref, h_ref, hs_ref = refs
    i = pl.program_id(0)
    f = pl.program_id(1)
    last = pl.num_programs(1) - 1

    @pl.when(f == 0)
    def _():
        x = x_ref[...]
        h_ref[...] = _rms(x, g_ref[...]).astype(BF16)
        o_ref[...] = x

    @pl.when((i == 0) & (f == 0))
    def _():
        xs = xs_ref[...]
        hs_ref[...] = _rms(xs, g_ref[...]).astype(BF16)
        os_ref[...] = xs

    wu = wu_ref[...].astype(BF16)
    wd = wd_ref[...].astype(BF16)

    def up_act_down(h):
        u = _dot(h, wu)
        return _dot(jnp.square(jnp.maximum(u, 0.0)).astype(BF16), wd)

    o_ref[...] += up_act_down(h_ref[...])

    @pl.when(i == 0)
    def _():
        os_ref[...] += up_act_down(hs_ref[...])

    if final:
        @pl.when(f == last)
        def _():
            o_ref[...] = _rms(o_ref[...], gf_ref[...])

        @pl.when((i == 0) & (f == last))
        def _():
            os_ref[...] = _rms(os_ref[...], gf_ref[...])


def _mlp(x, xs, g_all, li, w_up, w_down, final_g, tm, tf):
    m, d = x.shape
    ms = xs.shape[0]
    d_ff = w_up.shape[2]
    in_specs = [
        pl.BlockSpec((tm, d), lambda i, f: (i, 0), pipeline_mode=pl.Buffered(1)),
        pl.BlockSpec((ms, d), lambda i, f: (0, 0)),
        pl.BlockSpec((None, 1, d), lambda i, f: (li, 0, 0)),
        pl.BlockSpec((None, d, tf), lambda i, f: (li, 0, f)),
        pl.BlockSpec((None, tf, d), lambda i, f: (li, f, 0)),
    ]
    args = [x, xs, g_all, w_up, w_down]
    if final_g is not None:
        in_specs.append(pl.BlockSpec((1, d), lambda i, f: (0, 0)))
        args.append(final_g)
    return pl.pallas_call(
        functools.partial(_mlp_kernel, final=final_g is not None),
        grid=(m // tm, d_ff // tf),
        in_specs=in_specs,
        out_specs=[pl.BlockSpec((tm, d), lambda i, f: (i, 0)), pl.BlockSpec((ms, d), lambda i, f: (0, 0))],
        out_shape=[jax.ShapeDtypeStruct((m, d), F32), jax.ShapeDtypeStruct((ms, d), F32)],
        scratch_shapes=[pltpu.VMEM((tm, d), BF16), pltpu.VMEM((ms, d), BF16)],
        compiler_params=_params(2),
        name="mlp",
    )(*args)


def _flash_kernel_old(*refs, group, t, window, dil, sel_blk, sel_per_head, want_lse, head0):
    it = iter(refs)
    q_ref, k_ref, v_ref, tab_ref, far_ref = (next(it) for _ in range(5))
    sel_ref = next(it) if sel_blk else None
    o_ref = next(it)
    lse_ref = next(it) if want_lse else None
    kb_ref, vb_ref, qs_ref, m_ref, l_ref, acc_ref = (next(it) for _ in range(6))

    kvh = pl.program_id(1)
    qi = pl.program_id(2)
    rows = group * t

    @pl.when(qi == 0)
    def _():
        kb_ref[...] = k_ref[0].astype(BF16)
        vb_ref[...] = v_ref[0].astype(BF16)

    for j in range(group):
        qs_ref[j * t:(j + 1) * t, :] = q_ref[0, :, j * HEAD_DIM:(j + 1) * HEAD_DIM].astype(BF16)
    m_ref[...] = jnp.full((rows, HEAD_DIM), NEG_INF, F32)
    l_ref[...] = jnp.zeros((rows, HEAD_DIM), F32)
    acc_ref[...] = jnp.zeros((rows, HEAD_DIM), F32)

    delta = (lax.broadcasted_iota(jnp.int32, (t, t), 0) - lax.broadcasted_iota(jnp.int32, (t, t), 1))

    def tile(kj, kind):
        start = pl.multiple_of(kj * t, t)
        k = kb_ref[pl.ds(start, t), :]
        v = vb_ref[pl.ds(start, t), :]
        s = _dot(qs_ref[...], k, _NT) * SOFTMAX_SCALE
        dist = delta + (qi - kj) * t
        ok = None
        if kind == "diag":
            ok = dist >= 0
        if window is not None:
            w_ok = dist <= window
            ok = w_ok if ok is None else ok & w_ok
        if dil > 1:
            d_ok = (dist & (dil - 1)) == 0
            ok = d_ok if ok is None else ok & d_ok
        sel_ok = None
        if sel_blk:
            blk_of_key = (start + lax.broadcasted_iota(jnp.int32, (HEAD_DIM, t), 1)) // sel_blk
            expand = (lax.broadcasted_iota(jnp.int32, (HEAD_DIM, t), 0) == blk_of_key)
            expand = jnp.where(expand, 1.0, 0.0).astype(BF16)
            if sel_per_head:
                sel_rows = jnp.concatenate([sel_ref[0, j] for j in range(group)], axis=0)
            else:
                sel_rows = sel_ref[0, 0]
            sel_ok = _dot(sel_rows.astype(BF16), expand) > 0.5
        pieces = []
        for j in range(group):
            sj = s[j * t:(j + 1) * t]
            if kind == "far":
                sj = sj + far_ref[head0 + kvh * group + j]
            else:
                sj = sj + tab_ref[j, 0 if kind == "diag" else 1]
            mask = ok
            if sel_blk:
                sel_j = sel_ok[j * t:(j + 1) * t] if sel_per_head else sel_ok
                mask = sel_j if mask is None else mask & sel_j
            pieces.append((sj, mask))
        if pieces[0][1] is not None:
            s = jnp.concatenate([jnp.where(mk, sj, NEG_INF) for sj, mk in pieces], axis=0)
            mask_all = jnp.concatenate([mk for _, mk in pieces], axis=0)
        else:
            s = jnp.concatenate([sj for sj, _ in pieces], axis=0)
            mask_all = None
        m_prev = m_ref[...]
        m_next = jnp.maximum(m_prev, jnp.max(s, axis=1, keepdims=True))
        p = jnp.exp(s - jnp.concatenate([m_next] * (t // HEAD_DIM), axis=1))
        if mask_all is not None:
            p = jnp.where(mask_all, p, 0.0)
        alpha = jnp.exp(m_prev - m_next)
        l_ref[...] = alpha * l_ref[...] + jnp.sum(p, axis=1, keepdims=True)
        acc_ref[...] = alpha * acc_ref[...] + _dot(p.astype(BF16), v)
        m_ref[...] = m_next

    lo = 0 if window is None else jnp.maximum(qi * t - window, 0) // t

    def far_body(kj, carry):
        tile(kj, "far")
        return carry

    lax.fori_loop(lo, jnp.maximum(qi - 1, lo), far_body, 0)

    @pl.when(qi - 1 >= lo)
    def _():
        tile(qi - 1, "near")

    tile(qi, "diag")

    l = l_ref[...]
    out = acc_ref[...] / l
    for j in range(group):
        o_ref[0, :, j * HEAD_DIM:(j + 1) * HEAD_DIM] = out[j * t:(j + 1) * t]
    if want_lse:
        lse = m_ref[...] + jnp.log(l)
        for j in range(group):
            lse_ref[0, :, j * HEAD_DIM:(j + 1) * HEAD_DIM] = lse[j * t:(j + 1) * t]


def _flash_old(proj, tabs, far, sel, *, n_kv, group, q_col, k_col, v_col, head0=0,
           window=None, dil=1, sel_blk=0, sel_per_head=False, want_lse=False):
    b, s, _ = proj.shape
    t = ATTN_TILE
    gw = group * HEAD_DIM
    in_specs = [
        pl.BlockSpec((1, t, gw), lambda bi, h, qi: (bi, qi, q_col // group + h)),
        pl.BlockSpec((1, s, HEAD_DIM), lambda bi, h, qi: (bi, 0, k_col + h)),
        pl.BlockSpec((1, s, HEAD_DIM), lambda bi, h, qi: (bi, 0, v_col + h)),
        pl.BlockSpec((group, 2, t, t), lambda bi, h, qi: (head0 // group + h, 0, 0, 0)),
        pl.BlockSpec(memory_space=pltpu.SMEM),
    ]
    args = [proj, proj, proj, tabs, far]
    if sel_blk:
        if sel_per_head:
            in_specs.append(pl.BlockSpec((1, group, t, HEAD_DIM), lambda bi, h, qi: (bi, h, qi, 0)))
        else:
            in_specs.append(pl.BlockSpec((1, 1, t, HEAD_DIM), lambda bi, h, qi: (bi, h, qi, 0)))
        args.append(sel)
    out_spec = pl.BlockSpec((1, t, gw), lambda bi, h, qi: (bi, qi, h))
    out_sds = jax.ShapeDtypeStruct((b, s, n_kv * gw), F32)
    rows = group * t
    return pl.pallas_call(
        functools.partial(_flash_kernel, group=group, t=t, window=window, dil=dil, sel_blk=sel_blk,
                          sel_per_head=sel_per_head, want_lse=want_lse, head0=head0),
        grid=(b, n_kv, s // t),
        in_specs=in_specs,
        out_specs=[out_spec, out_spec] if want_lse else out_spec,
        out_shape=[out_sds, out_sds] if want_lse else out_sds,
        scratch_shapes=[pltpu.VMEM((s, HEAD_DIM), BF16), pltpu.VMEM((s, HEAD_DIM), BF16),
                        pltpu.VMEM((rows, HEAD_DIM), BF16), pltpu.VMEM((rows, HEAD_DIM), F32),
                        pltpu.VMEM((rows, HEAD_DIM), F32), pltpu.VMEM((rows, HEAD_DIM), F32)],
        compiler_params=_params(3),
        name="flash",
    )(*args)


def _flash_kernel(*refs, group, t, w_tiles, far_tab, edge_tab, sel_blk, sel_per_head, want_lse, head0):
    it = iter(refs)
    q_ref, k_ref, v_ref, tab_ref, far_ref = (next(it) for _ in range(5))
    sel_ref = next(it) if sel_blk else None
    o_ref = next(it)
    lse_ref = next(it) if want_lse else None
    kb_ref, vb_ref, qs_ref, s_ref, m_ref, l_ref, acc_ref = (next(it) for _ in range(7))

    kvh = pl.program_id(1)
    qi = pl.program_id(2)
    rows = group * t

    @pl.when(qi == 0)
    def _():
        kb_ref[...] = k_ref[0].astype(BF16)
        vb_ref[...] = v_ref[0].astype(BF16)

    for j in range(group):
        qs_ref[j * t:(j + 1) * t, :] = q_ref[0, :, j * HEAD_DIM:(j + 1) * HEAD_DIM].astype(BF16)

    def tile(kj, tab, first=False):
        start = pl.multiple_of(kj * t, t)
        k = kb_ref[pl.ds(start, t), :]
        v = vb_ref[pl.ds(start, t), :]
        sel_add = None
        if sel_blk and not sel_per_head:
            blk_of_key = (start + lax.broadcasted_iota(jnp.int32, (HEAD_DIM, t), 1)) // sel_blk
            expand = lax.broadcasted_iota(jnp.int32, (HEAD_DIM, t), 0) == blk_of_key
            expand = jnp.where(expand, 1.0, 0.0).astype(BF16)
            sel_add = _dot(sel_ref[0, 0].astype(BF16), expand)
        for j in range(group):
            hs = slice(j * t, (j + 1) * t)
            sj = _dot(qs_ref[hs, :], k, _NT) * SOFTMAX_SCALE
            sj = sj + (far_ref[head0 + kvh * group + j] if tab is None else tab_ref[j, tab])
            if sel_blk and sel_per_head:
                assert sel_blk == t
                lane = lax.broadcasted_iota(jnp.int32, (t, HEAD_DIM), 1)
                sj = sj + jnp.sum(jnp.where(lane == kj, sel_ref[0, j], 0.0), axis=1, keepdims=True)
            elif sel_blk:
                sj = sj + sel_add
            m_cur = jnp.max(jnp.maximum(sj[:, :HEAD_DIM], sj[:, HEAD_DIM:]), axis=1, keepdims=True)
            if first:
                m_next = jnp.broadcast_to(m_cur, (t, HEAD_DIM))
            else:
                m_prev = m_ref[hs, :]
                m_next = jnp.maximum(m_prev, m_cur)
            p = jnp.exp(sj - jnp.concatenate([m_next] * (t // HEAD_DIM), axis=1))
            p_sum = jnp.sum(p[:, :HEAD_DIM] + p[:, HEAD_DIM:], axis=1, keepdims=True)
            pv = _dot(p.astype(BF16), v)
            if first:
                l_ref[hs, :] = jnp.broadcast_to(p_sum, (t, HEAD_DIM))
                acc_ref[hs, :] = pv
            else:
                alpha = jnp.exp(m_prev - m_next)
                l_ref[hs, :] = alpha * l_ref[hs, :] + p_sum
                acc_ref[hs, :] = alpha * acc_ref[hs, :] + pv
            m_ref[hs, :] = m_next

    tile(qi, 0, first=True)

    @pl.when(qi >= 1)
    def _():
        tile(qi - 1, 1)

    def far_body(kj, carry):
        tile(kj, far_tab)
        return carry

    if w_tiles is None:
        lax.fori_loop(0, qi - 1, far_body, 0)
    elif w_tiles >= 2:
        lax.fori_loop(jnp.maximum(qi - w_tiles + 1, 0), qi - 1, far_body, 0)

        @pl.when(qi >= w_tiles)
        def _():
            tile(qi - w_tiles, edge_tab)

    l = l_ref[...]
    out = acc_ref[...] / l
    for j in range(group):
        o_ref[0, :, j * HEAD_DIM:(j + 1) * HEAD_DIM] = out[j * t:(j + 1) * t]
    if want_lse:
        lse = m_ref[...] + jnp.log(l)
        for j in range(group):
            lse_ref[0, :, j * HEAD_DIM:(j + 1) * HEAD_DIM] = lse[j * t:(j + 1) * t]


def _flash(proj, tabs, far, sel, *, n_kv, group, q_col, k_col, v_col, head0=0,
           window=None, dil=1, sel_blk=0, sel_per_head=False, want_lse=False):
    b, s, _ = proj.shape
    t = ATTN_TILE
    gw = group * HEAD_DIM
    n_tab = tabs.shape[1]
    w_tiles = None if window is None else window // t
    far_tab = 2 if dil > 1 else None
    edge_tab = n_tab - 1 if (w_tiles is not None and w_tiles >= 2) else None
    in_specs = [
        pl.BlockSpec((1, t, gw), lambda bi, h, qi: (bi, qi, q_col // group + h)),
        pl.BlockSpec((1, s, HEAD_DIM), lambda bi, h, qi: (bi, 0, k_col + h)),
        pl.BlockSpec((1, s, HEAD_DIM), lambda bi, h, qi: (bi, 0, v_col + h)),
        pl.BlockSpec((group, n_tab, t, t), lambda bi, h, qi: (head0 // group + h, 0, 0, 0)),
        pl.BlockSpec(memory_space=pltpu.SMEM),
    ]
    args = [proj, proj, proj, tabs, far]
    if sel_blk:
        if sel_per_head:
            in_specs.append(pl.BlockSpec((1, group, t, HEAD_DIM), lambda bi, h, qi: (bi, h, qi, 0)))
        else:
            in_specs.append(pl.BlockSpec((1, 1, t, HEAD_DIM), lambda bi, h, qi: (bi, h, qi, 0)))
        args.append(sel)
    out_spec = pl.BlockSpec((1, t, gw), lambda bi, h, qi: (bi, qi, h))
    out_sds = jax.ShapeDtypeStruct((b, s, n_kv * gw), F32)
    rows = group * t
    return pl.pallas_call(
        functools.partial(_flash_kernel, group=group, t=t, w_tiles=w_tiles, far_tab=far_tab, edge_tab=edge_tab,
                          sel_blk=sel_blk, sel_per_head=sel_per_head, want_lse=want_lse, head0=head0),
        grid=(b, n_kv, s // t),
        in_specs=in_specs,
        out_specs=[out_spec, out_spec] if want_lse else out_spec,
        out_shape=[out_sds, out_sds] if want_lse else out_sds,
        scratch_shapes=[pltpu.VMEM((s, HEAD_DIM), BF16), pltpu.VMEM((s, HEAD_DIM), BF16),
                        pltpu.VMEM((rows, HEAD_DIM), BF16), pltpu.VMEM((rows, t), F32),
                        pltpu.VMEM((rows, HEAD_DIM), F32), pltpu.VMEM((rows, HEAD_DIM), F32),
                        pltpu.VMEM((rows, HEAD_DIM), F32)],
        compiler_params=_params(3),
        name="flash",
    )(*args)


def _flash2_kernel(*refs, group, t, w_tiles, far_tab, edge_tab, sel_blk, sel_per_head, head0):
    it = iter(refs)
    q_ref, k_ref, v_ref, tab_ref, far_ref = (next(it) for _ in range(5))
    sel_ref = next(it) if sel_blk else None
    o_ref = next(it)
    kb_ref, vb_ref, qs_ref, s_ref, mx_ref, l_ref, acc_ref = (next(it) for _ in range(7))

    kvh = pl.program_id(1)
    qi = pl.program_id(2)
    rows = group * t

    @pl.when(qi == 0)
    def _():
        kb_ref[...] = k_ref[0].astype(BF16)
        vb_ref[...] = v_ref[0].astype(BF16)

    for j in range(group):
        qs_ref[j * t:(j + 1) * t, :] = q_ref[0, :, j * HEAD_DIM:(j + 1) * HEAD_DIM].astype(BF16)

    def scores(tiles, tab, first=False):
        per_head = [[] for _ in range(group)]
        for kj in tiles:
            start = pl.multiple_of(kj * t, t)
            k = kb_ref[pl.ds(start, t), :]
            sel_add = None
            if sel_blk and not sel_per_head:
                blk_of_key = (start + lax.broadcasted_iota(jnp.int32, (HEAD_DIM, t), 1)) // sel_blk
                expand = lax.broadcasted_iota(jnp.int32, (HEAD_DIM, t), 0) == blk_of_key
                expand = jnp.where(expand, 1.0, 0.0).astype(BF16)
                sel_add = _dot(sel_ref[0, 0].astype(BF16), expand)
            for j in range(group):
                hs = slice(j * t, (j + 1) * t)
                sj = _dot(qs_ref[hs, :], k, _NT) * (SOFTMAX_SCALE * LOG2E)
                sj = sj + (far_ref[head0 + kvh * group + j] if tab is None else tab_ref[j, tab])
                if sel_blk and sel_per_head:
                    assert sel_blk == t
                    lane = lax.broadcasted_iota(jnp.int32, (t, HEAD_DIM), 1)
                    sj = sj + jnp.sum(jnp.where(lane == kj, sel_ref[0, j], 0.0), axis=1, keepdims=True)
                elif sel_blk:
                    sj = sj + sel_add
                s_ref[kj, hs, :] = sj
                per_head[j].append(jnp.maximum(sj[:, :HEAD_DIM], sj[:, HEAD_DIM:]))
        for j in range(group):
            hs = slice(j * t, (j + 1) * t)
            mj = functools.reduce(jnp.maximum, per_head[j])
            mx_ref[hs, :] = mj if first else jnp.maximum(mx_ref[hs, :], mj)

    def accumulate(tiles, tab, first=False):
        for j in range(group):
            hs = slice(j * t, (j + 1) * t)
            m = mx_ref[hs, :]
            m2 = jnp.concatenate([m, m], axis=1)
            p_lane, pv = None, None
            for kj in tiles:
                v = vb_ref[pl.ds(pl.multiple_of(kj * t, t), t), :]
                p = jnp.exp2(s_ref[kj, hs, :] - m2)
                pl_ = p[:, :HEAD_DIM] + p[:, HEAD_DIM:]
                pv_ = _dot(p.astype(BF16), v)
                p_lane = pl_ if p_lane is None else p_lane + pl_
                pv = pv_ if pv is None else pv + pv_
            l_ref[hs, :] = p_lane if first else l_ref[hs, :] + p_lane
            acc_ref[hs, :] = pv if first else acc_ref[hs, :] + pv

    def sweep(visit):
        visit([qi], 0, first=True)

        @pl.when(qi >= 1)
        def _():
            visit([qi - 1], 1)

        def far_tiles(lo, hi):
            def pair_body(i, carry):
                visit([lo + 2 * i, lo + 2 * i + 1], far_tab)
                return carry

            n_far = jnp.maximum(hi - lo, 0)
            lax.fori_loop(0, n_far // 2, pair_body, 0)

            @pl.when(n_far % 2 == 1)
            def _():
                visit([hi - 1], far_tab)

        if w_tiles is None:
            far_tiles(0, qi - 1)
        elif w_tiles >= 2:
            far_tiles(jnp.maximum(qi - w_tiles + 1, 0), qi - 1)

            @pl.when(qi >= w_tiles)
            def _():
                visit([qi - w_tiles], edge_tab)

    sweep(scores)
    mx_ref[...] = jnp.broadcast_to(jnp.max(mx_ref[...], axis=1, keepdims=True), (rows, HEAD_DIM))
    sweep(accumulate)

    out = acc_ref[...] / jnp.sum(l_ref[...], axis=1, keepdims=True)
    for j in range(group):
        o_ref[0, :, j * HEAD_DIM:(j + 1) * HEAD_DIM] = out[j * t:(j + 1) * t]


def _flash2(proj, tabs, far, sel, *, n_kv, group, q_col, k_col, v_col, head0=0,
            window=None, sel_blk=0, sel_per_head=False):
    b, s, _ = proj.shape
    t = ATTN_TILE
    assert t == 2 * HEAD_DIM
    gw = group * HEAD_DIM
    n_tab = tabs.shape[1]
    w_tiles = None if window is None else window // t
    edge_tab = n_tab - 1 if (w_tiles is not None and w_tiles >= 2) else None
    tabs, far = tabs * LOG2E, far * LOG2E
    in_specs = [
        pl.BlockSpec((1, t, gw), lambda bi, h, qi: (bi, qi, q_col // group + h)),
        pl.BlockSpec((1, s, HEAD_DIM), lambda bi, h, qi: (bi, 0, k_col + h)),
        pl.BlockSpec((1, s, HEAD_DIM), lambda bi, h, qi: (bi, 0, v_col + h)),
        pl.BlockSpec((group, n_tab, t, t), lambda bi, h, qi: (head0 // group + h, 0, 0, 0)),
        pl.BlockSpec(memory_space=pltpu.SMEM),
    ]
    args = [proj, proj, proj, tabs, far]
    if sel_blk:
        if sel_per_head:
            in_specs.append(pl.BlockSpec((1, group, t, HEAD_DIM), lambda bi, h, qi: (bi, h, qi, 0)))
        else:
            in_specs.append(pl.BlockSpec((1, 1, t, HEAD_DIM), lambda bi, h, qi: (bi, h, qi, 0)))
        args.append(sel)
    rows = group * t
    return pl.pallas_call(
        functools.partial(_flash2_kernel, group=group, t=t, w_tiles=w_tiles, far_tab=None, edge_tab=edge_tab,
                          sel_blk=sel_blk, sel_per_head=sel_per_head, head0=head0),
        grid=(b, n_kv, s // t),
        in_specs=in_specs,
        out_specs=pl.BlockSpec((1, t, gw), lambda bi, h, qi: (bi, qi, h)),
        out_shape=jax.ShapeDtypeStruct((b, s, n_kv * gw), F32),
        scratch_shapes=[pltpu.VMEM((s, HEAD_DIM), BF16), pltpu.VMEM((s, HEAD_DIM), BF16),
                        pltpu.VMEM((rows, HEAD_DIM), BF16), pltpu.VMEM((s // t, rows, t), F32),
                        pltpu.VMEM((rows, HEAD_DIM), F32), pltpu.VMEM((rows, HEAD_DIM), F32),
                        pltpu.VMEM((rows, HEAD_DIM), F32)],
        compiler_params=_params(3),
        name="flash2",
    )(*args)


def _dilated_kernel(q_ref, k_ref, v_ref, tab_ref, o_ref, lse_ref, m_ref, l_ref, acc_ref, *, t, dil, n_q):
    def rows(r, tile_idx):
        return pl.ds(r + tile_idx * (t * dil), t, stride=dil)

    def lane_pair(op, x):
        return op(x[:, :HEAD_DIM], x[:, HEAD_DIM:])

    def one(idx):
        r = idx // n_q
        qi = idx % n_q
        q = q_ref[0, rows(r, qi), :].astype(BF16)
        tiles = [(qi, 0, None)]
        if n_q > 1:
            tiles.append((jnp.maximum(qi - 1, 0), 1, jnp.where(qi >= 1, 0.0, NEG_INF)))
        scores, values = [], []
        for kj, tab, extra in tiles:
            k = k_ref[0, rows(r, kj), :].astype(BF16)
            values.append(v_ref[0, rows(r, kj), :].astype(BF16))
            s = _dot(q, k, _NT) * SOFTMAX_SCALE + tab_ref[0, tab]
            scores.append(s if extra is None else s + extra)
        m_lane = lane_pair(jnp.maximum, scores[0])
        for s in scores[1:]:
            m_lane = jnp.maximum(m_lane, lane_pair(jnp.maximum, s))
        m = jnp.max(m_lane, axis=1, keepdims=True)
        l_lane, acc = None, None
        for s, v in zip(scores, values):
            p = jnp.exp(s - m)
            pl_ = lane_pair(jnp.add, p)
            pv = _dot(p.astype(BF16), v)
            l_lane = pl_ if l_lane is None else l_lane + pl_
            acc = pv if acc is None else acc + pv
        l = jnp.sum(l_lane, axis=1, keepdims=True)
        o_ref[0, rows(r, qi), :] = acc / l
        lse_ref[0, rows(r, qi), :] = jnp.broadcast_to(m + jnp.log(l), (t, HEAD_DIM))

    n_it = dil * n_q
    chains = 4 if n_it % 4 == 0 else 1

    def body(it, carry):
        for u in range(chains):
            one(it * chains + u)
        return carry

    lax.fori_loop(0, n_it // chains, body, 0)


def _dilated_attention(proj, tabs, *, n_heads, q_col, k_col, v_col, dil):
    b, s, _ = proj.shape
    t = ATTN_TILE
    assert t == 2 * HEAD_DIM and (s // dil) % t == 0
    spec = lambda col: pl.BlockSpec((1, s, HEAD_DIM), lambda bi, h: (bi, 0, col + h))
    out_sds = jax.ShapeDtypeStruct((b, s, n_heads * HEAD_DIM), F32)
    return pl.pallas_call(
        functools.partial(_dilated_kernel, t=t, dil=dil, n_q=s // dil // t),
        grid=(b, n_heads),
        in_specs=[spec(q_col), spec(k_col), spec(v_col),
                  pl.BlockSpec((1, 2, t, t), lambda bi, h: (h, 0, 0, 0))],
        out_specs=[spec(0), spec(0)],
        out_shape=[out_sds, out_sds],
        scratch_shapes=[pltpu.VMEM((t, HEAD_DIM), F32)] * 3,
        compiler_params=_params(2),
        name="dilated_attention",
    )(proj, proj, proj, tabs)


def _moba_select_kernel(q_ref, k_ref, sel_ref, kmean_ref, *, group, t, n_blk):
    qi = pl.program_id(2)

    @pl.when(qi == 0)
    def _():
        kmean_ref[...] = jnp.zeros_like(kmean_ref)
        s = k_ref.shape[1]
        km = jnp.sum(k_ref[0].reshape(n_blk, s // n_blk, HEAD_DIM), axis=1) * (1.0 / B_BLOCK)
        kmean_ref[0:n_blk, :] = km

    lane = lax.broadcasted_iota(jnp.int32, (t, HEAD_DIM), 1)
    pos = qi * t + lax.broadcasted_iota(jnp.int32, (t, HEAD_DIM), 0)
    own = pos // B_BLOCK
    past = lane < own
    kmean = kmean_ref[...]
    for j in range(group):
        q = q_ref[0, :, j * HEAD_DIM:(j + 1) * HEAD_DIM]
        gate = _dot_f32_both(q, kmean, _NT)
        gate = jnp.where(past, gate, NEG_INF)
        rank = _rank_lanes(gate, n_blk)
        sel = (past & (rank < B_TOPK)) | (lane == own)
        sel_ref[0, j] = jnp.where(sel, 0.0, NEG_INF)


def _moba_select(proj, *, n_kv, group, k_col):
    b, s, _ = proj.shape
    t = ATTN_TILE
    n_blk = s // B_BLOCK
    return pl.pallas_call(
        functools.partial(_moba_select_kernel, group=group, t=t, n_blk=n_blk),
        grid=(b, n_kv, s // t),
        in_specs=[pl.BlockSpec((1, t, group * HEAD_DIM), lambda bi, h, qi: (bi, qi, h)),
                  pl.BlockSpec((1, s, HEAD_DIM), lambda bi, h, qi: (bi, 0, k_col + h))],
        out_specs=pl.BlockSpec((1, group, t, HEAD_DIM), lambda bi, h, qi: (bi, h, qi, 0)),
        out_shape=jax.ShapeDtypeStruct((b, n_kv * group, s, HEAD_DIM), F32),
        scratch_shapes=[pltpu.VMEM((HEAD_DIM, HEAD_DIM), F32)],
        compiler_params=_params(3),
        name="moba_select",
    )(proj, proj)


def _cmp1_prompt_kernel(x_ref, w_ref, y_ref, *, n_seg):
    x2 = jnp.concatenate([x_ref[pl.ds(s, n_seg, stride=C_CMP_STRIDE), :] for s in range(C_CMP_STRIDE)], axis=1)
    y_ref[...] = _dot(x2.astype(BF16), w_ref[...])


def _cmp1_prompt(proj, w1cat, *, col0):
    b, s, _ = proj.shape
    n_seg = s // C_CMP_STRIDE
    return pl.pallas_call(
        functools.partial(_cmp1_prompt_kernel, n_seg=n_seg),
        grid=(b, 2, C_KV_HEADS),
        in_specs=[pl.BlockSpec((None, s, HEAD_DIM), lambda bi, c, g: (bi, 0, col0 + c * C_KV_HEADS + g)),
                  pl.BlockSpec((None, C_CMP_STRIDE * HEAD_DIM, 2 * HEAD_DIM), lambda bi, c, g: (c, 0, 0))],
        out_specs=pl.BlockSpec((None, n_seg, 2 * HEAD_DIM), lambda bi, c, g: ((bi * 2 + c) * C_KV_HEADS + g, 0, 0)),
        out_shape=jax.ShapeDtypeStruct((b * 2 * C_KV_HEADS, n_seg, 2 * HEAD_DIM), F32),
        compiler_params=_params(3),
        name="cmp1_prompt",
    )(proj, w1cat)


def _cmp2_kernel(y_ref, pe_ref, w1_ref, w2_ref, o_ref, *, n_seg):
    y = y_ref[0]
    first = y[:, :HEAD_DIM]
    second = pltpu.roll(y[:, HEAD_DIM:], n_seg - 1, 0)
    pe_term = _dot(pe_ref[0].astype(BF16), w1_ref[0].astype(BF16))[0:1]
    pre = first + second + pe_term
    tok = _dot(jnp.maximum(pre, 0.0).astype(BF16), w2_ref[0].astype(BF16))
    row = lax.broadcasted_iota(jnp.int32, tok.shape, 0)
    o_ref[0] = jnp.where(row < n_seg - 1, tok, 0.0)


def _cmp2(y, pe8, w1, w2):
    r, n_seg, _ = y.shape
    comp = lambda i: ((i // C_KV_HEADS) % 2, 0, 0)
    return pl.pallas_call(
        functools.partial(_cmp2_kernel, n_seg=n_seg),
        grid=(r,),
        in_specs=[pl.BlockSpec((1, n_seg, 2 * HEAD_DIM), lambda i: (i, 0, 0)),
                  pl.BlockSpec((1, 8, C_CMP_BLOCK * HEAD_DIM), comp),
                  pl.BlockSpec((1, C_CMP_BLOCK * HEAD_DIM, HEAD_DIM), comp),
                  pl.BlockSpec((1, HEAD_DIM, HEAD_DIM), comp)],
        out_specs=pl.BlockSpec((1, n_seg, HEAD_DIM), lambda i: (i, 0, 0)),
        out_shape=jax.ShapeDtypeStruct((r, n_seg, HEAD_DIM), F32),
        compiler_params=_params(1),
        name="cmp2",
    )(y, pe8, w1, w2)


CMP_BAND = 8


def _nsa_cmp_select_kernel(q_ref, kc_ref, vc_ref, bias_ref, far_ref, map_ref, oc_ref, sel_ref, *,
                           group, t, n_cmp, n_sel):
    kvh = pl.program_id(1)
    qi = pl.program_id(2)
    n_pad = kc_ref.shape[1]
    pos = qi * t + lax.broadcasted_iota(jnp.int32, (t, n_pad), 0)
    tok = lax.broadcasted_iota(jnp.int32, (t, n_pad), 1)
    valid = (pos - (tok * C_CMP_STRIDE + (C_CMP_BLOCK - 1)) >= 0) & (tok < n_cmp)
    band0 = qi * (t // C_CMP_STRIDE) - CMP_BAND
    shift = jnp.where(band0 < 0, band0 + n_pad, band0)
    is_far = tok < band0
    kc = kc_ref[0].astype(BF16)
    vc = vc_ref[0].astype(BF16)
    p_sum = jnp.zeros((t, n_pad), F32)
    for j in range(group):
        q = q_ref[0, :, j * HEAD_DIM:(j + 1) * HEAD_DIM].astype(BF16)
        bias = jnp.where(is_far, far_ref[kvh * group + j], pltpu.roll(bias_ref[j], shift, 1))
        s = _dot(q, kc, _NT) * SOFTMAX_SCALE + bias
        s = jnp.where(valid, s, NEG_INF)
        m = jnp.max(s, axis=1, keepdims=True)
        e = jnp.where(valid, jnp.exp(s - m), 0.0)
        p = e / jnp.maximum(jnp.sum(e, axis=1, keepdims=True), 1e-30)
        oc_ref[0, :, j * HEAD_DIM:(j + 1) * HEAD_DIM] = _dot(p.astype(BF16), vc)
        p_sum = p_sum + p
    imp = _dot_f32_lhs(p_sum, map_ref[...])
    blk = lax.broadcasted_iota(jnp.int32, (t, HEAD_DIM), 1)
    cur = (qi * t + lax.broadcasted_iota(jnp.int32, (t, HEAD_DIM), 0)) // C_SLC_BLOCK
    avail = (blk <= cur) & (blk < n_sel)
    forced = (blk == 0) | (blk == cur) | (blk == cur - 1)
    score = jnp.where(forced, jnp.inf, imp)
    score = jnp.where(avail, score, -jnp.inf)
    rank = _rank_lanes(score, n_sel)
    sel = avail & (rank < min(C_SLC_TOPK, n_sel))
    sel_ref[0, 0] = jnp.where(sel, 0.0, NEG_INF)


def _cmp_bias_table(rel_bias, s, n_seg):
    assert s == n_seg * C_CMP_STRIDE
    h = rel_bias.shape[1]
    k = np.arange(2 * n_seg + 1)
    e = np.where(k <= n_seg, -k, 2 * n_seg + 1 - k)
    dist = C_CMP_STRIDE * e[None, :] + np.arange(C_CMP_STRIDE)[:, None] - (C_CMP_BLOCK - 1)
    w = jnp.transpose(rel_bias[jnp.asarray(_bucket_np(dist))], (2, 0, 1))
    tab = _toeplitz(w.reshape(h * C_CMP_STRIDE, -1), n_seg).reshape(h, C_CMP_STRIDE, n_seg, n_seg)
    return jnp.transpose(tab, (0, 2, 1, 3)).reshape(h, s, n_seg)


def _cmp_map_np(n_rows, n_cmp, n_cols):
    ratio = C_SLC_BLOCK // C_CMP_STRIDE
    m = np.zeros((n_rows, n_cols), np.float32)
    for i in range(n_cmp):
        for r in range(C_CMP_BLOCK // C_CMP_STRIDE):
            m[i, (i + r) // ratio] += 1.0
    return m


def _nsa_cmp_select(proj, kvc, rel_bias, far, *, n_kv, group):
    b, s, _ = proj.shape
    t = ATTN_TILE
    n_seg = kvc.shape[1]
    n_cmp = n_seg - 1
    n_sel = -(-s // C_SLC_BLOCK)
    cmap = jnp.asarray(_cmp_map_np(n_seg, n_cmp, HEAD_DIM), BF16)
    gw = group * HEAD_DIM
    n_band = CMP_BAND + 2 * (t // C_CMP_STRIDE) - (C_CMP_BLOCK - 1) // C_CMP_STRIDE
    assert n_band <= n_seg and C_CMP_STRIDE * (CMP_BAND + 1) - (C_CMP_BLOCK - 1) >= 113
    dist = np.arange(t)[:, None] - C_CMP_STRIDE * (np.arange(n_band)[None, :] - CMP_BAND) - (C_CMP_BLOCK - 1)
    bias_c = jnp.pad(_bias_by_dist(rel_bias, np.maximum(dist, 0)), ((0, 0), (0, 0), (0, n_seg - n_band)))
    return pl.pallas_call(
        functools.partial(_nsa_cmp_select_kernel, group=group, t=t, n_cmp=n_cmp, n_sel=n_sel),
        grid=(b, n_kv, s // t),
        in_specs=[pl.BlockSpec((1, t, gw), lambda bi, h, qi: (bi, qi, h)),
                  pl.BlockSpec((1, n_seg, HEAD_DIM), lambda bi, h, qi: (bi * 8 + h, 0, 0)),
                  pl.BlockSpec((1, n_seg, HEAD_DIM), lambda bi, h, qi: (bi * 8 + 4 + h, 0, 0)),
                  pl.BlockSpec((group, t, n_seg), lambda bi, h, qi: (h, 0, 0)),
                  pl.BlockSpec(memory_space=pltpu.SMEM),
                  pl.BlockSpec((n_seg, HEAD_DIM), lambda bi, h, qi: (0, 0))],
        out_specs=[pl.BlockSpec((1, t, gw), lambda bi, h, qi: (bi, qi, h)),
                   pl.BlockSpec((1, 1, t, HEAD_DIM), lambda bi, h, qi: (bi, h, qi, 0))],
        out_shape=[jax.ShapeDtypeStruct((b, s, n_kv * gw), F32),
                   jax.ShapeDtypeStruct((b, n_kv, s, HEAD_DIM), F32)],
        compiler_params=_params(3),
        name="nsa_cmp_select",
    )(proj, kvc, kvc, bias_c, far, cmap)


def _row_tile(m, pref):
    return pref if m % pref == 0 else m


def _mixer_a_prompt(x, xs, b, g_mix, li, w_in, w_out, slot, tabs, far):
    m = x.shape[0]
    s = m // b
    n_in = w_in.shape[2]
    proj, proj_s = _norm_matmul(x, g_mix, li, w_in, slot, n_in, _row_tile(m, PROMPT_ROW_TILE), 512, xs=xs)
    proj3 = proj.reshape(b, s, n_in)
    outs, lses = [], []
    for g, (window, dil) in enumerate(A_PATTERNS):
        c0 = g * 3 * A_HEADS
        o, lse = _dilated_attention(proj3, tabs[g], n_heads=A_HEADS, q_col=c0, k_col=c0 + A_HEADS,
                                    v_col=c0 + 2 * A_HEADS, dil=dil)
        outs.append(o.reshape(m, -1))
        lses.append(lse.reshape(m, -1))
    x = _outproj(outs, lses, w_out, slot, x, "lse3", _row_tile(m, 512), 512)
    return x, proj3, proj_s


def _mixer_b_prompt(x, xs, b, g_mix, li, w_in, w_out, slot, tabs, far):
    m = x.shape[0]
    s = m // b
    n_in = w_in.shape[2]
    proj, proj_s = _norm_matmul(x, g_mix, li, w_in, slot, n_in, _row_tile(m, PROMPT_ROW_TILE), 512, xs=xs)
    proj3 = proj.reshape(b, s, n_in)
    sel = _moba_select(proj3, n_kv=B_KV_HEADS, group=GQA_GROUP, k_col=B_HEADS)
    o = _flash2(proj3, tabs, far, sel, n_kv=B_KV_HEADS, group=GQA_GROUP, q_col=0, k_col=B_HEADS,
               v_col=B_HEADS + B_KV_HEADS, sel_blk=B_BLOCK, sel_per_head=True)
    x = _outproj([o.reshape(m, -1)], [], w_out, slot, x, "plain", _row_tile(m, 512), 512)
    return x, proj3, proj_s


def _cmp_weights(cmp_pe, cmp_w1, cmp_w2, slot):
    half = C_CMP_STRIDE * HEAD_DIM
    w1 = cmp_w1[slot]
    w1cat = jnp.concatenate([w1[:, :half], w1[:, half:]], axis=2).astype(BF16)
    pe8 = jnp.broadcast_to(cmp_pe[slot].reshape(2, 1, C_CMP_BLOCK * HEAD_DIM), (2, 8, C_CMP_BLOCK * HEAD_DIM))
    return w1cat, pe8, w1, cmp_w2[slot]


def _mixer_c_prompt(x, xs, b, g_mix, li, w_in, w_gate, w_out, slot, cmp_w, tabs, tabs_win, far, rel_bias):
    m = x.shape[0]
    s = m // b
    n_main = (C_HEADS + 6 * C_KV_HEADS) * HEAD_DIM
    proj, proj_s = _norm_matmul(x, g_mix, li, w_in, slot, n_main, _row_tile(m, PROMPT_ROW_TILE), 512, xs=xs)
    gate, gate_s = _norm_matmul(x, g_mix, li, w_gate, 0, HEAD_DIM, _row_tile(m, 512), HEAD_DIM, xs=xs)
    proj3 = proj.reshape(b, s, n_main)
    w1cat, pe8, w1, w2 = cmp_w
    y = _cmp1_prompt(proj3, w1cat, col0=C_HEADS)
    kvc = _cmp2(y, pe8, w1, w2)
    n_seg = kvc.shape[1]
    o_cmp, sel = _nsa_cmp_select(proj3, kvc, rel_bias, far, n_kv=C_KV_HEADS, group=GQA_GROUP)
    kv0 = C_HEADS
    o_slc = _flash2(proj3, tabs, far, sel, n_kv=C_KV_HEADS, group=GQA_GROUP, q_col=0, k_col=kv0 + 2 * C_KV_HEADS,
                   v_col=kv0 + 3 * C_KV_HEADS, sel_blk=C_SLC_BLOCK)
    o_win = _flash2(proj3, tabs_win, far, None, n_kv=C_KV_HEADS, group=GQA_GROUP, q_col=0, k_col=kv0 + 4 * C_KV_HEADS,
                   v_col=kv0 + 5 * C_KV_HEADS, window=C_WINDOW)
    x = _outproj([o_cmp.reshape(m, -1), o_slc.reshape(m, -1), o_win.reshape(m, -1)], [gate], w_out, slot, x,
                 "gate3", _row_tile(m, 512), 512)
    return x, proj3, proj_s, gate_s


def _a_decode_kernel(qkv_ref, buf0_ref, buf1_ref, buf2_ref, bias_ref, o_ref):
    outs, lses = [], []
    for g, buf in enumerate((buf0_ref, buf1_ref, buf2_ref)):
        q, k_new, v_new = qkv_ref[0, g, 0], qkv_ref[0, g, 1], qkv_ref[0, g, 2]
        kb, vb = buf[:, 0], buf[:, 1]
        n = kb.shape[0]
        s = jnp.sum(q[None] * kb, axis=-1, keepdims=True) * SOFTMAX_SCALE + bias_ref[g, 0:n]
        s_new = jnp.sum(q * k_new, axis=-1, keepdims=True) * SOFTMAX_SCALE + bias_ref[g, n]
        m = jnp.maximum(jnp.max(s, axis=0), s_new)
        e = jnp.exp(s - m[None])
        e_new = jnp.exp(s_new - m)
        l = jnp.sum(e, axis=0) + e_new
        outs.append((jnp.sum(e * vb, axis=0) + e_new * v_new) / l)
        lses.append(m + jnp.log(l))
    mx = jnp.maximum(jnp.maximum(lses[0], lses[1]), lses[2])
    w = [jnp.exp(l - mx) for l in lses]
    o_ref[0] = (w[0] * outs[0] + w[1] * outs[1] + w[2] * outs[2]) / (w[0] + w[1] + w[2])


def _a_decode(qkv, bufs, slot, bias):
    n_dec = qkv.shape[0]
    in_specs = [pl.BlockSpec((1, 3, 3, A_HEADS, HEAD_DIM), lambda i: (i, 0, 0, 0, 0))]
    views = []
    for (window, dil), buf in zip(A_PATTERNS, bufs):
        n = window // dil
        assert buf.shape[2] == window, "window buffers must hold exactly `window` past rows"
        views.append(buf.reshape(buf.shape[0], n_dec, n, dil, 2, A_HEADS, HEAD_DIM))
        in_specs.append(pl.BlockSpec((None, None, n, None, 2, A_HEADS, HEAD_DIM),
                                     lambda i: (slot, i, 0, 0, 0, 0, 0)))
    in_specs.append(pl.BlockSpec(bias.shape, lambda i: (0, 0, 0, 0)))
    return pl.pallas_call(
        _a_decode_kernel,
        grid=(n_dec,),
        in_specs=in_specs,
        out_specs=pl.BlockSpec((1, A_HEADS, HEAD_DIM), lambda i: (i, 0, 0)),
        out_shape=jax.ShapeDtypeStruct((n_dec, A_HEADS, HEAD_DIM), F32),
        compiler_params=_params(1),
        name="a_decode",
    )(qkv, *views, bias)


DECODE_CHUNK = 2048


def _head_rows_store(o_ref, o_all, l_row):
    l_col = jnp.broadcast_to(l_row, (HEAD_DIM, HEAD_DIM)).T
    for g in range(B_KV_HEADS):
        rows = slice(g * GQA_GROUP, (g + 1) * GQA_GROUP)
        o_ref[0, rows, :] = o_all[rows, g * HEAD_DIM:(g + 1) * HEAD_DIM] / l_col[rows, :]


def _paged_decode_kernel(pt_ref, *refs, n_pg, n_sub, k_sub, v_sub, blk, topk, n_pages):
    page_refs = refs[:n_pg]
    qbd_ref, new_ref, bias_ref = refs[n_pg:n_pg + 3]
    rest = refs[n_pg + 3:]
    if topk is None:
        sel_ref, o_ref, st_ref, vs_ref, sel_s = rest
    else:
        o_ref, st_ref, vs_ref, sel_s = rest
    j = pl.program_id(1)
    past = n_pages * PAGE_SIZE
    n_blk = past // blk
    ch = DECODE_CHUNK

    if topk is not None:
        @pl.when(j == 0)
        def _():
            sel_s[...] = jnp.zeros_like(sel_s)

    for p in range(n_pg):
        page = j * n_pg + p
        row0 = pl.multiple_of(page * PAGE_SIZE, PAGE_SIZE)
        s_acc = jnp.zeros((PAGE_SIZE, HEAD_DIM), F32)
        for g in range(B_KV_HEADS):
            kg = page_refs[p][pl.ds(k_sub + g, PAGE_SIZE, stride=n_sub), :].astype(BF16)
            s_acc = s_acc + _dot(kg, qbd_ref[0, g])
            vg = page_refs[p][pl.ds(v_sub + g, PAGE_SIZE, stride=n_sub), :].astype(BF16)
            vs_ref[pl.ds(row0, PAGE_SIZE), g * HEAD_DIM:(g + 1) * HEAD_DIM] = vg
        st_ref[pl.ds(row0, PAGE_SIZE), :] = s_acc
        if topk is not None:
            b_idx = page // (blk // PAGE_SIZE)
            sel_s[pl.ds(b_idx, 1), :] = sel_s[pl.ds(b_idx, 1), :] + jnp.sum(s_acc, axis=0, keepdims=True)

    @pl.when(j == pl.num_programs(1) - 1)
    def _():
        s_new = jnp.zeros((8, HEAD_DIM), F32)
        for g in range(B_KV_HEADS):
            k_new = jnp.broadcast_to(new_ref[0, g:g + 1, :], (8, HEAD_DIM)).astype(BF16)
            s_new = s_new + _dot(k_new, qbd_ref[0, g])
            v_new = jnp.broadcast_to(new_ref[0, B_KV_HEADS + g:B_KV_HEADS + g + 1, :], (8, HEAD_DIM))
            vs_ref[past:past + 8, g * HEAD_DIM:(g + 1) * HEAD_DIM] = v_new.astype(BF16)
        l_new = s_new[0:1] * SOFTMAX_SCALE + bias_ref[ch + 1:ch + 2, :]

        if topk is not None:
            gate = sel_s[...] * (1.0 / blk)
            rank = _rank_rows(gate, n_blk)
            sel_s[...] = jnp.where(rank < topk, 1.0, 0.0)
        else:
            sel_s[...] = sel_ref[0, 0:n_blk, :]

        def masked_logits(c, bias):
            start = pl.multiple_of(c * ch, ch)
            lg = st_ref[pl.ds(start, ch), :] * SOFTMAX_SCALE + bias
            b0 = pl.multiple_of(c * (ch // blk), 8)
            sl = sel_s[pl.ds(b0, ch // blk), :]
            mk = jnp.broadcast_to(sl[:, None, :], (ch // blk, blk, HEAD_DIM)).reshape(ch, HEAD_DIM) > 0.5
            return jnp.where(mk, lg, NEG_INF), mk, start

        n_ch = past // ch
        far = bias_ref[ch:ch + 1, :]
        tail = bias_ref[0:ch, :]

        def max_body(c, m):
            lg, _, _ = masked_logits(c, far)
            return jnp.maximum(m, jnp.max(lg, axis=0, keepdims=True))

        m = lax.fori_loop(0, n_ch - 1, max_body, l_new)
        lg_t, mk_t, start_t = masked_logits(n_ch - 1, tail)
        m = jnp.maximum(m, jnp.max(lg_t, axis=0, keepdims=True))

        def acc_chunk(lg, mk, start, carry):
            l_sum, o_all = carry
            e = jnp.where(mk, jnp.exp(lg - m), 0.0)
            o_all = o_all + _dot(e.astype(BF16), vs_ref[pl.ds(start, ch), :], _TN)
            return l_sum + jnp.sum(e, axis=0, keepdims=True), o_all

        def acc_body(c, carry):
            lg, mk, start = masked_logits(c, far)
            return acc_chunk(lg, mk, start, carry)

        e_new = jnp.exp(l_new - m)
        row8 = lax.broadcasted_iota(jnp.int32, (8, HEAD_DIM), 0)
        e_new8 = jnp.where(row8 == 0, jnp.broadcast_to(e_new, (8, HEAD_DIM)), 0.0)
        o0 = _dot(e_new8.astype(BF16), vs_ref[past:past + 8, :], _TN)
        carry = lax.fori_loop(0, n_ch - 1, acc_body, (e_new, o0))
        l_sum, o_all = acc_chunk(lg_t, mk_t, start_t, carry)
        _head_rows_store(o_ref, o_all, l_sum)


def _paged_decode(page_table, pool2d, qbd, new_kv, bias, sel, *, n_pg, n_sub, k_sub, v_sub, blk, topk):
    n_dec, n_pages = page_table.shape
    past = n_pages * PAGE_SIZE
    assert past % DECODE_CHUNK == 0 and n_pages % n_pg == 0 and past % blk == 0
    n_blk = past // blk
    page_rows = PAGE_SIZE * n_sub
    in_specs = [pl.BlockSpec((page_rows, HEAD_DIM), functools.partial(lambda n, j, pt, p: (pt[n, j * n_pg + p], 0), p=p))
                for p in range(n_pg)]
    in_specs += [pl.BlockSpec((1, B_KV_HEADS, HEAD_DIM, HEAD_DIM), lambda n, j, pt: (n, 0, 0, 0)),
                 pl.BlockSpec((1, 2 * B_KV_HEADS, HEAD_DIM), lambda n, j, pt: (n, 0, 0)),
                 pl.BlockSpec(bias.shape, lambda n, j, pt: (0, 0))]
    args = [pool2d] * n_pg + [qbd, new_kv, bias]
    if topk is None:
        in_specs.append(pl.BlockSpec((1,) + sel.shape[1:], lambda n, j, pt: (n, 0, 0)))
        args.append(sel)
    return pl.pallas_call(
        functools.partial(_paged_decode_kernel, n_pg=n_pg, n_sub=n_sub, k_sub=k_sub, v_sub=v_sub, blk=blk,
                          topk=topk, n_pages=n_pages),
        grid_spec=pltpu.PrefetchScalarGridSpec(
            num_scalar_prefetch=1,
            grid=(n_dec, n_pages // n_pg),
            in_specs=in_specs,
            out_specs=pl.BlockSpec((1, B_HEADS, HEAD_DIM), lambda n, j, pt: (n, 0, 0)),
            scratch_shapes=[pltpu.VMEM((past + 8, HEAD_DIM), F32),
                            pltpu.VMEM((past + 8, B_KV_HEADS * HEAD_DIM), BF16),
                            pltpu.VMEM((n_blk, HEAD_DIM), F32)],
        ),
        out_shape=jax.ShapeDtypeStruct((n_dec, B_HEADS, HEAD_DIM), F32),
        compiler_params=_params(2),
        name="paged_decode",
    )(page_table, *args)


def _block_diag_q(q):
    n, h, d = q.shape
    qt = jnp.pad(jnp.swapaxes(q, 1, 2), ((0, 0), (0, 0), (0, HEAD_DIM - h)))
    grp = (np.arange(HEAD_DIM)[None, :] // GQA_GROUP == np.arange(B_KV_HEADS)[:, None]) & (np.arange(HEAD_DIM)[None, :] < h)
    return jnp.where(jnp.asarray(grp)[None, :, None, :], qt[:, None], 0.0).astype(BF16)


def _decode_bias(rel_bias, past, n_tail):
    dist = np.concatenate([n_tail - np.arange(n_tail), [MAX_DISTANCE * 8, 0], np.zeros(6, np.int64)])
    tab = rel_bias[jnp.asarray(_bucket_np(dist))]
    return jnp.pad(tab, ((0, 0), (0, HEAD_DIM - tab.shape[1])))


def _cmp1_paged_kernel(pt_ref, *refs, n_pg):
    page_refs = refs[:n_pg]
    w_ref, y_ref, ys_ref = refs[n_pg:]
    segs_per_page = PAGE_SIZE // C_CMP_STRIDE
    rows = segs_per_page * 8
    x2 = jnp.concatenate(
        [jnp.concatenate([ref[:, s, :, :].reshape(rows, HEAD_DIM) for s in range(C_CMP_STRIDE)], axis=1)
         for ref in page_refs], axis=0)
    y4 = _dot(x2.astype(BF16), w_ref[...])
    is_v = (lax.broadcasted_iota(jnp.int32, (x2.shape[0], 2 * HEAD_DIM), 0) % 8) >= C_KV_HEADS
    y = jnp.where(is_v, y4[:, 2 * HEAD_DIM:], y4[:, :2 * HEAD_DIM])
    ys_ref[0] = y[:, :HEAD_DIM]
    ys_ref[1] = y[:, HEAD_DIM:]
    for sub in range(8):
        y_ref[sub] = jnp.concatenate(
            [ys_ref[half, pl.ds(sub, n_pg * segs_per_page, stride=8), :] for half in range(2)], axis=1)


def _cmp1_paged(page_table, pool4d, w1cat4, *, sub_blk, n_pg):
    n_dec, n_pages = page_table.shape
    segs_per_page = PAGE_SIZE // C_CMP_STRIDE
    n_seg = n_pages * segs_per_page
    in_specs = [pl.BlockSpec((segs_per_page, C_CMP_STRIDE, 8, HEAD_DIM),
                             functools.partial(lambda n, j, pt, p: (pt[n, j * n_pg + p], 0, sub_blk, 0), p=p))
                for p in range(n_pg)]
    in_specs.append(pl.BlockSpec(w1cat4.shape, lambda n, j, pt: (0, 0)))
    return pl.pallas_call(
        functools.partial(_cmp1_paged_kernel, n_pg=n_pg),
        grid_spec=pltpu.PrefetchScalarGridSpec(
            num_scalar_prefetch=1,
            grid=(n_dec, n_pages // n_pg),
            in_specs=in_specs,
            out_specs=pl.BlockSpec((8, n_pg * segs_per_page, 2 * HEAD_DIM), lambda n, j, pt: (n, j, 0)),
            scratch_shapes=[pltpu.VMEM((2, n_pg * segs_per_page * 8, HEAD_DIM), F32)],
        ),
        out_shape=jax.ShapeDtypeStruct((n_dec * 8, n_seg, 2 * HEAD_DIM), F32),
        compiler_params=_params(2),
        name="cmp1_paged",
    )(page_table, *([pool4d] * n_pg), w1cat4)


def _nsa_decode_kernel(kvc_ref, qbd_ref, bias_c_ref, map_ref, gsum_ref, win_ref, new_ref, bias_w_ref,
                       oc_ref, ow_ref, sel_ref, *, n_cmp, n_blk, n_sub_w, k_sub_w):
    n_seg = kvc_ref.shape[1]
    sc = jnp.zeros((n_seg, HEAD_DIM), F32)
    for g in range(C_KV_HEADS):
        sc = sc + _dot(kvc_ref[g].astype(BF16), qbd_ref[0, g])
    valid = lax.broadcasted_iota(jnp.int32, (n_seg, HEAD_DIM), 0) < n_cmp
    lg = jnp.where(valid, sc * SOFTMAX_SCALE + bias_c_ref[...], NEG_INF)
    m = jnp.max(lg, axis=0, keepdims=True)
    e = jnp.where(valid, jnp.exp(lg - m), 0.0)
    p = e / jnp.maximum(jnp.sum(e, axis=0, keepdims=True), 1e-30)
    vc = jnp.concatenate([kvc_ref[C_KV_HEADS + g].astype(BF16) for g in range(C_KV_HEADS)], axis=1)
    _head_rows_store(oc_ref, _dot(p.astype(BF16), vc, _TN), jnp.ones((1, HEAD_DIM), F32))
    imp_h = _dot_f32_rhs(map_ref[...], p)
    imp = _dot_f32_lhs(imp_h, gsum_ref[...])
    blk = lax.broadcasted_iota(jnp.int32, imp.shape, 0)
    cur = n_blk - 1
    avail = blk <= cur
    forced = (blk == 0) | (blk == cur) | (blk == cur - 1)
    score = jnp.where(forced, jnp.inf, imp)
    score = jnp.where(avail, score, -jnp.inf)
    rank = _rank_rows(score, n_blk)
    sel_ref[0] = jnp.where(avail & (rank < min(C_SLC_TOPK, n_blk)), 1.0, 0.0)
    n_win = bias_w_ref.shape[0] - 8
    sw = jnp.zeros((n_win, HEAD_DIM), F32)
    s_new = jnp.zeros((8, HEAD_DIM), F32)
    v_parts, vn_parts = [], []
    for g in range(C_KV_HEADS):
        kg = win_ref[pl.ds(k_sub_w + g, n_win, stride=n_sub_w), :].astype(BF16)
        sw = sw + _dot(kg, qbd_ref[0, g])
        v_parts.append(win_ref[pl.ds(k_sub_w + C_KV_HEADS + g, n_win, stride=n_sub_w), :].astype(BF16))
        k_new = jnp.broadcast_to(new_ref[0, g:g + 1, :], (8, HEAD_DIM)).astype(BF16)
        s_new = s_new + _dot(k_new, qbd_ref[0, g])
        vn_parts.append(jnp.broadcast_to(new_ref[0, C_KV_HEADS + g:C_KV_HEADS + g + 1, :], (8, HEAD_DIM)).astype(BF16))
    lw = sw * SOFTMAX_SCALE + bias_w_ref[0:n_win, :]
    l_new = s_new[0:1] * SOFTMAX_SCALE + bias_w_ref[n_win + 1:n_win + 2, :]
    mw = jnp.maximum(jnp.max(lw, axis=0, keepdims=True), l_new)
    ew = jnp.exp(lw - mw)
    e_new = jnp.exp(l_new - mw)
    row8 = lax.broadcasted_iota(jnp.int32, (8, HEAD_DIM), 0)
    e_new8 = jnp.where(row8 == 0, jnp.broadcast_to(e_new, (8, HEAD_DIM)), 0.0)
    o_all = _dot(ew.astype(BF16), jnp.concatenate(v_parts, axis=1), _TN)
    o_all = o_all + _dot(e_new8.astype(BF16), jnp.concatenate(vn_parts, axis=1), _TN)
    _head_rows_store(ow_ref, o_all, jnp.sum(ew, axis=0, keepdims=True) + e_new)


def _nsa_decode(kvc, qbd, bias_c, cmap_t, gsum, win2d, new_win, bias_w, *, slot, n_blk):
    n_dec = qbd.shape[0]
    n_seg = kvc.shape[1]
    n_win = bias_w.shape[0] - 8
    n_blk_pad = cmap_t.shape[0]
    win_rows = n_win * 8
    o_sds = jax.ShapeDtypeStruct((n_dec, C_HEADS, HEAD_DIM), F32)
    o_spec = pl.BlockSpec((1, C_HEADS, HEAD_DIM), lambda n: (n, 0, 0))
    return pl.pallas_call(
        functools.partial(_nsa_decode_kernel, n_cmp=n_seg - 1, n_blk=n_blk, n_sub_w=8, k_sub_w=0),
        grid=(n_dec,),
        in_specs=[pl.BlockSpec((8, n_seg, HEAD_DIM), lambda n: (n, 0, 0)),
                  pl.BlockSpec((1, C_KV_HEADS, HEAD_DIM, HEAD_DIM), lambda n: (n, 0, 0, 0)),
                  pl.BlockSpec(bias_c.shape, lambda n: (0, 0)),
                  pl.BlockSpec(cmap_t.shape, lambda n: (0, 0)),
                  pl.BlockSpec(gsum.shape, lambda n: (0, 0)),
                  pl.BlockSpec((win_rows, HEAD_DIM), lambda n: (slot * n_dec + n, 0)),
                  pl.BlockSpec((1, 2 * C_KV_HEADS, HEAD_DIM), lambda n: (n, 0, 0)),
                  pl.BlockSpec(bias_w.shape, lambda n: (0, 0))],
        out_specs=[o_spec, o_spec, pl.BlockSpec((1, n_blk_pad, HEAD_DIM), lambda n: (n, 0, 0))],
        out_shape=[o_sds, o_sds, jax.ShapeDtypeStruct((n_dec, n_blk_pad, HEAD_DIM), F32)],
        compiler_params=_params(1),
        name="nsa_decode",
    )(kvc, qbd, bias_c, cmap_t, gsum, win2d, new_win, bias_w)


def _mixer_a_decode(x, proj, w_out, slot, bufs, rel_bias):
    n_dec = x.shape[0]
    qkv = proj.reshape(n_dec, len(A_PATTERNS), 3, A_HEADS, HEAD_DIM)
    dist = np.stack([np.concatenate([(w // d - np.arange(w // d)) * d, [0]]) for w, d in A_PATTERNS])
    bias = _bias_by_dist(rel_bias[:, :A_HEADS], dist)
    bias = jnp.broadcast_to(jnp.transpose(bias, (1, 2, 0))[..., None], dist.shape + (A_HEADS, HEAD_DIM))
    o = _a_decode(qkv, bufs, slot, bias)
    return _outproj([o.reshape(n_dec, -1)], [], w_out, slot, x, "plain", n_dec, 512)


def _mixer_b_decode(x, proj, w_out, slot, pool, page_table, bias_dec):
    n_dec = x.shape[0]
    nq = B_HEADS * HEAD_DIM
    qbd = _block_diag_q(proj[:, :nq].reshape(n_dec, B_HEADS, HEAD_DIM))
    new_kv = proj[:, nq:].reshape(n_dec, 2 * B_KV_HEADS, HEAD_DIM)
    n_sub = pool.shape[2] * 2 * B_KV_HEADS
    o = _paged_decode(page_table, pool.reshape(-1, HEAD_DIM), qbd, new_kv, bias_dec, None, n_pg=8, n_sub=n_sub,
                      k_sub=slot * 2 * B_KV_HEADS, v_sub=slot * 2 * B_KV_HEADS + B_KV_HEADS, blk=B_BLOCK, topk=B_TOPK)
    return _outproj([o.reshape(n_dec, -1)], [], w_out, slot, x, "plain", n_dec, 512)


def _mixer_c_decode(x, proj, gate, w_out, slot, cmp_w, pool, win_buf, page_table, bias_dec, rel_bias):
    n_dec = x.shape[0]
    n_pages = page_table.shape[1]
    past = n_pages * PAGE_SIZE
    assert win_buf.shape[2] == C_WINDOW, "window buffer must hold exactly C_WINDOW past rows"
    nq = C_HEADS * HEAD_DIM
    kvw = C_KV_HEADS * HEAD_DIM
    qbd = _block_diag_q(proj[:, :nq].reshape(n_dec, C_HEADS, HEAD_DIM))
    new_slc = proj[:, nq + 2 * kvw:nq + 4 * kvw].reshape(n_dec, 2 * C_KV_HEADS, HEAD_DIM)
    new_win = proj[:, nq + 4 * kvw:nq + 6 * kvw].reshape(n_dec, 2 * C_KV_HEADS, HEAD_DIM)
    w1cat, pe8, w1, w2 = cmp_w
    w1cat4 = jnp.concatenate([w1cat[0], w1cat[1]], axis=1)
    n_layers = pool.shape[2]
    n_sub = n_layers * 4 * C_KV_HEADS
    pool4d = pool.reshape(-1, C_CMP_STRIDE, n_sub, HEAD_DIM)
    y = _cmp1_paged(page_table, pool4d, w1cat4, sub_blk=slot * 2, n_pg=8)
    kvc = _cmp2(y, pe8, w1, w2)
    n_seg = kvc.shape[1]
    n_cmp = n_seg - 1
    n_blk = past // C_SLC_BLOCK + 1
    n_blk_pad = -(-n_blk // 8) * 8
    dist_c = past - (np.arange(n_seg) * C_CMP_STRIDE + (C_CMP_BLOCK - 1))
    bias_c = jnp.pad(rel_bias[jnp.asarray(_bucket_np(dist_c))], ((0, 0), (0, HEAD_DIM - C_HEADS)))
    cmap_t = jnp.asarray(_cmp_map_np(n_seg, n_cmp, n_blk_pad).T, BF16)
    lanes = np.arange(HEAD_DIM)
    gsum = jnp.asarray((lanes[:, None] // GQA_GROUP == lanes[None, :] // GQA_GROUP)
                       & (lanes[:, None] < C_HEADS) & (lanes[None, :] < C_HEADS), BF16)
    bias_w = _decode_bias(rel_bias, past, C_WINDOW)
    o_cmp, o_win, sel = _nsa_decode(kvc, qbd, bias_c, cmap_t, gsum, win_buf.reshape(-1, HEAD_DIM), new_win, bias_w,
                                    slot=slot, n_blk=n_blk)
    o_slc = _paged_decode(page_table, pool.reshape(-1, HEAD_DIM), qbd, new_slc, bias_dec, sel, n_pg=4, n_sub=n_sub,
                          k_sub=slot * 16 + 2 * C_KV_HEADS, v_sub=slot * 16 + 3 * C_KV_HEADS, blk=C_SLC_BLOCK, topk=None)
    return _outproj([o_cmp.reshape(n_dec, -1), o_slc.reshape(n_dec, -1), o_win.reshape(n_dec, -1)], [gate], w_out, slot,
                    x, "gate3", n_dec, 512)


def _shifted(buf, new_rows):
    return jnp.concatenate([buf[:, :, 1:], new_rows[:, :, None]], axis=2)


def kernel(x_prompt, x_sample, state_a_w128, state_a_w512, state_a_w2048, cache_b_kv, cache_c_kv, state_c_win, page_table, rel_bias, norm_mix, norm_ffn, norm_final, w_in_a, w_out_a, w_in_b, w_out_b, w_in_c, w_out_c, cmp_pe, cmp_w1, cmp_w2, w_up, w_down):
    b, s, d = x_prompt.shape
    n_dec, t_dec, _ = x_sample.shape
    depth = norm_mix.shape[0]
    assert t_dec == 1 and s % B_BLOCK == 0 and s % ATTN_TILE == 0
    past = page_table.shape[1] * PAGE_SIZE
    assert past % B_BLOCK == 0 and past % C_SLC_BLOCK == 0

    xp = x_prompt.reshape(b * s, d)
    xs = x_sample.reshape(n_dec, d)
    g_mix = norm_mix.reshape(depth, 1, d)
    g_ffn = norm_ffn.reshape(depth, 1, d)
    g_fin = norm_final.reshape(1, d)
    tabs = _tile_tables(rel_bias, ATTN_TILE)
    tabs_win = _tile_tables(rel_bias, ATTN_TILE, window=C_WINDOW)
    tabs_a = [_tile_tables(rel_bias[:, :A_HEADS], ATTN_TILE, window=w // dl, dist_scale=dl) for w, dl in A_PATTERNS]
    far = rel_bias[NUM_BUCKETS - 1]
    bias_dec = _decode_bias(rel_bias, past, DECODE_CHUNK)
    n_gate = 3 * C_HEADS
    n_main_c = (C_HEADS + 6 * C_KV_HEADS) * HEAD_DIM
    w_gate = jnp.pad(w_in_c[:, :, n_main_c:], ((0, 0), (0, 0), (0, HEAD_DIM - n_gate)))
    a_bufs = (state_a_w128, state_a_w512, state_a_w2048)
    w_in_a, w_in_b, w_in_c = w_in_a.astype(BF16), w_in_b.astype(BF16), w_in_c.astype(BF16)

    a_p, a_s = [[], [], []], [[], [], []]
    b_p, b_s, c_p, c_s, cw_p, cw_s = [], [], [], [], [], []
    for li in range(depth):
        kind, slot = li % 3, li // 3
        if kind == 0:
            xp, pp, ps = _mixer_a_prompt(xp, xs, b, g_mix, li, w_in_a, w_out_a, slot, tabs_a, far)
            xs = _mixer_a_decode(xs, ps, w_out_a, slot, a_bufs, rel_bias)
            gw = 3 * A_HEADS * HEAD_DIM
            kvw = 2 * A_HEADS * HEAD_DIM
            for g, (window, _) in enumerate(A_PATTERNS):
                c0 = g * gw + A_HEADS * HEAD_DIM
                a_p[g].append(pp[:, s - min(window, s):, c0:c0 + kvw].reshape(b, min(window, s), 2, A_HEADS, HEAD_DIM))
                a_s[g].append(ps[:, c0:c0 + kvw].reshape(n_dec, 2, A_HEADS, HEAD_DIM))
        elif kind == 1:
            xp, pp, ps = _mixer_b_prompt(xp, xs, b, g_mix, li, w_in_b, w_out_b, slot, tabs, far)
            xs = _mixer_b_decode(xs, ps, w_out_b, slot, cache_b_kv, page_table, bias_dec)
            nq = B_HEADS * HEAD_DIM
            b_p.append(pp[:, :, nq:].reshape(b, s, 2, B_KV_HEADS, HEAD_DIM))
            b_s.append(ps[:, nq:].reshape(n_dec, 1, 2, B_KV_HEADS, HEAD_DIM))
        else:
            cmp_w = _cmp_weights(cmp_pe, cmp_w1, cmp_w2, slot)
            xp, pp, ps, gate_s = _mixer_c_prompt(xp, xs, b, g_mix, li, w_in_c, w_gate[slot:slot + 1], w_out_c, slot, cmp_w,
                                                 tabs, tabs_win, far, rel_bias)
            xs = _mixer_c_decode(xs, ps, gate_s, w_out_c, slot, cmp_w, cache_c_kv, state_c_win, page_table, bias_dec,
                                 rel_bias)
            nq = C_HEADS * HEAD_DIM
            kvw = C_KV_HEADS * HEAD_DIM
            c_p.append(pp[:, :, nq:nq + 4 * kvw].reshape(b, s, 4, C_KV_HEADS, HEAD_DIM))
            c_s.append(ps[:, nq:nq + 4 * kvw].reshape(n_dec, 1, 4, C_KV_HEADS, HEAD_DIM))
            wlen = min(C_WINDOW, s)
            cw_p.append(pp[:, s - wlen:, nq + 4 * kvw:nq + 6 * kvw].reshape(b, wlen, 2, C_KV_HEADS, HEAD_DIM))
            cw_s.append(ps[:, nq + 4 * kvw:nq + 6 * kvw].reshape(n_dec, 2, C_KV_HEADS, HEAD_DIM))
        last = li == depth - 1
        xp, xs = _mlp(xp, xs, g_ffn, li, w_up, w_down, g_fin if last else None, _row_tile(b * s, PROMPT_ROW_TILE), 512)

    outs = [xp.reshape(b, s, d), xs.reshape(n_dec, 1, d)]
    for g in range(len(A_PATTERNS)):
        outs.append(jnp.stack(a_p[g], axis=0))
        outs.append(_shifted(a_bufs[g], jnp.stack(a_s[g], axis=0)))
    outs += [jnp.stack(b_p, axis=2), jnp.stack(b_s, axis=2), jnp.stack(c_p, axis=2), jnp.stack(c_s, axis=2),
             jnp.stack(cw_p, axis=0), _shifted(state_c_win, jnp.stack(cw_s, axis=0))]
    return tuple(outs)
```

```python
import functools
import math

import jax
import jax.numpy as jnp
import numpy as np
from jax import lax
from jax.experimental import pallas as pl
from jax.experimental.pallas import tpu as pltpu

F32 = jnp.float32
BF16 = jnp.bfloat16

HEAD_DIM = 128
RMS_EPS = 1e-6
NEG_INF = -1e30
SOFTMAX_SCALE = HEAD_DIM ** -0.5
LOG2E = math.log2(math.e)
PAGE_SIZE = 128

NUM_BUCKETS = 32
MAX_DISTANCE = 128

A_PATTERNS = ((128, 1), (512, 4), (2048, 16))
A_HEADS = 8
B_HEADS = 16
B_KV_HEADS = 4
B_BLOCK = 256
B_TOPK = 3
C_HEADS = 16
C_KV_HEADS = 4
C_CMP_BLOCK = 32
C_CMP_STRIDE = 16
C_SLC_BLOCK = 64
C_SLC_TOPK = 16
C_WINDOW = 512
GQA_GROUP = 4

ATTN_TILE = 256
PROMPT_ROW_TILE = 1024
VMEM_LIMIT_BYTES = 56 * 1024 * 1024


def _params(n_axes, vmem=VMEM_LIMIT_BYTES):
    return pltpu.CompilerParams(dimension_semantics=("arbitrary",) * n_axes, vmem_limit_bytes=vmem)


def _bucket_np(dist):
    exact = NUM_BUCKETS // 2
    d = np.maximum(np.asarray(dist, np.int64), 0)
    far = exact + (np.log(np.maximum(d, exact).astype(np.float64) / exact)
                   / math.log(MAX_DISTANCE / exact) * (NUM_BUCKETS - exact)).astype(np.int64)
    return np.where(d < exact, d, np.minimum(far, NUM_BUCKETS - 1)).astype(np.int32)


def _bias_by_dist(rel_bias, dist_np):
    tab = rel_bias[jnp.asarray(_bucket_np(dist_np))]
    return jnp.moveaxis(tab, -1, 0)


def _toeplitz(w, t):
    return jnp.tile(w, (1, t))[:, :2 * t * t].reshape(w.shape[0], t, 2 * t)[:, :, :t]


def _tile_tables(rel_bias, t, window=None, dil=1, dist_scale=1):
    assert window is None or window < t or (window % t == 0 and window >= 2 * t)
    k = np.arange(2 * t + 1)
    e = np.where(k <= t, -k, 2 * t + 1 - k)

    def table(dist, ok):
        vals = rel_bias[jnp.asarray(_bucket_np(dist * dist_scale))].T
        return _toeplitz(jnp.where(jnp.asarray(ok)[None], vals, NEG_INF), t)

    def allowed(dist):
        ok = dist >= 0
        if window is not None:
            ok &= dist <= window
        if dil > 1:
            ok &= dist % dil == 0
        return ok

    tabs = [table(e, allowed(e)), table(t + e, allowed(t + e))]
    if dil > 1:
        tabs.append(table(2 * t + e, (2 * t + e) % dil == 0))
    if window is not None and window >= 2 * t:
        tabs.append(table(window + e, allowed(window + e)))
    return jnp.stack(tabs, axis=1)


def _rms(x, g):
    return x * lax.rsqrt(jnp.mean(x * x, axis=-1, keepdims=True) + RMS_EPS) * g


def _split_bf16(x, parts):
    out = []
    for _ in range(parts):
        p = x.astype(BF16)
        out.append(p)
        x = x - p.astype(F32)
    return out


def _dot(a, b, dims=(((1,), (0,)), ((), ()))):
    return lax.dot_general(a, b, dims, preferred_element_type=F32)


_NT = (((1,), (1,)), ((), ()))
_TN = (((0,), (0,)), ((), ()))


def _dot_f32_lhs(a, b_bf16, dims=(((1,), (0,)), ((), ()))):
    return sum(_dot(p, b_bf16, dims) for p in _split_bf16(a, 3))


def _dot_f32_rhs(a_bf16, b, dims=(((1,), (0,)), ((), ()))):
    return sum(_dot(a_bf16, p, dims) for p in _split_bf16(b, 3))


def _dot_f32_both(a, b, dims):
    a_hi, a_lo = _split_bf16(a, 2)
    b_hi, b_lo = _split_bf16(b, 2)
    return _dot(a_hi, b_hi, dims) + (_dot(a_hi, b_lo, dims) + _dot(a_lo, b_hi, dims))


def _rank_rows(score, n_rows):
    row = lax.broadcasted_iota(jnp.int32, score.shape, 0)
    rank = jnp.zeros(score.shape, jnp.int32)
    for m in range(n_rows):
        sm = score[m:m + 1, :]
        beats = (sm > score) | ((sm == score) & (row > m))
        rank = rank + beats.astype(jnp.int32)
    return rank


def _rank_lanes(score, n_lanes):
    n_rows, width = score.shape
    n_pad = -(-n_lanes // 8) * 8
    rank_t = _rank_rows(score.T[:n_pad], n_lanes).astype(F32)
    if n_pad < width:
        rank_t = jnp.concatenate([rank_t, jnp.zeros((width - n_pad, n_rows), F32)], axis=0)
    return rank_t.T


def _norm_matmul_kernel(x_ref, g_ref, w_ref, o_ref, h_ref):
    @pl.when(pl.program_id(1) == 0)
    def _():
        h_ref[...] = _rms(x_ref[...], g_ref[...]).astype(BF16)

    o_ref[...] = _dot(h_ref[...], w_ref[...].astype(BF16))


def _norm_matmul2_kernel(x_ref, xs_ref, g_ref, w_ref, o_ref, os_ref, h_ref, hs_ref):
    i = pl.program_id(0)
    j = pl.program_id(1)

    @pl.when(j == 0)
    def _():
        h_ref[...] = _rms(x_ref[...], g_ref[...]).astype(BF16)

    @pl.when((i == 0) & (j == 0))
    def _():
        hs_ref[...] = _rms(xs_ref[...], g_ref[...]).astype(BF16)

    w = w_ref[...].astype(BF16)
    o_ref[...] = _dot(h_ref[...], w)

    @pl.when(i == 0)
    def _():
        os_ref[j] = _dot(hs_ref[...], w)


def _norm_matmul(x, g_all, li, w_all, slot, n_cols, tm, tn, xs=None):
    m, k = x.shape
    n_j = n_cols // tn
    x_spec = pl.BlockSpec((tm, k), lambda i, j: (i, 0), pipeline_mode=pl.Buffered(1))
    g_spec = pl.BlockSpec((None, 1, k), lambda i, j: (li, 0, 0))
    w_spec = pl.BlockSpec((None, k, tn), lambda i, j: (slot, 0, j))
    o_spec = pl.BlockSpec((tm, tn), lambda i, j: (i, j))
    if xs is None:
        return pl.pallas_call(
            _norm_matmul_kernel,
            grid=(m // tm, n_j),
            in_specs=[x_spec, g_spec, w_spec],
            out_specs=o_spec,
            out_shape=jax.ShapeDtypeStruct((m, n_cols), F32),
            scratch_shapes=[pltpu.VMEM((tm, k), BF16)],
            compiler_params=_params(2),
            name="norm_matmul",
        )(x, g_all, w_all)
    ms = xs.shape[0]
    out, out_s = pl.pallas_call(
        _norm_matmul2_kernel,
        grid=(m // tm, n_j),
        in_specs=[x_spec, pl.BlockSpec((ms, k), lambda i, j: (0, 0)), g_spec, w_spec],
        out_specs=[o_spec, pl.BlockSpec((n_j, ms, tn), lambda i, j: (0, 0, 0))],
        out_shape=[jax.ShapeDtypeStruct((m, n_cols), F32), jax.ShapeDtypeStruct((n_j, ms, tn), F32)],
        scratch_shapes=[pltpu.VMEM((tm, k), BF16), pltpu.VMEM((ms, k), BF16)],
        compiler_params=_params(2),
        name="norm_matmul2",
    )(x, xs, g_all, w_all)
    return out, jnp.swapaxes(out_s, 0, 1).reshape(ms, n_cols)


def _outproj_kernel(*refs, mode, n_heads):
    if mode == "plain":
        o_refs, extra, (w_ref, x_ref, out_ref, a_ref) = refs[:1], (), refs[1:]
    elif mode == "lse3":
        o_refs, extra, (w_ref, x_ref, out_ref, a_ref) = refs[:3], refs[3:6], refs[6:]
    else:
        o_refs, extra, (w_ref, x_ref, out_ref, a_ref) = refs[:3], refs[3:5], refs[5:]

    @pl.when(pl.program_id(1) == 0)
    def _():
        if mode == "plain":
            a = o_refs[0][...]
        elif mode == "lse3":
            lse = [r[...] for r in extra]
            mx = jnp.maximum(jnp.maximum(lse[0], lse[1]), lse[2])
            e = [jnp.exp(l - mx) for l in lse]
            den = e[0] + e[1] + e[2]
            a = sum((e[g] / den) * o_refs[g][...] for g in range(3))
        else:
            gate = jax.nn.sigmoid(extra[0][...])
            a = sum(_dot_f32_lhs(gate, extra[1][br]) * o_refs[br][...] for br in range(3))
        a_ref[...] = a.astype(BF16)

    out_ref[...] = x_ref[...] + _dot(a_ref[...], w_ref[...].astype(BF16))


def _outproj(o_list, extra_list, w_all, slot, x, mode, tm, tn):
    m, k = o_list[0].shape
    n = x.shape[1]
    row_spec = lambda width: pl.BlockSpec((tm, width), lambda i, j: (i, 0))
    in_specs = [row_spec(k) for _ in o_list] + [row_spec(e.shape[1]) for e in extra_list]
    if mode == "gate3":
        spread = np.zeros((3, HEAD_DIM, k), np.float32)
        for h in range(k // HEAD_DIM):
            for br in range(3):
                spread[br, 3 * h + br, h * HEAD_DIM:(h + 1) * HEAD_DIM] = 1.0
        extra_list = list(extra_list) + [jnp.asarray(spread, BF16)]
        in_specs.append(pl.BlockSpec(spread.shape, lambda i, j: (0, 0, 0)))
    in_specs += [pl.BlockSpec((None, k, tn), lambda i, j: (slot, 0, j)),
                 pl.BlockSpec((tm, tn), lambda i, j: (i, j))]
    return pl.pallas_call(
        functools.partial(_outproj_kernel, mode=mode, n_heads=k // HEAD_DIM),
        grid=(m // tm, n // tn),
        in_specs=in_specs,
        out_specs=pl.BlockSpec((tm, tn), lambda i, j: (i, j)),
        out_shape=jax.ShapeDtypeStruct((m, n), F32),
        scratch_shapes=[pltpu.VMEM((tm, k), BF16)],
        compiler_params=_params(2),
        name="outproj_" + mode,
    )(*o_list, *extra_list, w_all, x)


def _mlp_kernel(*refs, final):
    if final:
        x_ref, xs_ref, g_ref, wu_ref, wd_ref, gf_ref, o_ref, os_ref, h_ref, hs_ref = refs
    else:
        x_ref, xs_ref, g_ref, wu_ref, wd_ref, o_ref, os_ref, h_ref, hs_ref = refs
    i = pl.program_id(0)
    f = pl.program_id(1)
    last = pl.num_programs(1) - 1

    @pl.when(f == 0)
    def _():
        x = x_ref[...]
        h_ref[...] = _rms(x, g_ref[...]).astype(BF16)
        o_ref[...] = x

    @pl.when((i == 0) & (f == 0))
    def _():
        xs = xs_ref[...]
        hs_ref[...] = _rms(xs, g_ref[...]).astype(BF16)
        os_ref[...] = xs

    wu = wu_ref[...].astype(BF16)
    wd = wd_ref[...].astype(BF16)

    def up_act_down(h):
        u = _dot(h, wu)
        return _dot(jnp.square(jnp.maximum(u, 0.0)).astype(BF16), wd)

    o_ref[...] += up_act_down(h_ref[...])

    @pl.when(i == 0)
    def _():
        os_ref[...] += up_act_down(hs_ref[...])

    if final:
        @pl.when(f == last)
        def _():
            o_ref[...] = _rms(o_ref[...], gf_ref[...])

        @pl.when((i == 0) & (f == last))
        def _():
            os_ref[...] = _rms(os_ref[...], gf_ref[...])


def _mlp(x, xs, g_all, li, w_up, w_down, final_g, tm, tf):
    m, d = x.shape
    ms = xs.shape[0]
    d_ff = w_up.shape[2]
    in_specs = [
        pl.BlockSpec((tm, d), lambda i, f: (i, 0), pipeline_mode=pl.Buffered(1)),
        pl.BlockSpec((ms, d), lambda i, f: (0, 0)),
        pl.BlockSpec((None, 1, d), lambda i, f: (li, 0, 0)),
        pl.BlockSpec((None, d, tf), lambda i, f: (li, 0, f)),
        pl.BlockSpec((None, tf, d), lambda i, f: (li, f, 0)),
    ]
    args = [x, xs, g_all, w_up, w_down]
    if final_g is not None:
        in_specs.append(pl.BlockSpec((1, d), lambda i, f: (0, 0)))
        args.append(final_g)
    return pl.pallas_call(
        functools.partial(_mlp_kernel, final=final_g is not None),
        grid=(m // tm, d_ff // tf),
        in_specs=in_specs,
        out_specs=[pl.BlockSpec((tm, d), lambda i, f: (i, 0)), pl.BlockSpec((ms, d), lambda i, f: (0, 0))],
        out_shape=[jax.ShapeDtypeStruct((m, d), F32), jax.ShapeDtypeStruct((ms, d), F32)],
        scratch_shapes=[pltpu.VMEM((tm, d), BF16), pltpu.VMEM((ms, d), BF16)],
        compiler_params=_params(2),
        name="mlp",
    )(*args)


def _flash_kernel_old(*refs, group, t, window, dil, sel_blk, sel_per_head, want_lse, head0):
    it = iter(refs)
    q_ref, k_ref, v_ref, tab_ref, far_ref = (next(it) for _ in range(5))
    sel_ref = next(it) if sel_blk else None
    o_ref = next(it)
    lse_ref = next(it) if want_lse else None
    kb_ref, vb_ref, qs_ref, m_ref, l_ref, acc_ref = (next(it) for _ in range(6))

    kvh = pl.program_id(1)
    qi = pl.program_id(2)
    rows = group * t

    @pl.when(qi == 0)
    def _():
        kb_ref[...] = k_ref[0].astype(BF16)
        vb_ref[...] = v_ref[0].astype(BF16)

    for j in range(group):
        qs_ref[j * t:(j + 1) * t, :] = q_ref[0, :, j * HEAD_DIM:(j + 1) * HEAD_DIM].astype(BF16)
    m_ref[...] = jnp.full((rows, HEAD_DIM), NEG_INF, F32)
    l_ref[...] = jnp.zeros((rows, HEAD_DIM), F32)
    acc_ref[...] = jnp.zeros((rows, HEAD_DIM), F32)

    delta = (lax.broadcasted_iota(jnp.int32, (t, t), 0) - lax.broadcasted_iota(jnp.int32, (t, t), 1))

    def tile(kj, kind):
        start = pl.multiple_of(kj * t, t)
        k = kb_ref[pl.ds(start, t), :]
        v = vb_ref[pl.ds(start, t), :]
        s = _dot(qs_ref[...], k, _NT) * SOFTMAX_SCALE
        dist = delta + (qi - kj) * t
        ok = None
        if kind == "diag":
            ok = dist >= 0
        if window is not None:
            w_ok = dist <= window
            ok = w_ok if ok is None else ok & w_ok
        if dil > 1:
            d_ok = (dist & (dil - 1)) == 0
            ok = d_ok if ok is None else ok & d_ok
        sel_ok = None
        if sel_blk:
            blk_of_key = (start + lax.broadcasted_iota(jnp.int32, (HEAD_DIM, t), 1)) // sel_blk
            expand = (lax.broadcasted_iota(jnp.int32, (HEAD_DIM, t), 0) == blk_of_key)
            expand = jnp.where(expand, 1.0, 0.0).astype(BF16)
            if sel_per_head:
                sel_rows = jnp.concatenate([sel_ref[0, j] for j in range(group)], axis=0)
            else:
                sel_rows = sel_ref[0, 0]
            sel_ok = _dot(sel_rows.astype(BF16), expand) > 0.5
        pieces = []
        for j in range(group):
            sj = s[j * t:(j + 1) * t]
            if kind == "far":
                sj = sj + far_ref[head0 + kvh * group + j]
            else:
                sj = sj + tab_ref[j, 0 if kind == "diag" else 1]
            mask = ok
            if sel_blk:
                sel_j = sel_ok[j * t:(j + 1) * t] if sel_per_head else sel_ok
                mask = sel_j if mask is None else mask & sel_j
            pieces.append((sj, mask))
        if pieces[0][1] is not None:
            s = jnp.concatenate([jnp.where(mk, sj, NEG_INF) for sj, mk in pieces], axis=0)
            mask_all = jnp.concatenate([mk for _, mk in pieces], axis=0)
        else:
            s = jnp.concatenate([sj for sj, _ in pieces], axis=0)
            mask_all = None
        m_prev = m_ref[...]
        m_next = jnp.maximum(m_prev, jnp.max(s, axis=1, keepdims=True))
        p = jnp.exp(s - jnp.concatenate([m_next] * (t // HEAD_DIM), axis=1))
        if mask_all is not None:
            p = jnp.where(mask_all, p, 0.0)
        alpha = jnp.exp(m_prev - m_next)
        l_ref[...] = alpha * l_ref[...] + jnp.sum(p, axis=1, keepdims=True)
        acc_ref[...] = alpha * acc_ref[...] + _dot(p.astype(BF16), v)
        m_ref[...] = m_next

    lo = 0 if window is None else jnp.maximum(qi * t - window, 0) // t

    def far_body(kj, carry):
        tile(kj, "far")
        return carry

    lax.fori_loop(lo, jnp.maximum(qi - 1, lo), far_body, 0)

    @pl.when(qi - 1 >= lo)
    def _():
        tile(qi - 1, "near")

    tile(qi, "diag")

    l = l_ref[...]
    out = acc_ref[...] / l
    for j in range(group):
        o_ref[0, :, j * HEAD_DIM:(j + 1) * HEAD_DIM] = out[j * t:(j + 1) * t]
    if want_lse:
        lse = m_ref[...] + jnp.log(l)
        for j in range(group):
            lse_ref[0, :, j * HEAD_DIM:(j + 1) * HEAD_DIM] = lse[j * t:(j + 1) * t]


def _flash_old(proj, tabs, far, sel, *, n_kv, group, q_col, k_col, v_col, head0=0,
           window=None, dil=1, sel_blk=0, sel_per_head=False, want_lse=False):
    b, s, _ = proj.shape
    t = ATTN_TILE
    gw = group * HEAD_DIM
    in_specs = [
        pl.BlockSpec((1, t, gw), lambda bi, h, qi: (bi, qi, q_col // group + h)),
        pl.BlockSpec((1, s, HEAD_DIM), lambda bi, h, qi: (bi, 0, k_col + h)),
        pl.BlockSpec((1, s, HEAD_DIM), lambda bi, h, qi: (bi, 0, v_col + h)),
        pl.BlockSpec((group, 2, t, t), lambda bi, h, qi: (head0 // group + h, 0, 0, 0)),
        pl.BlockSpec(memory_space=pltpu.SMEM),
    ]
    args = [proj, proj, proj, tabs, far]
    if sel_blk:
        if sel_per_head:
            in_specs.append(pl.BlockSpec((1, group, t, HEAD_DIM), lambda bi, h, qi: (bi, h, qi, 0)))
        else:
            in_specs.append(pl.BlockSpec((1, 1, t, HEAD_DIM), lambda bi, h, qi: (bi, h, qi, 0)))
        args.append(sel)
    out_spec = pl.BlockSpec((1, t, gw), lambda bi, h, qi: (bi, qi, h))
    out_sds = jax.ShapeDtypeStruct((b, s, n_kv * gw), F32)
    rows = group * t
    return pl.pallas_call(
        functools.partial(_flash_kernel, group=group, t=t, window=window, dil=dil, sel_blk=sel_blk,
                          sel_per_head=sel_per_head, want_lse=want_lse, head0=head0),
        grid=(b, n_kv, s // t),
        in_specs=in_specs,
        out_specs=[out_spec, out_spec] if want_lse else out_spec,
        out_shape=[out_sds, out_sds] if want_lse else out_sds,
        scratch_shapes=[pltpu.VMEM((s, HEAD_DIM), BF16), pltpu.VMEM((s, HEAD_DIM), BF16),
                        pltpu.VMEM((rows, HEAD_DIM), BF16), pltpu.VMEM((rows, HEAD_DIM), F32),
                        pltpu.VMEM((rows, HEAD_DIM), F32), pltpu.VMEM((rows, HEAD_DIM), F32)],
        compiler_params=_params(3),
        name="flash",
    )(*args)


def _flash_kernel(*refs, group, t, w_tiles, far_tab, edge_tab, sel_blk, sel_per_head, want_lse, head0):
    it = iter(refs)
    q_ref, k_ref, v_ref, tab_ref, far_ref = (next(it) for _ in range(5))
    sel_ref = next(it) if sel_blk else None
    o_ref = next(it)
    lse_ref = next(it) if want_lse else None
    kb_ref, vb_ref, qs_ref, s_ref, m_ref, l_ref, acc_ref = (next(it) for _ in range(7))

    kvh = pl.program_id(1)
    qi = pl.program_id(2)
    rows = group * t

    @pl.when(qi == 0)
    def _():
        kb_ref[...] = k_ref[0].astype(BF16)
        vb_ref[...] = v_ref[0].astype(BF16)

    for j in range(group):
        qs_ref[j * t:(j + 1) * t, :] = q_ref[0, :, j * HEAD_DIM:(j + 1) * HEAD_DIM].astype(BF16)

    def tile(kj, tab, first=False):
        start = pl.multiple_of(kj * t, t)
        k = kb_ref[pl.ds(start, t), :]
        v = vb_ref[pl.ds(start, t), :]
        sel_add = None
        if sel_blk and not sel_per_head:
            blk_of_key = (start + lax.broadcasted_iota(jnp.int32, (HEAD_DIM, t), 1)) // sel_blk
            expand = lax.broadcasted_iota(jnp.int32, (HEAD_DIM, t), 0) == blk_of_key
            expand = jnp.where(expand, 1.0, 0.0).astype(BF16)
            sel_add = _dot(sel_ref[0, 0].astype(BF16), expand)
        for j in range(group):
            hs = slice(j * t, (j + 1) * t)
            sj = _dot(qs_ref[hs, :], k, _NT) * SOFTMAX_SCALE
            sj = sj + (far_ref[head0 + kvh * group + j] if tab is None else tab_ref[j, tab])
            if sel_blk and sel_per_head:
                assert sel_blk == t
                lane = lax.broadcasted_iota(jnp.int32, (t, HEAD_DIM), 1)
                sj = sj + jnp.sum(jnp.where(lane == kj, sel_ref[0, j], 0.0), axis=1, keepdims=True)
            elif sel_blk:
                sj = sj + sel_add
            m_cur = jnp.max(jnp.maximum(sj[:, :HEAD_DIM], sj[:, HEAD_DIM:]), axis=1, keepdims=True)
            if first:
                m_next = jnp.broadcast_to(m_cur, (t, HEAD_DIM))
            else:
                m_prev = m_ref[hs, :]
                m_next = jnp.maximum(m_prev, m_cur)
            p = jnp.exp(sj - jnp.concatenate([m_next] * (t // HEAD_DIM), axis=1))
            p_sum = jnp.sum(p[:, :HEAD_DIM] + p[:, HEAD_DIM:], axis=1, keepdims=True)
            pv = _dot(p.astype(BF16), v)
            if first:
                l_ref[hs, :] = jnp.broadcast_to(p_sum, (t, HEAD_DIM))
                acc_ref[hs, :] = pv
            else:
                alpha = jnp.exp(m_prev - m_next)
                l_ref[hs, :] = alpha * l_ref[hs, :] + p_sum
                acc_ref[hs, :] = alpha * acc_ref[hs, :] + pv
            m_ref[hs, :] = m_next

    tile(qi, 0, first=True)

    @pl.when(qi >= 1)
    def _():
        tile(qi - 1, 1)

    def far_body(kj, carry):
        tile(kj, far_tab)
        return carry

    if w_tiles is None:
        lax.fori_loop(0, qi - 1, far_body, 0)
    elif w_tiles >= 2:
        lax.fori_loop(jnp.maximum(qi - w_tiles + 1, 0), qi - 1, far_body, 0)

        @pl.when(qi >= w_tiles)
        def _():
            tile(qi - w_tiles, edge_tab)

    l = l_ref[...]
    out = acc_ref[...] / l
    for j in range(group):
        o_ref[0, :, j * HEAD_DIM:(j + 1) * HEAD_DIM] = out[j * t:(j + 1) * t]
    if want_lse:
        lse = m_ref[...] + jnp.log(l)
        for j in range(group):
            lse_ref[0, :, j * HEAD_DIM:(j + 1) * HEAD_DIM] = lse[j * t:(j + 1) * t]


def _flash(proj, tabs, far, sel, *, n_kv, group, q_col, k_col, v_col, head0=0,
           window=None, dil=1, sel_blk=0, sel_per_head=False, want_lse=False):
    b, s, _ = proj.shape
    t = ATTN_TILE
    gw = group * HEAD_DIM
    n_tab = tabs.shape[1]
    w_tiles = None if window is None else window // t
    far_tab = 2 if dil > 1 else None
    edge_tab = n_tab - 1 if (w_tiles is not None and w_tiles >= 2) else None
    in_specs = [
        pl.BlockSpec((1, t, gw), lambda bi, h, qi: (bi, qi, q_col // group + h)),
        pl.BlockSpec((1, s, HEAD_DIM), lambda bi, h, qi: (bi, 0, k_col + h)),
        pl.BlockSpec((1, s, HEAD_DIM), lambda bi, h, qi: (bi, 0, v_col + h)),
        pl.BlockSpec((group, n_tab, t, t), lambda bi, h, qi: (head0 // group + h, 0, 0, 0)),
        pl.BlockSpec(memory_space=pltpu.SMEM),
    ]
    args = [proj, proj, proj, tabs, far]
    if sel_blk:
        if sel_per_head:
            in_specs.append(pl.BlockSpec((1, group, t, HEAD_DIM), lambda bi, h, qi: (bi, h, qi, 0)))
        else:
            in_specs.append(pl.BlockSpec((1, 1, t, HEAD_DIM), lambda bi, h, qi: (bi, h, qi, 0)))
        args.append(sel)
    out_spec = pl.BlockSpec((1, t, gw), lambda bi, h, qi: (bi, qi, h))
    out_sds = jax.ShapeDtypeStruct((b, s, n_kv * gw), F32)
    rows = group * t
    return pl.pallas_call(
        functools.partial(_flash_kernel, group=group, t=t, w_tiles=w_tiles, far_tab=far_tab, edge_tab=edge_tab,
                          sel_blk=sel_blk, sel_per_head=sel_per_head, want_lse=want_lse, head0=head0),
        grid=(b, n_kv, s // t),
        in_specs=in_specs,
        out_specs=[out_spec, out_spec] if want_lse else out_spec,
        out_shape=[out_sds, out_sds] if want_lse else out_sds,
        scratch_shapes=[pltpu.VMEM((s, HEAD_DIM), BF16), pltpu.VMEM((s, HEAD_DIM), BF16),
                        pltpu.VMEM((rows, HEAD_DIM), BF16), pltpu.VMEM((rows, t), F32),
                        pltpu.VMEM((rows, HEAD_DIM), F32), pltpu.VMEM((rows, HEAD_DIM), F32),
                        pltpu.VMEM((rows, HEAD_DIM), F32)],
        compiler_params=_params(3),
        name="flash",
    )(*args)


def _flash2_kernel(*refs, group, t, w_tiles, far_tab, edge_tab, sel_blk, sel_per_head, head0):
    it = iter(refs)
    q_ref, k_ref, v_ref, tab_ref, far_ref = (next(it) for _ in range(5))
    sel_ref = next(it) if sel_blk else None
    o_ref = next(it)
    kb_ref, vb_ref, qs_ref, s_ref, mx_ref, l_ref, acc_ref = (next(it) for _ in range(7))

    kvh = pl.program_id(1)
    qi = pl.program_id(2)
    rows = group * t

    @pl.when(qi == 0)
    def _():
        kb_ref[...] = k_ref[0].astype(BF16)
        vb_ref[...] = v_ref[0].astype(BF16)

    for j in range(group):
        qs_ref[j * t:(j + 1) * t, :] = q_ref[0, :, j * HEAD_DIM:(j + 1) * HEAD_DIM].astype(BF16)

    def scores(tiles, tab, first=False):
        per_head = [[] for _ in range(group)]
        for kj in tiles:
            start = pl.multiple_of(kj * t, t)
            k = kb_ref[pl.ds(start, t), :]
            sel_add = None
            if sel_blk and not sel_per_head:
                blk_of_key = (start + lax.broadcasted_iota(jnp.int32, (HEAD_DIM, t), 1)) // sel_blk
                expand = lax.broadcasted_iota(jnp.int32, (HEAD_DIM, t), 0) == blk_of_key
                expand = jnp.where(expand, 1.0, 0.0).astype(BF16)
                sel_add = _dot(sel_ref[0, 0].astype(BF16), expand)
            for j in range(group):
                hs = slice(j * t, (j + 1) * t)
                sj = _dot(qs_ref[hs, :], k, _NT) * (SOFTMAX_SCALE * LOG2E)
                sj = sj + (far_ref[head0 + kvh * group + j] if tab is None else tab_ref[j, tab])
                if sel_blk and sel_per_head:
                    assert sel_blk == t
                    lane = lax.broadcasted_iota(jnp.int32, (t, HEAD_DIM), 1)
                    sj = sj + jnp.sum(jnp.where(lane == kj, sel_ref[0, j], 0.0), axis=1, keepdims=True)
                elif sel_blk:
                    sj = sj + sel_add
                s_ref[kj, hs, :] = sj
                per_head[j].append(jnp.maximum(sj[:, :HEAD_DIM], sj[:, HEAD_DIM:]))
        for j in range(group):
            hs = slice(j * t, (j + 1) * t)
            mj = functools.reduce(jnp.maximum, per_head[j])
            mx_ref[hs, :] = mj if first else jnp.maximum(mx_ref[hs, :], mj)

    def accumulate(tiles, tab, first=False):
        for j in range(group):
            hs = slice(j * t, (j + 1) * t)
            m = mx_ref[hs, :]
            m2 = jnp.concatenate([m, m], axis=1)
            p_lane, pv = None, None
            for kj in tiles:
                v = vb_ref[pl.ds(pl.multiple_of(kj * t, t), t), :]
                p = jnp.exp2(s_ref[kj, hs, :] - m2)
                pl_ = p[:, :HEAD_DIM] + p[:, HEAD_DIM:]
                pv_ = _dot(p.astype(BF16), v)
                p_lane = pl_ if p_lane is None else p_lane + pl_
                pv = pv_ if pv is None else pv + pv_
            l_ref[hs, :] = p_lane if first else l_ref[hs, :] + p_lane
            acc_ref[hs, :] = pv if first else acc_ref[hs, :] + pv

    def sweep(visit):
        visit([qi], 0, first=True)

        @pl.when(qi >= 1)
        def _():
            visit([qi - 1], 1)

        def far_tiles(lo, hi):
            def pair_body(i, carry):
                visit([lo + 2 * i, lo + 2 * i + 1], far_tab)
                return carry

            n_far = jnp.maximum(hi - lo, 0)
            lax.fori_loop(0, n_far // 2, pair_body, 0)

            @pl.when(n_far % 2 == 1)
            def _():
                visit([hi - 1], far_tab)

        if w_tiles is None:
            far_tiles(0, qi - 1)
        elif w_tiles >= 2:
            far_tiles(jnp.maximum(qi - w_tiles + 1, 0), qi - 1)

            @pl.when(qi >= w_tiles)
            def _():
                visit([qi - w_tiles], edge_tab)

    sweep(scores)
    mx_ref[...] = jnp.broadcast_to(jnp.max(mx_ref[...], axis=1, keepdims=True), (rows, HEAD_DIM))
    sweep(accumulate)

    out = acc_ref[...] / jnp.sum(l_ref[...], axis=1, keepdims=True)
    for j in range(group):
        o_ref[0, :, j * HEAD_DIM:(j + 1) * HEAD_DIM] = out[j * t:(j + 1) * t]


def _flash2(proj, tabs, far, sel, *, n_kv, group, q_col, k_col, v_col, head0=0,
            window=None, sel_blk=0, sel_per_head=False):
    b, s, _ = proj.shape
    t = ATTN_TILE
    assert t == 2 * HEAD_DIM
    gw = group * HEAD_DIM
    n_tab = tabs.shape[1]
    w_tiles = None if window is None else window // t
    edge_tab = n_tab - 1 if (w_tiles is not None and w_tiles >= 2) else None
    tabs, far = tabs * LOG2E, far * LOG2E
    in_specs = [
        pl.BlockSpec((1, t, gw), lambda bi, h, qi: (bi, qi, q_col // group + h)),
        pl.BlockSpec((1, s, HEAD_DIM), lambda bi, h, qi: (bi, 0, k_col + h)),
        pl.BlockSpec((1, s, HEAD_DIM), lambda bi, h, qi: (bi, 0, v_col + h)),
        pl.BlockSpec((group, n_tab, t, t), lambda bi, h, qi: (head0 // group + h, 0, 0, 0)),
        pl.BlockSpec(memory_space=pltpu.SMEM),
    ]
    args = [proj, proj, proj, tabs, far]
    if sel_blk:
        if sel_per_head:
            in_specs.append(pl.BlockSpec((1, group, t, HEAD_DIM), lambda bi, h, qi: (bi, h, qi, 0)))
        else:
            in_specs.append(pl.BlockSpec((1, 1, t, HEAD_DIM), lambda bi, h, qi: (bi, h, qi, 0)))
        args.append(sel)
    rows = group * t
    return pl.pallas_call(
        functools.partial(_flash2_kernel, group=group, t=t, w_tiles=w_tiles, far_tab=None, edge_tab=edge_tab,
                          sel_blk=sel_blk, sel_per_head=sel_per_head, head0=head0),
        grid=(b, n_kv, s // t),
        in_specs=in_specs,
        out_specs=pl.BlockSpec((1, t, gw), lambda bi, h, qi: (bi, qi, h)),
        out_shape=jax.ShapeDtypeStruct((b, s, n_kv * gw), F32),
        scratch_shapes=[pltpu.VMEM((s, HEAD_DIM), BF16), pltpu.VMEM((s, HEAD_DIM), BF16),
                        pltpu.VMEM((rows, HEAD_DIM), BF16), pltpu.VMEM((s // t, rows, t), F32),
                        pltpu.VMEM((rows, HEAD_DIM), F32), pltpu.VMEM((rows, HEAD_DIM), F32),
                        pltpu.VMEM((rows, HEAD_DIM), F32)],
        compiler_params=_params(3),
        name="flash2",
    )(*args)


def _dilated_kernel(q_ref, k_ref, v_ref, tab_ref, o_ref, lse_ref, m_ref, l_ref, acc_ref, *, t, dil, n_q):
    def rows(r, tile_idx):
        return pl.ds(r + tile_idx * (t * dil), t, stride=dil)

    def lane_pair(op, x):
        return op(x[:, :HEAD_DIM], x[:, HEAD_DIM:])

    def one(idx):
        r = idx // n_q
        qi = idx % n_q
        q = q_ref[0, rows(r, qi), :].astype(BF16)
        tiles = [(qi, 0, None)]
        if n_q > 1:
            tiles.append((jnp.maximum(qi - 1, 0), 1, jnp.where(qi >= 1, 0.0, NEG_INF)))
        scores, values = [], []
        for kj, tab, extra in tiles:
            k = k_ref[0, rows(r, kj), :].astype(BF16)
            values.append(v_ref[0, rows(r, kj), :].astype(BF16))
            s = _dot(q, k, _NT) * SOFTMAX_SCALE + tab_ref[0, tab]
            scores.append(s if extra is None else s + extra)
        m_lane = lane_pair(jnp.maximum, scores[0])
        for s in scores[1:]:
            m_lane = jnp.maximum(m_lane, lane_pair(jnp.maximum, s))
        m = jnp.max(m_lane, axis=1, keepdims=True)
        l_lane, acc = None, None
        for s, v in zip(scores, values):
            p = jnp.exp(s - m)
            pl_ = lane_pair(jnp.add, p)
            pv = _dot(p.astype(BF16), v)
            l_lane = pl_ if l_lane is None else l_lane + pl_
            acc = pv if acc is None else acc + pv
        l = jnp.sum(l_lane, axis=1, keepdims=True)
        o_ref[0, rows(r, qi), :] = acc / l
        lse_ref[0, rows(r, qi), :] = jnp.broadcast_to(m + jnp.log(l), (t, HEAD_DIM))

    n_it = dil * n_q
    chains = 4 if n_it % 4 == 0 else 1

    def body(it, carry):
        for u in range(chains):
            one(it * chains + u)
        return carry

    lax.fori_loop(0, n_it // chains, body, 0)


def _dilated_attention(proj, tabs, *, n_heads, q_col, k_col, v_col, dil):
    b, s, _ = proj.shape
    t = ATTN_TILE
    assert t == 2 * HEAD_DIM and (s // dil) % t == 0
    spec = lambda col: pl.BlockSpec((1, s, HEAD_DIM), lambda bi, h: (bi, 0, col + h))
    out_sds = jax.ShapeDtypeStruct((b, s, n_heads * HEAD_DIM), F32)
    return pl.pallas_call(
        functools.partial(_dilated_kernel, t=t, dil=dil, n_q=s // dil // t),
        grid=(b, n_heads),
        in_specs=[spec(q_col), spec(k_col), spec(v_col),
                  pl.BlockSpec((1, 2, t, t), lambda bi, h: (h, 0, 0, 0))],
        out_specs=[spec(0), spec(0)],
        out_shape=[out_sds, out_sds],
        scratch_shapes=[pltpu.VMEM((t, HEAD_DIM), F32)] * 3,
        compiler_params=_params(2),
        name="dilated_attention",
    )(proj, proj, proj, tabs)


def _moba_select_kernel(q_ref, k_ref, sel_ref, kmean_ref, *, group, t, n_blk):
    qi = pl.program_id(2)

    @pl.when(qi == 0)
    def _():
        kmean_ref[...] = jnp.zeros_like(kmean_ref)
        s = k_ref.shape[1]
        km = jnp.sum(k_ref[0].reshape(n_blk, s // n_blk, HEAD_DIM), axis=1) * (1.0 / B_BLOCK)
        kmean_ref[0:n_blk, :] = km

    lane = lax.broadcasted_iota(jnp.int32, (t, HEAD_DIM), 1)
    pos = qi * t + lax.broadcasted_iota(jnp.int32, (t, HEAD_DIM), 0)
    own = pos // B_BLOCK
    past = lane < own
    kmean = kmean_ref[...]
    for j in range(group):
        q = q_ref[0, :, j * HEAD_DIM:(j + 1) * HEAD_DIM]
        gate = _dot_f32_both(q, kmean, _NT)
        gate = jnp.where(past, gate, NEG_INF)
        rank = _rank_lanes(gate, n_blk)
        sel = (past & (rank < B_TOPK)) | (lane == own)
        sel_ref[0, j] = jnp.where(sel, 0.0, NEG_INF)


def _moba_select(proj, *, n_kv, group, k_col):
    b, s, _ = proj.shape
    t = ATTN_TILE
    n_blk = s // B_BLOCK
    return pl.pallas_call(
        functools.partial(_moba_select_kernel, group=group, t=t, n_blk=n_blk),
        grid=(b, n_kv, s // t),
        in_specs=[pl.BlockSpec((1, t, group * HEAD_DIM), lambda bi, h, qi: (bi, qi, h)),
                  pl.BlockSpec((1, s, HEAD_DIM), lambda bi, h, qi: (bi, 0, k_col + h))],
        out_specs=pl.BlockSpec((1, group, t, HEAD_DIM), lambda bi, h, qi: (bi, h, qi, 0)),
        out_shape=jax.ShapeDtypeStruct((b, n_kv * group, s, HEAD_DIM), F32),
        scratch_shapes=[pltpu.VMEM((HEAD_DIM, HEAD_DIM), F32)],
        compiler_params=_params(3),
        name="moba_select",
    )(proj, proj)


def _cmp1_prompt_kernel(x_ref, w_ref, y_ref, *, n_seg):
    x2 = jnp.concatenate([x_ref[pl.ds(s, n_seg, stride=C_CMP_STRIDE), :] for s in range(C_CMP_STRIDE)], axis=1)
    y_ref[...] = _dot(x2.astype(BF16), w_ref[...])


def _cmp1_prompt(proj, w1cat, *, col0):
    b, s, _ = proj.shape
    n_seg = s // C_CMP_STRIDE
    return pl.pallas_call(
        functools.partial(_cmp1_prompt_kernel, n_seg=n_seg),
        grid=(b, 2, C_KV_HEADS),
        in_specs=[pl.BlockSpec((None, s, HEAD_DIM), lambda bi, c, g: (bi, 0, col0 + c * C_KV_HEADS + g)),
                  pl.BlockSpec((None, C_CMP_STRIDE * HEAD_DIM, 2 * HEAD_DIM), lambda bi, c, g: (c, 0, 0))],
        out_specs=pl.BlockSpec((None, n_seg, 2 * HEAD_DIM), lambda bi, c, g: ((bi * 2 + c) * C_KV_HEADS + g, 0, 0)),
        out_shape=jax.ShapeDtypeStruct((b * 2 * C_KV_HEADS, n_seg, 2 * HEAD_DIM), F32),
        compiler_params=_params(3),
        name="cmp1_prompt",
    )(proj, w1cat)


def _cmp2_kernel(y_ref, pe_ref, w1_ref, w2_ref, o_ref, *, n_seg):
    y = y_ref[0]
    first = y[:, :HEAD_DIM]
    second = pltpu.roll(y[:, HEAD_DIM:], n_seg - 1, 0)
    pe_term = _dot(pe_ref[0].astype(BF16), w1_ref[0].astype(BF16))[0:1]
    pre = first + second + pe_term
    tok = _dot(jnp.maximum(pre, 0.0).astype(BF16), w2_ref[0].astype(BF16))
    row = lax.broadcasted_iota(jnp.int32, tok.shape, 0)
    o_ref[0] = jnp.where(row < n_seg - 1, tok, 0.0)


def _cmp2(y, pe8, w1, w2):
    r, n_seg, _ = y.shape
    comp = lambda i: ((i // C_KV_HEADS) % 2, 0, 0)
    return pl.pallas_call(
        functools.partial(_cmp2_kernel, n_seg=n_seg),
        grid=(r,),
        in_specs=[pl.BlockSpec((1, n_seg, 2 * HEAD_DIM), lambda i: (i, 0, 0)),
                  pl.BlockSpec((1, 8, C_CMP_BLOCK * HEAD_DIM), comp),
                  pl.BlockSpec((1, C_CMP_BLOCK * HEAD_DIM, HEAD_DIM), comp),
                  pl.BlockSpec((1, HEAD_DIM, HEAD_DIM), comp)],
        out_specs=pl.BlockSpec((1, n_seg, HEAD_DIM), lambda i: (i, 0, 0)),
        out_shape=jax.ShapeDtypeStruct((r, n_seg, HEAD_DIM), F32),
        compiler_params=_params(1),
        name="cmp2",
    )(y, pe8, w1, w2)


CMP_BAND = 8


def _nsa_cmp_select_kernel(q_ref, kc_ref, vc_ref, bias_ref, far_ref, map_ref, oc_ref, sel_ref, *,
                           group, t, n_cmp, n_sel):
    kvh = pl.program_id(1)
    qi = pl.program_id(2)
    n_pad = kc_ref.shape[1]
    pos = qi * t + lax.broadcasted_iota(jnp.int32, (t, n_pad), 0)
    tok = lax.broadcasted_iota(jnp.int32, (t, n_pad), 1)
    valid = (pos - (tok * C_CMP_STRIDE + (C_CMP_BLOCK - 1)) >= 0) & (tok < n_cmp)
    band0 = qi * (t // C_CMP_STRIDE) - CMP_BAND
    shift = jnp.where(band0 < 0, band0 + n_pad, band0)
    is_far = tok < band0
    kc = kc_ref[0].astype(BF16)
    vc = vc_ref[0].astype(BF16)
    p_sum = jnp.zeros((t, n_pad), F32)
    for j in range(group):
        q = q_ref[0, :, j * HEAD_DIM:(j + 1) * HEAD_DIM].astype(BF16)
        bias = jnp.where(is_far, far_ref[kvh * group + j], pltpu.roll(bias_ref[j], shift, 1))
        s = _dot(q, kc, _NT) * SOFTMAX_SCALE + bias
        s = jnp.where(valid, s, NEG_INF)
        m = jnp.max(s, axis=1, keepdims=True)
        e = jnp.where(valid, jnp.exp(s - m), 0.0)
        p = e / jnp.maximum(jnp.sum(e, axis=1, keepdims=True), 1e-30)
        oc_ref[0, :, j * HEAD_DIM:(j + 1) * HEAD_DIM] = _dot(p.astype(BF16), vc)
        p_sum = p_sum + p
    imp = _dot_f32_lhs(p_sum, map_ref[...])
    blk = lax.broadcasted_iota(jnp.int32, (t, HEAD_DIM), 1)
    cur = (qi * t + lax.broadcasted_iota(jnp.int32, (t, HEAD_DIM), 0)) // C_SLC_BLOCK
    avail = (blk <= cur) & (blk < n_sel)
    forced = (blk == 0) | (blk == cur) | (blk == cur - 1)
    score = jnp.where(forced, jnp.inf, imp)
    score = jnp.where(avail, score, -jnp.inf)
    rank = _rank_lanes(score, n_sel)
    sel = avail & (rank < min(C_SLC_TOPK, n_sel))
    sel_ref[0, 0] = jnp.where(sel, 0.0, NEG_INF)


def _cmp_bias_table(rel_bias, s, n_seg):
    assert s == n_seg * C_CMP_STRIDE
    h = rel_bias.shape[1]
    k = np.arange(2 * n_seg + 1)
    e = np.where(k <= n_seg, -k, 2 * n_seg + 1 - k)
    dist = C_CMP_STRIDE * e[None, :] + np.arange(C_CMP_STRIDE)[:, None] - (C_CMP_BLOCK - 1)
    w = jnp.transpose(rel_bias[jnp.asarray(_bucket_np(dist))], (2, 0, 1))
    tab = _toeplitz(w.reshape(h * C_CMP_STRIDE, -1), n_seg).reshape(h, C_CMP_STRIDE, n_seg, n_seg)
    return jnp.transpose(tab, (0, 2, 1, 3)).reshape(h, s, n_seg)


def _cmp_map_np(n_rows, n_cmp, n_cols):
    ratio = C_SLC_BLOCK // C_CMP_STRIDE
    m = np.zeros((n_rows, n_cols), np.float32)
    for i in range(n_cmp):
        for r in range(C_CMP_BLOCK // C_CMP_STRIDE):
            m[i, (i + r) // ratio] += 1.0
    return m


def _nsa_cmp_select(proj, kvc, rel_bias, far, *, n_kv, group):
    b, s, _ = proj.shape
    t = ATTN_TILE
    n_seg = kvc.shape[1]
    n_cmp = n_seg - 1
    n_sel = -(-s // C_SLC_BLOCK)
    cmap = jnp.asarray(_cmp_map_np(n_seg, n_cmp, HEAD_DIM), BF16)
    gw = group * HEAD_DIM
    n_band = CMP_BAND + 2 * (t // C_CMP_STRIDE) - (C_CMP_BLOCK - 1) // C_CMP_STRIDE
    assert n_band <= n_seg and C_CMP_STRIDE * (CMP_BAND + 1) - (C_CMP_BLOCK - 1) >= 113
    dist = np.arange(t)[:, None] - C_CMP_STRIDE * (np.arange(n_band)[None, :] - CMP_BAND) - (C_CMP_BLOCK - 1)
    bias_c = jnp.pad(_bias_by_dist(rel_bias, np.maximum(dist, 0)), ((0, 0), (0, 0), (0, n_seg - n_band)))
    return pl.pallas_call(
        functools.partial(_nsa_cmp_select_kernel, group=group, t=t, n_cmp=n_cmp, n_sel=n_sel),
        grid=(b, n_kv, s // t),
        in_specs=[pl.BlockSpec((1, t, gw), lambda bi, h, qi: (bi, qi, h)),
                  pl.BlockSpec((1, n_seg, HEAD_DIM), lambda bi, h, qi: (bi * 8 + h, 0, 0)),
                  pl.BlockSpec((1, n_seg, HEAD_DIM), lambda bi, h, qi: (bi * 8 + 4 + h, 0, 0)),
                  pl.BlockSpec((group, t, n_seg), lambda bi, h, qi: (h, 0, 0)),
                  pl.BlockSpec(memory_space=pltpu.SMEM),
                  pl.BlockSpec((n_seg, HEAD_DIM), lambda bi, h, qi: (0, 0))],
        out_specs=[pl.BlockSpec((1, t, gw), lambda bi, h, qi: (bi, qi, h)),
                   pl.BlockSpec((1, 1, t, HEAD_DIM), lambda bi, h, qi: (bi, h, qi, 0))],
        out_shape=[jax.ShapeDtypeStruct((b, s, n_kv * gw), F32),
                   jax.ShapeDtypeStruct((b, n_kv, s, HEAD_DIM), F32)],
        compiler_params=_params(3),
        name="nsa_cmp_select",
    )(proj, kvc, kvc, bias_c, far, cmap)


def _row_tile(m, pref):
    return pref if m % pref == 0 else m


def _mixer_a_prompt(x, xs, b, g_mix, li, w_in, w_out, slot, tabs, far):
    m = x.shape[0]
    s = m // b
    n_in = w_in.shape[2]
    proj, proj_s = _norm_matmul(x, g_mix, li, w_in, slot, n_in, _row_tile(m, PROMPT_ROW_TILE), 512, xs=xs)
    proj3 = proj.reshape(b, s, n_in)
    outs, lses = [], []
    for g, (window, dil) in enumerate(A_PATTERNS):
        c0 = g * 3 * A_HEADS
        o, lse = _dilated_attention(proj3, tabs[g], n_heads=A_HEADS, q_col=c0, k_col=c0 + A_HEADS,
                                    v_col=c0 + 2 * A_HEADS, dil=dil)
        outs.append(o.reshape(m, -1))
        lses.append(lse.reshape(m, -1))
    x = _outproj(outs, lses, w_out, slot, x, "lse3", _row_tile(m, 512), 512)
    return x, proj3, proj_s


def _mixer_b_prompt(x, xs, b, g_mix, li, w_in, w_out, slot, tabs, far):
    m = x.shape[0]
    s = m // b
    n_in = w_in.shape[2]
    proj, proj_s = _norm_matmul(x, g_mix, li, w_in, slot, n_in, _row_tile(m, PROMPT_ROW_TILE), 512, xs=xs)
    proj3 = proj.reshape(b, s, n_in)
    sel = _moba_select(proj3, n_kv=B_KV_HEADS, group=GQA_GROUP, k_col=B_HEADS)
    o = _flash2(proj3, tabs, far, sel, n_kv=B_KV_HEADS, group=GQA_GROUP, q_col=0, k_col=B_HEADS,
               v_col=B_HEADS + B_KV_HEADS, sel_blk=B_BLOCK, sel_per_head=True)
    x = _outproj([o.reshape(m, -1)], [], w_out, slot, x, "plain", _row_tile(m, 512), 512)
    return x, proj3, proj_s


def _cmp_weights(cmp_pe, cmp_w1, cmp_w2, slot):
    half = C_CMP_STRIDE * HEAD_DIM
    w1 = cmp_w1[slot]
    w1cat = jnp.concatenate([w1[:, :half], w1[:, half:]], axis=2).astype(BF16)
    pe8 = jnp.broadcast_to(cmp_pe[slot].reshape(2, 1, C_CMP_BLOCK * HEAD_DIM), (2, 8, C_CMP_BLOCK * HEAD_DIM))
    return w1cat, pe8, w1, cmp_w2[slot]


def _mixer_c_prompt(x, xs, b, g_mix, li, w_in, w_gate, w_out, slot, cmp_w, tabs, tabs_win, far, rel_bias):
    m = x.shape[0]
    s = m // b
    n_main = (C_HEADS + 6 * C_KV_HEADS) * HEAD_DIM
    proj, proj_s = _norm_matmul(x, g_mix, li, w_in, slot, n_main, _row_tile(m, PROMPT_ROW_TILE), 512, xs=xs)
    gate, gate_s = _norm_matmul(x, g_mix, li, w_gate, 0, HEAD_DIM, _row_tile(m, 512), HEAD_DIM, xs=xs)
    proj3 = proj.reshape(b, s, n_main)
    w1cat, pe8, w1, w2 = cmp_w
    y = _cmp1_prompt(proj3, w1cat, col0=C_HEADS)
    kvc = _cmp2(y, pe8, w1, w2)
    n_seg = kvc.shape[1]
    o_cmp, sel = _nsa_cmp_select(proj3, kvc, rel_bias, far, n_kv=C_KV_HEADS, group=GQA_GROUP)
    kv0 = C_HEADS
    o_slc = _flash2(proj3, tabs, far, sel, n_kv=C_KV_HEADS, group=GQA_GROUP, q_col=0, k_col=kv0 + 2 * C_KV_HEADS,
                   v_col=kv0 + 3 * C_KV_HEADS, sel_blk=C_SLC_BLOCK)
    o_win = _flash2(proj3, tabs_win, far, None, n_kv=C_KV_HEADS, group=GQA_GROUP, q_col=0, k_col=kv0 + 4 * C_KV_HEADS,
                   v_col=kv0 + 5 * C_KV_HEADS, window=C_WINDOW)
    x = _outproj([o_cmp.reshape(m, -1), o_slc.reshape(m, -1), o_win.reshape(m, -1)], [gate], w_out, slot, x,
                 "gate3", _row_tile(m, 512), 512)
    return x, proj3, proj_s, gate_s


def _a_decode_kernel(qkv_ref, buf0_ref, buf1_ref, buf2_ref, bias_ref, o_ref):
    outs, lses = [], []
    for g, buf in enumerate((buf0_ref, buf1_ref, buf2_ref)):
        q, k_new, v_new = qkv_ref[0, g, 0], qkv_ref[0, g, 1], qkv_ref[0, g, 2]
        kb, vb = buf[:, 0], buf[:, 1]
        n = kb.shape[0]
        s = jnp.sum(q[None] * kb, axis=-1, keepdims=True) * SOFTMAX_SCALE + bias_ref[g, 0:n]
        s_new = jnp.sum(q * k_new, axis=-1, keepdims=True) * SOFTMAX_SCALE + bias_ref[g, n]
        m = jnp.maximum(jnp.max(s, axis=0), s_new)
        e = jnp.exp(s - m[None])
        e_new = jnp.exp(s_new - m)
        l = jnp.sum(e, axis=0) + e_new
        outs.append((jnp.sum(e * vb, axis=0) + e_new * v_new) / l)
        lses.append(m + jnp.log(l))
    mx = jnp.maximum(jnp.maximum(lses[0], lses[1]), lses[2])
    w = [jnp.exp(l - mx) for l in lses]
    o_ref[0] = (w[0] * outs[0] + w[1] * outs[1] + w[2] * outs[2]) / (w[0] + w[1] + w[2])


def _a_decode(qkv, bufs, slot, bias):
    n_dec = qkv.shape[0]
    in_specs = [pl.BlockSpec((1, 3, 3, A_HEADS, HEAD_DIM), lambda i: (i, 0, 0, 0, 0))]
    views = []
    for (window, dil), buf in zip(A_PATTERNS, bufs):
        n = window // dil
        assert buf.shape[2] == window, "window buffers must hold exactly `window` past rows"
        views.append(buf.reshape(buf.shape[0], n_dec, n, dil, 2, A_HEADS, HEAD_DIM))
        in_specs.append(pl.BlockSpec((None, None, n, None, 2, A_HEADS, HEAD_DIM),
                                     lambda i: (slot, i, 0, 0, 0, 0, 0)))
    in_specs.append(pl.BlockSpec(bias.shape, lambda i: (0, 0, 0, 0)))
    return pl.pallas_call(
        _a_decode_kernel,
        grid=(n_dec,),
        in_specs=in_specs,
        out_specs=pl.BlockSpec((1, A_HEADS, HEAD_DIM), lambda i: (i, 0, 0)),
        out_shape=jax.ShapeDtypeStruct((n_dec, A_HEADS, HEAD_DIM), F32),
        compiler_params=_params(1),
        name="a_decode",
    )(qkv, *views, bias)


DECODE_CHUNK = 2048


def _head_rows_store(o_ref, o_all, l_row):
    l_col = jnp.broadcast_to(l_row, (HEAD_DIM, HEAD_DIM)).T
    for g in range(B_KV_HEADS):
        rows = slice(g * GQA_GROUP, (g + 1) * GQA_GROUP)
        o_ref[0, rows, :] = o_all[rows, g * HEAD_DIM:(g + 1) * HEAD_DIM] / l_col[rows, :]


def _paged_decode_kernel(pt_ref, *refs, n_pg, n_sub, k_sub, v_sub, blk, topk, n_pages):
    page_refs = refs[:n_pg]
    qbd_ref, new_ref, bias_ref = refs[n_pg:n_pg + 3]
    rest = refs[n_pg + 3:]
    if topk is None:
        sel_ref, o_ref, st_ref, vs_ref, sel_s = rest
    else:
        o_ref, st_ref, vs_ref, sel_s = rest
    j = pl.program_id(1)
    past = n_pages * PAGE_SIZE
    n_blk = past // blk
    ch = DECODE_CHUNK

    if topk is not None:
        @pl.when(j == 0)
        def _():
            sel_s[...] = jnp.zeros_like(sel_s)

    for p in range(n_pg):
        page = j * n_pg + p
        row0 = pl.multiple_of(page * PAGE_SIZE, PAGE_SIZE)
        s_acc = jnp.zeros((PAGE_SIZE, HEAD_DIM), F32)
        for g in range(B_KV_HEADS):
            kg = page_refs[p][pl.ds(k_sub + g, PAGE_SIZE, stride=n_sub), :].astype(BF16)
            s_acc = s_acc + _dot(kg, qbd_ref[0, g])
            vg = page_refs[p][pl.ds(v_sub + g, PAGE_SIZE, stride=n_sub), :].astype(BF16)
            vs_ref[pl.ds(row0, PAGE_SIZE), g * HEAD_DIM:(g + 1) * HEAD_DIM] = vg
        st_ref[pl.ds(row0, PAGE_SIZE), :] = s_acc
        if topk is not None:
            b_idx = page // (blk // PAGE_SIZE)
            sel_s[pl.ds(b_idx, 1), :] = sel_s[pl.ds(b_idx, 1), :] + jnp.sum(s_acc, axis=0, keepdims=True)

    @pl.when(j == pl.num_programs(1) - 1)
    def _():
        s_new = jnp.zeros((8, HEAD_DIM), F32)
        for g in range(B_KV_HEADS):
            k_new = jnp.broadcast_to(new_ref[0, g:g + 1, :], (8, HEAD_DIM)).astype(BF16)
            s_new = s_new + _dot(k_new, qbd_ref[0, g])
            v_new = jnp.broadcast_to(new_ref[0, B_KV_HEADS + g:B_KV_HEADS + g + 1, :], (8, HEAD_DIM))
            vs_ref[past:past + 8, g * HEAD_DIM:(g + 1) * HEAD_DIM] = v_new.astype(BF16)
        l_new = s_new[0:1] * SOFTMAX_SCALE + bias_ref[ch + 1:ch + 2, :]

        if topk is not None:
            gate = sel_s[...] * (1.0 / blk)
            rank = _rank_rows(gate, n_blk)
            sel_s[...] = jnp.where(rank < topk, 1.0, 0.0)
        else:
            sel_s[...] = sel_ref[0, 0:n_blk, :]

        def masked_logits(c, bias):
            start = pl.multiple_of(c * ch, ch)
            lg = st_ref[pl.ds(start, ch), :] * SOFTMAX_SCALE + bias
            b0 = pl.multiple_of(c * (ch // blk), 8)
            sl = sel_s[pl.ds(b0, ch // blk), :]
            mk = jnp.broadcast_to(sl[:, None, :], (ch // blk, blk, HEAD_DIM)).reshape(ch, HEAD_DIM) > 0.5
            return jnp.where(mk, lg, NEG_INF), mk, start

        n_ch = past // ch
        far = bias_ref[ch:ch + 1, :]
        tail = bias_ref[0:ch, :]

        def max_body(c, m):
            lg, _, _ = masked_logits(c, far)
            return jnp.maximum(m, jnp.max(lg, axis=0, keepdims=True))

        m = lax.fori_loop(0, n_ch - 1, max_body, l_new)
        lg_t, mk_t, start_t = masked_logits(n_ch - 1, tail)
        m = jnp.maximum(m, jnp.max(lg_t, axis=0, keepdims=True))

        def acc_chunk(lg, mk, start, carry):
            l_sum, o_all = carry
            e = jnp.where(mk, jnp.exp(lg - m), 0.0)
            o_all = o_all + _dot(e.astype(BF16), vs_ref[pl.ds(start, ch), :], _TN)
            return l_sum + jnp.sum(e, axis=0, keepdims=True), o_all

        def acc_body(c, carry):
            lg, mk, start = masked_logits(c, far)
            return acc_chunk(lg, mk, start, carry)

        e_new = jnp.exp(l_new - m)
        row8 = lax.broadcasted_iota(jnp.int32, (8, HEAD_DIM), 0)
        e_new8 = jnp.where(row8 == 0, jnp.broadcast_to(e_new, (8, HEAD_DIM)), 0.0)
        o0 = _dot(e_new8.astype(BF16), vs_ref[past:past + 8, :], _TN)
        carry = lax.fori_loop(0, n_ch - 1, acc_body, (e_new, o0))
        l_sum, o_all = acc_chunk(lg_t, mk_t, start_t, carry)
        _head_rows_store(o_ref, o_all, l_sum)


def _paged_decode(page_table, pool2d, qbd, new_kv, bias, sel, *, n_pg, n_sub, k_sub, v_sub, blk, topk):
    n_dec, n_pages = page_table.shape
    past = n_pages * PAGE_SIZE
    assert past % DECODE_CHUNK == 0 and n_pages % n_pg == 0 and past % blk == 0
    n_blk = past // blk
    page_rows = PAGE_SIZE * n_sub
    in_specs = [pl.BlockSpec((page_rows, HEAD_DIM), functools.partial(lambda n, j, pt, p: (pt[n, j * n_pg + p], 0), p=p))
                for p in range(n_pg)]
    in_specs += [pl.BlockSpec((1, B_KV_HEADS, HEAD_DIM, HEAD_DIM), lambda n, j, pt: (n, 0, 0, 0)),
                 pl.BlockSpec((1, 2 * B_KV_HEADS, HEAD_DIM), lambda n, j, pt: (n, 0, 0)),
                 pl.BlockSpec(bias.shape, lambda n, j, pt: (0, 0))]
    args = [pool2d] * n_pg + [qbd, new_kv, bias]
    if topk is None:
        in_specs.append(pl.BlockSpec((1,) + sel.shape[1:], lambda n, j, pt: (n, 0, 0)))
        args.append(sel)
    return pl.pallas_call(
        functools.partial(_paged_decode_kernel, n_pg=n_pg, n_sub=n_sub, k_sub=k_sub, v_sub=v_sub, blk=blk,
                          topk=topk, n_pages=n_pages),
        grid_spec=pltpu.PrefetchScalarGridSpec(
            num_scalar_prefetch=1,
            grid=(n_dec, n_pages // n_pg),
            in_specs=in_specs,
            out_specs=pl.BlockSpec((1, B_HEADS, HEAD_DIM), lambda n, j, pt: (n, 0, 0)),
            scratch_shapes=[pltpu.VMEM((past + 8, HEAD_DIM), F32),
                            pltpu.VMEM((past + 8, B_KV_HEADS * HEAD_DIM), BF16),
                            pltpu.VMEM((n_blk, HEAD_DIM), F32)],
        ),
        out_shape=jax.ShapeDtypeStruct((n_dec, B_HEADS, HEAD_DIM), F32),
        compiler_params=_params(2),
        name="paged_decode",
    )(page_table, *args)


def _block_diag_q(q):
    n, h, d = q.shape
    qt = jnp.pad(jnp.swapaxes(q, 1, 2), ((0, 0), (0, 0), (0, HEAD_DIM - h)))
    grp = (np.arange(HEAD_DIM)[None, :] // GQA_GROUP == np.arange(B_KV_HEADS)[:, None]) & (np.arange(HEAD_DIM)[None, :] < h)
    return jnp.where(jnp.asarray(grp)[None, :, None, :], qt[:, None], 0.0).astype(BF16)


def _decode_bias(rel_bias, past, n_tail):
    dist = np.concatenate([n_tail - np.arange(n_tail), [MAX_DISTANCE * 8, 0], np.zeros(6, np.int64)])
    tab = rel_bias[jnp.asarray(_bucket_np(dist))]
    return jnp.pad(tab, ((0, 0), (0, HEAD_DIM - tab.shape[1])))


def _cmp1_paged_kernel(pt_ref, *refs, n_pg):
    page_refs = refs[:n_pg]
    w_ref, y_ref, ys_ref = refs[n_pg:]
    segs_per_page = PAGE_SIZE // C_CMP_STRIDE
    rows = segs_per_page * 8
    x2 = jnp.concatenate(
        [jnp.concatenate([ref[:, s, :, :].reshape(rows, HEAD_DIM) for s in range(C_CMP_STRIDE)], axis=1)
         for ref in page_refs], axis=0)
    y4 = _dot(x2.astype(BF16), w_ref[...])
    is_v = (lax.broadcasted_iota(jnp.int32, (x2.shape[0], 2 * HEAD_DIM), 0) % 8) >= C_KV_HEADS
    y = jnp.where(is_v, y4[:, 2 * HEAD_DIM:], y4[:, :2 * HEAD_DIM])
    ys_ref[0] = y[:, :HEAD_DIM]
    ys_ref[1] = y[:, HEAD_DIM:]
    for sub in range(8):
        y_ref[sub] = jnp.concatenate(
            [ys_ref[half, pl.ds(sub, n_pg * segs_per_page, stride=8), :] for half in range(2)], axis=1)


def _cmp1_paged(page_table, pool4d, w1cat4, *, sub_blk, n_pg):
    n_dec, n_pages = page_table.shape
    segs_per_page = PAGE_SIZE // C_CMP_STRIDE
    n_seg = n_pages * segs_per_page
    in_specs = [pl.BlockSpec((segs_per_page, C_CMP_STRIDE, 8, HEAD_DIM),
                             functools.partial(lambda n, j, pt, p: (pt[n, j * n_pg + p], 0, sub_blk, 0), p=p))
                for p in range(n_pg)]
    in_specs.append(pl.BlockSpec(w1cat4.shape, lambda n, j, pt: (0, 0)))
    return pl.pallas_call(
        functools.partial(_cmp1_paged_kernel, n_pg=n_pg),
        grid_spec=pltpu.PrefetchScalarGridSpec(
            num_scalar_prefetch=1,
            grid=(n_dec, n_pages // n_pg),
            in_specs=in_specs,
            out_specs=pl.BlockSpec((8, n_pg * segs_per_page, 2 * HEAD_DIM), lambda n, j, pt: (n, j, 0)),
            scratch_shapes=[pltpu.VMEM((2, n_pg * segs_per_page * 8, HEAD_DIM), F32)],
        ),
        out_shape=jax.ShapeDtypeStruct((n_dec * 8, n_seg, 2 * HEAD_DIM), F32),
        compiler_params=_params(2),
        name="cmp1_paged",
    )(page_table, *([pool4d] * n_pg), w1cat4)


def _nsa_decode_kernel(kvc_ref, qbd_ref, bias_c_ref, map_ref, gsum_ref, win_ref, new_ref, bias_w_ref,
                       oc_ref, ow_ref, sel_ref, *, n_cmp, n_blk, n_sub_w, k_sub_w):
    n_seg = kvc_ref.shape[1]
    sc = jnp.zeros((n_seg, HEAD_DIM), F32)
    for g in range(C_KV_HEADS):
        sc = sc + _dot(kvc_ref[g].astype(BF16), qbd_ref[0, g])
    valid = lax.broadcasted_iota(jnp.int32, (n_seg, HEAD_DIM), 0) < n_cmp
    lg = jnp.where(valid, sc * SOFTMAX_SCALE + bias_c_ref[...], NEG_INF)
    m = jnp.max(lg, axis=0, keepdims=True)
    e = jnp.where(valid, jnp.exp(lg - m), 0.0)
    p = e / jnp.maximum(jnp.sum(e, axis=0, keepdims=True), 1e-30)
    vc = jnp.concatenate([kvc_ref[C_KV_HEADS + g].astype(BF16) for g in range(C_KV_HEADS)], axis=1)
    _head_rows_store(oc_ref, _dot(p.astype(BF16), vc, _TN), jnp.ones((1, HEAD_DIM), F32))
    imp_h = _dot_f32_rhs(map_ref[...], p)
    imp = _dot_f32_lhs(imp_h, gsum_ref[...])
    blk = lax.broadcasted_iota(jnp.int32, imp.shape, 0)
    cur = n_blk - 1
    avail = blk <= cur
    forced = (blk == 0) | (blk == cur) | (blk == cur - 1)
    score = jnp.where(forced, jnp.inf, imp)
    score = jnp.where(avail, score, -jnp.inf)
    rank = _rank_rows(score, n_blk)
    sel_ref[0] = jnp.where(avail & (rank < min(C_SLC_TOPK, n_blk)), 1.0, 0.0)
    n_win = bias_w_ref.shape[0] - 8
    sw = jnp.zeros((n_win, HEAD_DIM), F32)
    s_new = jnp.zeros((8, HEAD_DIM), F32)
    v_parts, vn_parts = [], []
    for g in range(C_KV_HEADS):
        kg = win_ref[pl.ds(k_sub_w + g, n_win, stride=n_sub_w), :].astype(BF16)
        sw = sw + _dot(kg, qbd_ref[0, g])
        v_parts.append(win_ref[pl.ds(k_sub_w + C_KV_HEADS + g, n_win, stride=n_sub_w), :].astype(BF16))
        k_new = jnp.broadcast_to(new_ref[0, g:g + 1, :], (8, HEAD_DIM)).astype(BF16)
        s_new = s_new + _dot(k_new, qbd_ref[0, g])
        vn_parts.append(jnp.broadcast_to(new_ref[0, C_KV_HEADS + g:C_KV_HEADS + g + 1, :], (8, HEAD_DIM)).astype(BF16))
    lw = sw * SOFTMAX_SCALE + bias_w_ref[0:n_win, :]
    l_new = s_new[0:1] * SOFTMAX_SCALE + bias_w_ref[n_win + 1:n_win + 2, :]
    mw = jnp.maximum(jnp.max(lw, axis=0, keepdims=True), l_new)
    ew = jnp.exp(lw - mw)
    e_new = jnp.exp(l_new - mw)
    row8 = lax.broadcasted_iota(jnp.int32, (8, HEAD_DIM), 0)
    e_new8 = jnp.where(row8 == 0, jnp.broadcast_to(e_new, (8, HEAD_DIM)), 0.0)
    o_all = _dot(ew.astype(BF16), jnp.concatenate(v_parts, axis=1), _TN)
    o_all = o_all + _dot(e_new8.astype(BF16), jnp.concatenate(vn_parts, axis=1), _TN)
    _head_rows_store(ow_ref, o_all, jnp.sum(ew, axis=0, keepdims=True) + e_new)


def _nsa_decode(kvc, qbd, bias_c, cmap_t, gsum, win2d, new_win, bias_w, *, slot, n_blk):
    n_dec = qbd.shape[0]
    n_seg = kvc.shape[1]
    n_win = bias_w.shape[0] - 8
    n_blk_pad = cmap_t.shape[0]
    win_rows = n_win * 8
    o_sds = jax.ShapeDtypeStruct((n_dec, C_HEADS, HEAD_DIM), F32)
    o_spec = pl.BlockSpec((1, C_HEADS, HEAD_DIM), lambda n: (n, 0, 0))
    return pl.pallas_call(
        functools.partial(_nsa_decode_kernel, n_cmp=n_seg - 1, n_blk=n_blk, n_sub_w=8, k_sub_w=0),
        grid=(n_dec,),
        in_specs=[pl.BlockSpec((8, n_seg, HEAD_DIM), lambda n: (n, 0, 0)),
                  pl.BlockSpec((1, C_KV_HEADS, HEAD_DIM, HEAD_DIM), lambda n: (n, 0, 0, 0)),
                  pl.BlockSpec(bias_c.shape, lambda n: (0, 0)),
                  pl.BlockSpec(cmap_t.shape, lambda n: (0, 0)),
                  pl.BlockSpec(gsum.shape, lambda n: (0, 0)),
                  pl.BlockSpec((win_rows, HEAD_DIM), lambda n: (slot * n_dec + n, 0)),
                  pl.BlockSpec((1, 2 * C_KV_HEADS, HEAD_DIM), lambda n: (n, 0, 0)),
                  pl.BlockSpec(bias_w.shape, lambda n: (0, 0))],
        out_specs=[o_spec, o_spec, pl.BlockSpec((1, n_blk_pad, HEAD_DIM), lambda n: (n, 0, 0))],
        out_shape=[o_sds, o_sds, jax.ShapeDtypeStruct((n_dec, n_blk_pad, HEAD_DIM), F32)],
        compiler_params=_params(1),
        name="nsa_decode",
    )(kvc, qbd, bias_c, cmap_t, gsum, win2d, new_win, bias_w)


def _mixer_a_decode(x, proj, w_out, slot, bufs, rel_bias):
    n_dec = x.shape[0]
    qkv = proj.reshape(n_dec, len(A_PATTERNS), 3, A_HEADS, HEAD_DIM)
    dist = np.stack([np.concatenate([(w // d - np.arange(w // d)) * d, [0]]) for w, d in A_PATTERNS])
    bias = _bias_by_dist(rel_bias[:, :A_HEADS], dist)
    bias = jnp.broadcast_to(jnp.transpose(bias, (1, 2, 0))[..., None], dist.shape + (A_HEADS, HEAD_DIM))
    o = _a_decode(qkv, bufs, slot, bias)
    return _outproj([o.reshape(n_dec, -1)], [], w_out, slot, x, "plain", n_dec, 512)


def _mixer_b_decode(x, proj, w_out, slot, pool, page_table, bias_dec):
    n_dec = x.shape[0]
    nq = B_HEADS * HEAD_DIM
    qbd = _block_diag_q(proj[:, :nq].reshape(n_dec, B_HEADS, HEAD_DIM))
    new_kv = proj[:, nq:].reshape(n_dec, 2 * B_KV_HEADS, HEAD_DIM)
    n_sub = pool.shape[2] * 2 * B_KV_HEADS
    o = _paged_decode(page_table, pool.reshape(-1, HEAD_DIM), qbd, new_kv, bias_dec, None, n_pg=8, n_sub=n_sub,
                      k_sub=slot * 2 * B_KV_HEADS, v_sub=slot * 2 * B_KV_HEADS + B_KV_HEADS, blk=B_BLOCK, topk=B_TOPK)
    return _outproj([o.reshape(n_dec, -1)], [], w_out, slot, x, "plain", n_dec, 512)


def _mixer_c_decode(x, proj, gate, w_out, slot, cmp_w, pool, win_buf, page_table, bias_dec, rel_bias):
    n_dec = x.shape[0]
    n_pages = page_table.shape[1]
    past = n_pages * PAGE_SIZE
    assert win_buf.shape[2] == C_WINDOW, "window buffer must hold exactly C_WINDOW past rows"
    nq = C_HEADS * HEAD_DIM
    kvw = C_KV_HEADS * HEAD_DIM
    qbd = _block_diag_q(proj[:, :nq].reshape(n_dec, C_HEADS, HEAD_DIM))
    new_slc = proj[:, nq + 2 * kvw:nq + 4 * kvw].reshape(n_dec, 2 * C_KV_HEADS, HEAD_DIM)
    new_win = proj[:, nq + 4 * kvw:nq + 6 * kvw].reshape(n_dec, 2 * C_KV_HEADS, HEAD_DIM)
    w1cat, pe8, w1, w2 = cmp_w
    w1cat4 = jnp.concatenate([w1cat[0], w1cat[1]], axis=1)
    n_layers = pool.shape[2]
    n_sub = n_layers * 4 * C_KV_HEADS
    pool4d = pool.reshape(-1, C_CMP_STRIDE, n_sub, HEAD_DIM)
    y = _cmp1_paged(page_table, pool4d, w1cat4, sub_blk=slot * 2, n_pg=8)
    kvc = _cmp2(y, pe8, w1, w2)
    n_seg = kvc.shape[1]
    n_cmp = n_seg - 1
    n_blk = past // C_SLC_BLOCK + 1
    n_blk_pad = -(-n_blk // 8) * 8
    dist_c = past - (np.arange(n_seg) * C_CMP_STRIDE + (C_CMP_BLOCK - 1))
    bias_c = jnp.pad(rel_bias[jnp.asarray(_bucket_np(dist_c))], ((0, 0), (0, HEAD_DIM - C_HEADS)))
    cmap_t = jnp.asarray(_cmp_map_np(n_seg, n_cmp, n_blk_pad).T, BF16)
    lanes = np.arange(HEAD_DIM)
    gsum = jnp.asarray((lanes[:, None] // GQA_GROUP == lanes[None, :] // GQA_GROUP)
                       & (lanes[:, None] < C_HEADS) & (lanes[None, :] < C_HEADS), BF16)
    bias_w = _decode_bias(rel_bias, past, C_WINDOW)
    o_cmp, o_win, sel = _nsa_decode(kvc, qbd, bias_c, cmap_t, gsum, win_buf.reshape(-1, HEAD_DIM), new_win, bias_w,
                                    slot=slot, n_blk=n_blk)
    o_slc = _paged_decode(page_table, pool.reshape(-1, HEAD_DIM), qbd, new_slc, bias_dec, sel, n_pg=4, n_sub=n_sub,
                          k_sub=slot * 16 + 2 * C_KV_HEADS, v_sub=slot * 16 + 3 * C_KV_HEADS, blk=C_SLC_BLOCK, topk=None)
    return _outproj([o_cmp.reshape(n_dec, -1), o_slc.reshape(n_dec, -1), o_win.reshape(n_dec, -1)], [gate], w_out, slot,
                    x, "gate3", n_dec, 512)


def _shift_kernel(buf_ref, new_ref, out_ref, sems, *, n_l, n_seq):
    w = buf_ref.shape[2]

    def copies(l, n):
        i = l * n_seq + n
        return (pltpu.make_async_copy(buf_ref.at[l, n, pl.ds(1, w - 1)], out_ref.at[l, n, pl.ds(0, w - 1)], sems.at[0, i]),
                pltpu.make_async_copy(new_ref.at[l, n], out_ref.at[l, n, pl.ds(w - 1, 1)], sems.at[1, i]))

    pairs = [copies(l, n) for l in range(n_l) for n in range(n_seq)]
    for body, tail in pairs:
        body.start()
        tail.start()
    for body, tail in pairs:
        body.wait()
        tail.wait()


def _shifted(buf, new_rows):
    n_l, n_seq = buf.shape[:2]
    return pl.pallas_call(
        functools.partial(_shift_kernel, n_l=n_l, n_seq=n_seq),
        in_specs=[pl.BlockSpec(memory_space=pl.ANY), pl.BlockSpec(memory_space=pl.ANY)],
        out_specs=pl.BlockSpec(memory_space=pl.ANY),
        out_shape=jax.ShapeDtypeStruct(buf.shape, buf.dtype),
        scratch_shapes=[pltpu.SemaphoreType.DMA((2, n_l * n_seq))],
        name="shift_window",
    )(buf, new_rows[:, :, None])


def kernel(x_prompt, x_sample, state_a_w128, state_a_w512, state_a_w2048, cache_b_kv, cache_c_kv, state_c_win, page_table, rel_bias, norm_mix, norm_ffn, norm_final, w_in_a, w_out_a, w_in_b, w_out_b, w_in_c, w_out_c, cmp_pe, cmp_w1, cmp_w2, w_up, w_down):
    b, s, d = x_prompt.shape
    n_dec, t_dec, _ = x_sample.shape
    depth = norm_mix.shape[0]
    assert t_dec == 1 and s % B_BLOCK == 0 and s % ATTN_TILE == 0
    past = page_table.shape[1] * PAGE_SIZE
    assert past % B_BLOCK == 0 and past % C_SLC_BLOCK == 0

    xp = x_prompt.reshape(b * s, d)
    xs = x_sample.reshape(n_dec, d)
    g_mix = norm_mix.reshape(depth, 1, d)
    g_ffn = norm_ffn.reshape(depth, 1, d)
    g_fin = norm_final.reshape(1, d)
    tabs = _tile_tables(rel_bias, ATTN_TILE)
    tabs_win = _tile_tables(rel_bias, ATTN_TILE, window=C_WINDOW)
    tabs_a = [_tile_tables(rel_bias[:, :A_HEADS], ATTN_TILE, window=w // dl, dist_scale=dl) for w, dl in A_PATTERNS]
    far = rel_bias[NUM_BUCKETS - 1]
    bias_dec = _decode_bias(rel_bias, past, DECODE_CHUNK)
    n_gate = 3 * C_HEADS
    n_main_c = (C_HEADS + 6 * C_KV_HEADS) * HEAD_DIM
    w_gate = jnp.pad(w_in_c[:, :, n_main_c:], ((0, 0), (0, 0), (0, HEAD_DIM - n_gate)))
    a_bufs = (state_a_w128, state_a_w512, state_a_w2048)
    w_in_a, w_in_b, w_in_c = w_in_a.astype(BF16), w_in_b.astype(BF16), w_in_c.astype(BF16)

    a_p, a_s = [[], [], []], [[], [], []]
    b_p, b_s, c_p, c_s, cw_p, cw_s = [], [], [], [], [], []
    for li in range(depth):
        kind, slot = li % 3, li // 3
        if kind == 0:
            xp, pp, ps = _mixer_a_prompt(xp, xs, b, g_mix, li, w_in_a, w_out_a, slot, tabs_a, far)
            xs = _mixer_a_decode(xs, ps, w_out_a, slot, a_bufs, rel_bias)
            gw = 3 * A_HEADS * HEAD_DIM
            kvw = 2 * A_HEADS * HEAD_DIM
            for g, (window, _) in enumerate(A_PATTERNS):
                c0 = g * gw + A_HEADS * HEAD_DIM
                a_p[g].append(pp[:, s - min(window, s):, c0:c0 + kvw].reshape(b, min(window, s), 2, A_HEADS, HEAD_DIM))
                a_s[g].append(ps[:, c0:c0 + kvw].reshape(n_dec, 2, A_HEADS, HEAD_DIM))
        elif kind == 1:
            xp, pp, ps = _mixer_b_prompt(xp, xs, b, g_mix, li, w_in_b, w_out_b, slot, tabs, far)
            xs = _mixer_b_decode(xs, ps, w_out_b, slot, cache_b_kv, page_table, bias_dec)
            nq = B_HEADS * HEAD_DIM
            b_p.append(pp[:, :, nq:].reshape(b, s, 2, B_KV_HEADS, HEAD_DIM))
            b_s.append(ps[:, nq:].reshape(n_dec, 1, 2, B_KV_HEADS, HEAD_DIM))
        else:
            cmp_w = _cmp_weights(cmp_pe, cmp_w1, cmp_w2, slot)
            xp, pp, ps, gate_s = _mixer_c_prompt(xp, xs, b, g_mix, li, w_in_c, w_gate[slot:slot + 1], w_out_c, slot, cmp_w,
                                                 tabs, tabs_win, far, rel_bias)
            xs = _mixer_c_decode(xs, ps, gate_s, w_out_c, slot, cmp_w, cache_c_kv, state_c_win, page_table, bias_dec,
                                 rel_bias)
            nq = C_HEADS * HEAD_DIM
            kvw = C_KV_HEADS * HEAD_DIM
            c_p.append(pp[:, :, nq:nq + 4 * kvw].reshape(b, s, 4, C_KV_HEADS, HEAD_DIM))
            c_s.append(ps[:, nq:nq + 4 * kvw].reshape(n_dec, 1, 4, C_KV_HEADS, HEAD_DIM))
            wlen = min(C_WINDOW, s)
            cw_p.append(pp[:, s - wlen:, nq + 4 * kvw:nq + 6 * kvw].reshape(b, wlen, 2, C_KV_HEADS, HEAD_DIM))
            cw_s.append(ps[:, nq + 4 * kvw:nq + 6 * kvw].reshape(n_dec, 2, C_KV_HEADS, HEAD_DIM))
        last = li == depth - 1
        xp, xs = _mlp(xp, xs, g_ffn, li, w_up, w_down, g_fin if last else None, _row_tile(b * s, PROMPT_ROW_TILE), 512)

    outs = [xp.reshape(b, s, d), xs.reshape(n_dec, 1, d)]
    for g in range(len(A_PATTERNS)):
        outs.append(jnp.stack(a_p[g], axis=0))
        outs.append(_shifted(a_bufs[g], jnp.stack(a_s[g], axis=0)))
    outs += [jnp.stack(b_p, axis=2), jnp.stack(b_s, axis=2), jnp.stack(c_p, axis=2), jnp.stack(c_s, axis=2),
             jnp.stack(cw_p, axis=0), _shifted(state_c_win, jnp.stack(cw_s, axis=0))]
    return tuple(outs)
```

```python
import functools
import math

import jax
import jax.numpy as jnp
import numpy as np
from jax import lax
from jax.experimental import pallas as pl
from jax.experimental.pallas import tpu as pltpu

F32 = jnp.float32
BF16 = jnp.bfloat16

HEAD_DIM = 128
RMS_EPS = 1e-6
NEG_INF = -1e30
SOFTMAX_SCALE = HEAD_DIM ** -0.5
LOG2E = math.log2(math.e)
PAGE_SIZE = 128

NUM_BUCKETS = 32
MAX_DISTANCE = 128

A_PATTERNS = ((128, 1), (512, 4), (2048, 16))
A_HEADS = 8
B_HEADS = 16
B_KV_HEADS = 4
B_BLOCK = 256
B_TOPK = 3
C_HEADS = 16
C_KV_HEADS = 4
C_CMP_BLOCK = 32
C_CMP_STRIDE = 16
C_SLC_BLOCK = 64
C_SLC_TOPK = 16
C_WINDOW = 512
GQA_GROUP = 4

ATTN_TILE = 256
PROMPT_ROW_TILE = 1024
VMEM_LIMIT_BYTES = 56 * 1024 * 1024


def _params(n_axes, vmem=VMEM_LIMIT_BYTES):
    return pltpu.CompilerParams(dimension_semantics=("arbitrary",) * n_axes, vmem_limit_bytes=vmem)


def _bucket_np(dist):
    exact = NUM_BUCKETS // 2
    d = np.maximum(np.asarray(dist, np.int64), 0)
    far = exact + (np.log(np.maximum(d, exact).astype(np.float64) / exact)
                   / math.log(MAX_DISTANCE / exact) * (NUM_BUCKETS - exact)).astype(np.int64)
    return np.where(d < exact, d, np.minimum(far, NUM_BUCKETS - 1)).astype(np.int32)


def _bias_by_dist(rel_bias, dist_np):
    tab = rel_bias[jnp.asarray(_bucket_np(dist_np))]
    return jnp.moveaxis(tab, -1, 0)


def _toeplitz(w, t):
    return jnp.tile(w, (1, t))[:, :2 * t * t].reshape(w.shape[0], t, 2 * t)[:, :, :t]


def _tile_tables(rel_bias, t, window=None, dil=1, dist_scale=1):
    assert window is None or window < t or (window % t == 0 and window >= 2 * t)
    k = np.arange(2 * t + 1)
    e = np.where(k <= t, -k, 2 * t + 1 - k)

    def table(dist, ok):
        vals = rel_bias[jnp.asarray(_bucket_np(dist * dist_scale))].T
        return _toeplitz(jnp.where(jnp.asarray(ok)[None], vals, NEG_INF), t)

    def allowed(dist):
        ok = dist >= 0
        if window is not None:
            ok &= dist <= window
        if dil > 1:
            ok &= dist % dil == 0
        return ok

    tabs = [table(e, allowed(e)), table(t + e, allowed(t + e))]
    if dil > 1:
        tabs.append(table(2 * t + e, (2 * t + e) % dil == 0))
    if window is not None and window >= 2 * t:
        tabs.append(table(window + e, allowed(window + e)))
    return jnp.stack(tabs, axis=1)


def _rms(x, g):
    return x * lax.rsqrt(jnp.mean(x * x, axis=-1, keepdims=True) + RMS_EPS) * g


def _split_bf16(x, parts):
    out = []
    for _ in range(parts):
        p = x.astype(BF16)
        out.append(p)
        x = x - p.astype(F32)
    return out


def _dot(a, b, dims=(((1,), (0,)), ((), ()))):
    return lax.dot_general(a, b, dims, preferred_element_type=F32)


_NT = (((1,), (1,)), ((), ()))
_TN = (((0,), (0,)), ((), ()))


def _dot_f32_lhs(a, b_bf16, dims=(((1,), (0,)), ((), ()))):
    return sum(_dot(p, b_bf16, dims) for p in _split_bf16(a, 3))


def _dot_f32_rhs(a_bf16, b, dims=(((1,), (0,)), ((), ()))):
    return sum(_dot(a_bf16, p, dims) for p in _split_bf16(b, 3))


def _dot_f32_both(a, b, dims):
    a_hi, a_lo = _split_bf16(a, 2)
    b_hi, b_lo = _split_bf16(b, 2)
    return _dot(a_hi, b_hi, dims) + (_dot(a_hi, b_lo, dims) + _dot(a_lo, b_hi, dims))


def _rank_rows(score, n_rows):
    row = lax.broadcasted_iota(jnp.int32, score.shape, 0)
    rank = jnp.zeros(score.shape, jnp.int32)
    for m in range(n_rows):
        sm = score[m:m + 1, :]
        beats = (sm > score) | ((sm == score) & (row > m))
        rank = rank + beats.astype(jnp.int32)
    return rank


def _rank_lanes(score, n_lanes):
    n_rows, width = score.shape
    n_pad = -(-n_lanes // 8) * 8
    rank_t = _rank_rows(score.T[:n_pad], n_lanes).astype(F32)
    if n_pad < width:
        rank_t = jnp.concatenate([rank_t, jnp.zeros((width - n_pad, n_rows), F32)], axis=0)
    return rank_t.T


def _norm_matmul_kernel(x_ref, g_ref, w_ref, o_ref, h_ref):
    @pl.when(pl.program_id(1) == 0)
    def _():
        h_ref[...] = _rms(x_ref[...], g_ref[...]).astype(BF16)

    o_ref[...] = _dot(h_ref[...], w_ref[...].astype(BF16))


def _norm_matmul2_kernel(x_ref, xs_ref, g_ref, w_ref, o_ref, os_ref, h_ref, hs_ref):
    i = pl.program_id(0)
    j = pl.program_id(1)

    @pl.when(j == 0)
    def _():
        h_ref[...] = _rms(x_ref[...], g_ref[...]).astype(BF16)

    @pl.when((i == 0) & (j == 0))
    def _():
        hs_ref[...] = _rms(xs_ref[...], g_ref[...]).astype(BF16)

    w = w_ref[...].astype(BF16)
    o_ref[...] = _dot(h_ref[...], w)

    @pl.when(i == 0)
    def _():
        os_ref[j] = _dot(hs_ref[...], w)


def _norm_matmul(x, g_all, li, w_all, slot, n_cols, tm, tn, xs=None):
    m, k = x.shape
    n_j = n_cols // tn
    x_spec = pl.BlockSpec((tm, k), lambda i, j: (i, 0), pipeline_mode=pl.Buffered(1))
    g_spec = pl.BlockSpec((None, 1, k), lambda i, j: (li, 0, 0))
    w_spec = pl.BlockSpec((None, k, tn), lambda i, j: (slot, 0, j))
    o_spec = pl.BlockSpec((tm, tn), lambda i, j: (i, j))
    if xs is None:
        return pl.pallas_call(
            _norm_matmul_kernel,
            grid=(m // tm, n_j),
            in_specs=[x_spec, g_spec, w_spec],
            out_specs=o_spec,
            out_shape=jax.ShapeDtypeStruct((m, n_cols), F32),
            scratch_shapes=[pltpu.VMEM((tm, k), BF16)],
            compiler_params=_params(2),
            name="norm_matmul",
        )(x, g_all, w_all)
    ms = xs.shape[0]
    out, out_s = pl.pallas_call(
        _norm_matmul2_kernel,
        grid=(m // tm, n_j),
        in_specs=[x_spec, pl.BlockSpec((ms, k), lambda i, j: (0, 0)), g_spec, w_spec],
        out_specs=[o_spec, pl.BlockSpec((n_j, ms, tn), lambda i, j: (0, 0, 0))],
        out_shape=[jax.ShapeDtypeStruct((m, n_cols), F32), jax.ShapeDtypeStruct((n_j, ms, tn), F32)],
        scratch_shapes=[pltpu.VMEM((tm, k), BF16), pltpu.VMEM((ms, k), BF16)],
        compiler_params=_params(2),
        name="norm_matmul2",
    )(x, xs, g_all, w_all)
    return out, jnp.swapaxes(out_s, 0, 1).reshape(ms, n_cols)


def _outproj_kernel(*refs, mode, n_heads):
    if mode == "plain":
        o_refs, extra, (w_ref, x_ref, out_ref, a_ref) = refs[:1], (), refs[1:]
    elif mode == "lse3":
        o_refs, extra, (w_ref, x_ref, out_ref, a_ref) = refs[:3], refs[3:6], refs[6:]
    else:
        o_refs, extra, (w_ref, x_ref, out_ref, a_ref) = refs[:3], refs[3:4], refs[4:]

    @pl.when(pl.program_id(1) == 0)
    def _():
        if mode == "plain":
            a = o_refs[0][...]
        elif mode == "lse3":
            lse = [r[...] for r in extra]
            mx = jnp.maximum(jnp.maximum(lse[0], lse[1]), lse[2])
            e = [jnp.exp(l - mx) for l in lse]
            den = e[0] + e[1] + e[2]
            a = sum((e[g] / den) * o_refs[g][...] for g in range(3))
        else:
            gate = jax.nn.sigmoid(extra[0][...])
            tm = gate.shape[0]
            heads = []
            for h in range(n_heads):
                acc = jnp.zeros((tm, HEAD_DIM), F32)
                for br in range(3):
                    gcol = gate[:, 3 * h + br:3 * h + br + 1]
                    acc = acc + gcol * o_refs[br][:, h * HEAD_DIM:(h + 1) * HEAD_DIM]
                heads.append(acc)
            a = jnp.concatenate(heads, axis=1)
        a_ref[...] = a.astype(BF16)

    out_ref[...] = x_ref[...] + _dot(a_ref[...], w_ref[...].astype(BF16))


def _outproj(o_list, extra_list, w_all, slot, x, mode, tm, tn):
    m, k = o_list[0].shape
    n = x.shape[1]
    row_spec = lambda width: pl.BlockSpec((tm, width), lambda i, j: (i, 0))
    in_specs = [row_spec(k) for _ in o_list] + [row_spec(e.shape[1]) for e in extra_list]
    in_specs += [pl.BlockSpec((None, k, tn), lambda i, j: (slot, 0, j)),
                 pl.BlockSpec((tm, tn), lambda i, j: (i, j))]
    return pl.pallas_call(
        functools.partial(_outproj_kernel, mode=mode, n_heads=k // HEAD_DIM),
        grid=(m // tm, n // tn),
        in_specs=in_specs,
        out_specs=pl.BlockSpec((tm, tn), lambda i, j: (i, j)),
        out_shape=jax.ShapeDtypeStruct((m, n), F32),
        scratch_shapes=[pltpu.VMEM((tm, k), BF16)],
        compiler_params=_params(2),
        name="outproj_" + mode,
    )(*o_list, *extra_list, w_all, x)


def _mlp_kernel(*refs, final):
    if final:
        x_ref, xs_ref, g_ref, wu_ref, wd_ref, gf_ref, o_ref, os_ref, h_ref, hs_ref = refs
    else:
        x_ref, xs_ref, g_ref, wu_ref, wd_ref, o_ref, os_ref, h_ref, hs_ref = refs
    i = pl.program_id(0)
    f = pl.program_id(1)
    last = pl.num_programs(1) - 1

    @pl.when(f == 0)
    def _():
        x = x_ref[...]
        h_ref[...] = _rms(x, g_ref[...]).astype(BF16)
        o_ref[...] = x

    @pl.when((i == 0) & (f == 0))
    def _():
        xs = xs_ref[...]
        hs_ref[...] = _rms(xs, g_ref[...]).astype(BF16)
        os_ref[...] = xs

    wu = wu_ref[...].astype(BF16)
    wd = wd_ref[...].astype(BF16)

    def up_act_down(h):
        u = _dot(h, wu)
        return _dot(jnp.square(jnp.maximum(u, 0.0)).astype(BF16), wd)

    o_ref[...] += up_act_down(h_ref[...])

    @pl.when(i == 0)
    def _():
        os_ref[...] += up_act_down(hs_ref[...])

    if final:
        @pl.when(f == last)
        def _():
            o_ref[...] = _rms(o_ref[...], gf_ref[...])

        @pl.when((i == 0) & (f == last))
        def _():
            os_ref[...] = _rms(os_ref[...], gf_ref[...])


def _mlp(x, xs, g_all, li, w_up, w_down, final_g, tm, tf):
    m, d = x.shape
    ms = xs.shape[0]
    d_ff = w_up.shape[2]
    in_specs = [
        pl.BlockSpec((tm, d), lambda i, f: (i, 0), pipeline_mode=pl.Buffered(1)),
        pl.BlockSpec((ms, d), lambda i, f: (0, 0)),
        pl.BlockSpec((None, 1, d), lambda i, f: (li, 0, 0)),
        pl.BlockSpec((None, d, tf), lambda i, f: (li, 0, f)),
        pl.BlockSpec((None, tf, d), lambda i, f: (li, f, 0)),
    ]
    args = [x, xs, g_all, w_up, w_down]
    if final_g is not None:
        in_specs.append(pl.BlockSpec((1, d), lambda i, f: (0, 0)))
        args.append(final_g)
    return pl.pallas_call(
        functools.partial(_mlp_kernel, final=final_g is not None),
        grid=(m // tm, d_ff // tf),
        in_specs=in_specs,
        out_specs=[pl.BlockSpec((tm, d), lambda i, f: (i, 0)), pl.BlockSpec((ms, d), lambda i, f: (0, 0))],
        out_shape=[jax.ShapeDtypeStruct((m, d), F32), jax.ShapeDtypeStruct((ms, d), F32)],
        scratch_shapes=[pltpu.VMEM((tm, d), BF16), pltpu.VMEM((ms, d), BF16)],
        compiler_params=_params(2),
        name="mlp",
    )(*args)


def _flash_kernel_old(*refs, group, t, window, dil, sel_blk, sel_per_head, want_lse, head0):
    it = iter(refs)
    q_ref, k_ref, v_ref, tab_ref, far_ref = (next(it) for _ in range(5))
    sel_ref = next(it) if sel_blk else None
    o_ref = next(it)
    lse_ref = next(it) if want_lse else None
    kb_ref, vb_ref, qs_ref, m_ref, l_ref, acc_ref = (next(it) for _ in range(6))

    kvh = pl.program_id(1)
    qi = pl.program_id(2)
    rows = group * t

    @pl.when(qi == 0)
    def _():
        kb_ref[...] = k_ref[0].astype(BF16)
        vb_ref[...] = v_ref[0].astype(BF16)

    for j in range(group):
        qs_ref[j * t:(j + 1) * t, :] = q_ref[0, :, j * HEAD_DIM:(j + 1) * HEAD_DIM].astype(BF16)
    m_ref[...] = jnp.full((rows, HEAD_DIM), NEG_INF, F32)
    l_ref[...] = jnp.zeros((rows, HEAD_DIM), F32)
    acc_ref[...] = jnp.zeros((rows, HEAD_DIM), F32)

    delta = (lax.broadcasted_iota(jnp.int32, (t, t), 0) - lax.broadcasted_iota(jnp.int32, (t, t), 1))

    def tile(kj, kind):
        start = pl.multiple_of(kj * t, t)
        k = kb_ref[pl.ds(start, t), :]
        v = vb_ref[pl.ds(start, t), :]
        s = _dot(qs_ref[...], k, _NT) * SOFTMAX_SCALE
        dist = delta + (qi - kj) * t
        ok = None
        if kind == "diag":
            ok = dist >= 0
        if window is not None:
            w_ok = dist <= window
            ok = w_ok if ok is None else ok & w_ok
        if dil > 1:
            d_ok = (dist & (dil - 1)) == 0
            ok = d_ok if ok is None else ok & d_ok
        sel_ok = None
        if sel_blk:
            blk_of_key = (start + lax.broadcasted_iota(jnp.int32, (HEAD_DIM, t), 1)) // sel_blk
            expand = (lax.broadcasted_iota(jnp.int32, (HEAD_DIM, t), 0) == blk_of_key)
            expand = jnp.where(expand, 1.0, 0.0).astype(BF16)
            if sel_per_head:
                sel_rows = jnp.concatenate([sel_ref[0, j] for j in range(group)], axis=0)
            else:
                sel_rows = sel_ref[0, 0]
            sel_ok = _dot(sel_rows.astype(BF16), expand) > 0.5
        pieces = []
        for j in range(group):
            sj = s[j * t:(j + 1) * t]
            if kind == "far":
                sj = sj + far_ref[head0 + kvh * group + j]
            else:
                sj = sj + tab_ref[j, 0 if kind == "diag" else 1]
            mask = ok
            if sel_blk:
                sel_j = sel_ok[j * t:(j + 1) * t] if sel_per_head else sel_ok
                mask = sel_j if mask is None else mask & sel_j
            pieces.append((sj, mask))
        if pieces[0][1] is not None:
            s = jnp.concatenate([jnp.where(mk, sj, NEG_INF) for sj, mk in pieces], axis=0)
            mask_all = jnp.concatenate([mk for _, mk in pieces], axis=0)
        else:
            s = jnp.concatenate([sj for sj, _ in pieces], axis=0)
            mask_all = None
        m_prev = m_ref[...]
        m_next = jnp.maximum(m_prev, jnp.max(s, axis=1, keepdims=True))
        p = jnp.exp(s - jnp.concatenate([m_next] * (t // HEAD_DIM), axis=1))
        if mask_all is not None:
            p = jnp.where(mask_all, p, 0.0)
        alpha = jnp.exp(m_prev - m_next)
        l_ref[...] = alpha * l_ref[...] + jnp.sum(p, axis=1, keepdims=True)
        acc_ref[...] = alpha * acc_ref[...] + _dot(p.astype(BF16), v)
        m_ref[...] = m_next

    lo = 0 if window is None else jnp.maximum(qi * t - window, 0) // t

    def far_body(kj, carry):
        tile(kj, "far")
        return carry

    lax.fori_loop(lo, jnp.maximum(qi - 1, lo), far_body, 0)

    @pl.when(qi - 1 >= lo)
    def _():
        tile(qi - 1, "near")

    tile(qi, "diag")

    l = l_ref[...]
    out = acc_ref[...] / l
    for j in range(group):
        o_ref[0, :, j * HEAD_DIM:(j + 1) * HEAD_DIM] = out[j * t:(j + 1) * t]
    if want_lse:
        lse = m_ref[...] + jnp.log(l)
        for j in range(group):
            lse_ref[0, :, j * HEAD_DIM:(j + 1) * HEAD_DIM] = lse[j * t:(j + 1) * t]


def _flash_old(proj, tabs, far, sel, *, n_kv, group, q_col, k_col, v_col, head0=0,
           window=None, dil=1, sel_blk=0, sel_per_head=False, want_lse=False):
    b, s, _ = proj.shape
    t = ATTN_TILE
    gw = group * HEAD_DIM
    in_specs = [
        pl.BlockSpec((1, t, gw), lambda bi, h, qi: (bi, qi, q_col // group + h)),
        pl.BlockSpec((1, s, HEAD_DIM), lambda bi, h, qi: (bi, 0, k_col + h)),
        pl.BlockSpec((1, s, HEAD_DIM), lambda bi, h, qi: (bi, 0, v_col + h)),
        pl.BlockSpec((group, 2, t, t), lambda bi, h, qi: (head0 // group + h, 0, 0, 0)),
        pl.BlockSpec(memory_space=pltpu.SMEM),
    ]
    args = [proj, proj, proj, tabs, far]
    if sel_blk:
        if sel_per_head:
            in_specs.append(pl.BlockSpec((1, group, t, HEAD_DIM), lambda bi, h, qi: (bi, h, qi, 0)))
        else:
            in_specs.append(pl.BlockSpec((1, 1, t, HEAD_DIM), lambda bi, h, qi: (bi, h, qi, 0)))
        args.append(sel)
    out_spec = pl.BlockSpec((1, t, gw), lambda bi, h, qi: (bi, qi, h))
    out_sds = jax.ShapeDtypeStruct((b, s, n_kv * gw), F32)
    rows = group * t
    return pl.pallas_call(
        functools.partial(_flash_kernel, group=group, t=t, window=window, dil=dil, sel_blk=sel_blk,
                          sel_per_head=sel_per_head, want_lse=want_lse, head0=head0),
        grid=(b, n_kv, s // t),
        in_specs=in_specs,
        out_specs=[out_spec, out_spec] if want_lse else out_spec,
        out_shape=[out_sds, out_sds] if want_lse else out_sds,
        scratch_shapes=[pltpu.VMEM((s, HEAD_DIM), BF16), pltpu.VMEM((s, HEAD_DIM), BF16),
                        pltpu.VMEM((rows, HEAD_DIM), BF16), pltpu.VMEM((rows, HEAD_DIM), F32),
                        pltpu.VMEM((rows, HEAD_DIM), F32), pltpu.VMEM((rows, HEAD_DIM), F32)],
        compiler_params=_params(3),
        name="flash",
    )(*args)


def _flash_kernel(*refs, group, t, w_tiles, far_tab, edge_tab, sel_blk, sel_per_head, want_lse, head0):
    it = iter(refs)
    q_ref, k_ref, v_ref, tab_ref, far_ref = (next(it) for _ in range(5))
    sel_ref = next(it) if sel_blk else None
    o_ref = next(it)
    lse_ref = next(it) if want_lse else None
    kb_ref, vb_ref, qs_ref, s_ref, m_ref, l_ref, acc_ref = (next(it) for _ in range(7))

    kvh = pl.program_id(1)
    qi = pl.program_id(2)
    rows = group * t

    @pl.when(qi == 0)
    def _():
        kb_ref[...] = k_ref[0].astype(BF16)
        vb_ref[...] = v_ref[0].astype(BF16)

    for j in range(group):
        qs_ref[j * t:(j + 1) * t, :] = q_ref[0, :, j * HEAD_DIM:(j + 1) * HEAD_DIM].astype(BF16)

    def tile(kj, tab, first=False):
        start = pl.multiple_of(kj * t, t)
        k = kb_ref[pl.ds(start, t), :]
        v = vb_ref[pl.ds(start, t), :]
        sel_add = None
        if sel_blk and not sel_per_head:
            blk_of_key = (start + lax.broadcasted_iota(jnp.int32, (HEAD_DIM, t), 1)) // sel_blk
            expand = lax.broadcasted_iota(jnp.int32, (HEAD_DIM, t), 0) == blk_of_key
            expand = jnp.where(expand, 1.0, 0.0).astype(BF16)
            sel_add = _dot(sel_ref[0, 0].astype(BF16), expand)
        for j in range(group):
            hs = slice(j * t, (j + 1) * t)
            sj = _dot(qs_ref[hs, :], k, _NT) * SOFTMAX_SCALE
            sj = sj + (far_ref[head0 + kvh * group + j] if tab is None else tab_ref[j, tab])
            if sel_blk and sel_per_head:
                assert sel_blk == t
                lane = lax.broadcasted_iota(jnp.int32, (t, HEAD_DIM), 1)
                sj = sj + jnp.sum(jnp.where(lane == kj, sel_ref[0, j], 0.0), axis=1, keepdims=True)
            elif sel_blk:
                sj = sj + sel_add
            m_cur = jnp.max(jnp.maximum(sj[:, :HEAD_DIM], sj[:, HEAD_DIM:]), axis=1, keepdims=True)
            if first:
                m_next = jnp.broadcast_to(m_cur, (t, HEAD_DIM))
            else:
                m_prev = m_ref[hs, :]
                m_next = jnp.maximum(m_prev, m_cur)
            p = jnp.exp(sj - jnp.concatenate([m_next] * (t // HEAD_DIM), axis=1))
            p_sum = jnp.sum(p[:, :HEAD_DIM] + p[:, HEAD_DIM:], axis=1, keepdims=True)
            pv = _dot(p.astype(BF16), v)
            if first:
                l_ref[hs, :] = jnp.broadcast_to(p_sum, (t, HEAD_DIM))
                acc_ref[hs, :] = pv
            else:
                alpha = jnp.exp(m_prev - m_next)
                l_ref[hs, :] = alpha * l_ref[hs, :] + p_sum
                acc_ref[hs, :] = alpha * acc_ref[hs, :] + pv
            m_ref[hs, :] = m_next

    tile(qi, 0, first=True)

    @pl.when(qi >= 1)
    def _():
        tile(qi - 1, 1)

    def far_body(kj, carry):
        tile(kj, far_tab)
        return carry

    if w_tiles is None:
        lax.fori_loop(0, qi - 1, far_body, 0)
    elif w_tiles >= 2:
        lax.fori_loop(jnp.maximum(qi - w_tiles + 1, 0), qi - 1, far_body, 0)

        @pl.when(qi >= w_tiles)
        def _():
            tile(qi - w_tiles, edge_tab)

    l = l_ref[...]
    out = acc_ref[...] / l
    for j in range(group):
        o_ref[0, :, j * HEAD_DIM:(j + 1) * HEAD_DIM] = out[j * t:(j + 1) * t]
    if want_lse:
        lse = m_ref[...] + jnp.log(l)
        for j in range(group):
            lse_ref[0, :, j * HEAD_DIM:(j + 1) * HEAD_DIM] = lse[j * t:(j + 1) * t]


def _flash(proj, tabs, far, sel, *, n_kv, group, q_col, k_col, v_col, head0=0,
           window=None, dil=1, sel_blk=0, sel_per_head=False, want_lse=False):
    b, s, _ = proj.shape
    t = ATTN_TILE
    gw = group * HEAD_DIM
    n_tab = tabs.shape[1]
    w_tiles = None if window is None else window // t
    far_tab = 2 if dil > 1 else None
    edge_tab = n_tab - 1 if (w_tiles is not None and w_tiles >= 2) else None
    in_specs = [
        pl.BlockSpec((1, t, gw), lambda bi, h, qi: (bi, qi, q_col // group + h)),
        pl.BlockSpec((1, s, HEAD_DIM), lambda bi, h, qi: (bi, 0, k_col + h)),
        pl.BlockSpec((1, s, HEAD_DIM), lambda bi, h, qi: (bi, 0, v_col + h)),
        pl.BlockSpec((group, n_tab, t, t), lambda bi, h, qi: (head0 // group + h, 0, 0, 0)),
        pl.BlockSpec(memory_space=pltpu.SMEM),
    ]
    args = [proj, proj, proj, tabs, far]
    if sel_blk:
        if sel_per_head:
            in_specs.append(pl.BlockSpec((1, group, t, HEAD_DIM), lambda bi, h, qi: (bi, h, qi, 0)))
        else:
            in_specs.append(pl.BlockSpec((1, 1, t, HEAD_DIM), lambda bi, h, qi: (bi, h, qi, 0)))
        args.append(sel)
    out_spec = pl.BlockSpec((1, t, gw), lambda bi, h, qi: (bi, qi, h))
    out_sds = jax.ShapeDtypeStruct((b, s, n_kv * gw), F32)
    rows = group * t
    return pl.pallas_call(
        functools.partial(_flash_kernel, group=group, t=t, w_tiles=w_tiles, far_tab=far_tab, edge_tab=edge_tab,
                          sel_blk=sel_blk, sel_per_head=sel_per_head, want_lse=want_lse, head0=head0),
        grid=(b, n_kv, s // t),
        in_specs=in_specs,
        out_specs=[out_spec, out_spec] if want_lse else out_spec,
        out_shape=[out_sds, out_sds] if want_lse else out_sds,
        scratch_shapes=[pltpu.VMEM((s, HEAD_DIM), BF16), pltpu.VMEM((s, HEAD_DIM), BF16),
                        pltpu.VMEM((rows, HEAD_DIM), BF16), pltpu.VMEM((rows, t), F32),
                        pltpu.VMEM((rows, HEAD_DIM), F32), pltpu.VMEM((rows, HEAD_DIM), F32),
                        pltpu.VMEM((rows, HEAD_DIM), F32)],
        compiler_params=_params(3),
        name="flash",
    )(*args)


def _flash2_kernel(*refs, group, t, w_tiles, far_tab, edge_tab, sel_blk, sel_per_head, head0):
    it = iter(refs)
    q_ref, k_ref, v_ref, tab_ref, far_ref = (next(it) for _ in range(5))
    sel_ref = next(it) if sel_blk else None
    o_ref = next(it)
    kb_ref, vb_ref, qs_ref, s_ref, mx_ref, l_ref, acc_ref = (next(it) for _ in range(7))

    kvh = pl.program_id(1)
    qi = pl.program_id(2)
    rows = group * t

    @pl.when(qi == 0)
    def _():
        kb_ref[...] = k_ref[0].astype(BF16)
        vb_ref[...] = v_ref[0].astype(BF16)

    for j in range(group):
        qs_ref[j * t:(j + 1) * t, :] = q_ref[0, :, j * HEAD_DIM:(j + 1) * HEAD_DIM].astype(BF16)

    def scores(tiles, tab, first=False):
        per_head = [[] for _ in range(group)]
        for kj in tiles:
            start = pl.multiple_of(kj * t, t)
            k = kb_ref[pl.ds(start, t), :]
            sel_add = None
            if sel_blk and not sel_per_head:
                blk_of_key = (start + lax.broadcasted_iota(jnp.int32, (HEAD_DIM, t), 1)) // sel_blk
                expand = lax.broadcasted_iota(jnp.int32, (HEAD_DIM, t), 0) == blk_of_key
                expand = jnp.where(expand, 1.0, 0.0).astype(BF16)
                sel_add = _dot(sel_ref[0, 0].astype(BF16), expand)
            for j in range(group):
                hs = slice(j * t, (j + 1) * t)
                sj = _dot(qs_ref[hs, :], k, _NT) * (SOFTMAX_SCALE * LOG2E)
                sj = sj + (far_ref[head0 + kvh * group + j] if tab is None else tab_ref[j, tab])
                if sel_blk and sel_per_head:
                    assert sel_blk == t
                    lane = lax.broadcasted_iota(jnp.int32, (t, HEAD_DIM), 1)
                    sj = sj + jnp.sum(jnp.where(lane == kj, sel_ref[0, j], 0.0), axis=1, keepdims=True)
                elif sel_blk:
                    sj = sj + sel_add
                s_ref[kj, hs, :] = sj
                per_head[j].append(jnp.maximum(sj[:, :HEAD_DIM], sj[:, HEAD_DIM:]))
        for j in range(group):
            hs = slice(j * t, (j + 1) * t)
            mj = functools.reduce(jnp.maximum, per_head[j])
            mx_ref[hs, :] = mj if first else jnp.maximum(mx_ref[hs, :], mj)

    def accumulate(tiles, tab, first=False):
        for j in range(group):
            hs = slice(j * t, (j + 1) * t)
            m = mx_ref[hs, :]
            m2 = jnp.concatenate([m, m], axis=1)
            p_lane, pv = None, None
            for kj in tiles:
                v = vb_ref[pl.ds(pl.multiple_of(kj * t, t), t), :]
                p = jnp.exp2(s_ref[kj, hs, :] - m2)
                pl_ = p[:, :HEAD_DIM] + p[:, HEAD_DIM:]
                pv_ = _dot(p.astype(BF16), v)
                p_lane = pl_ if p_lane is None else p_lane + pl_
                pv = pv_ if pv is None else pv + pv_
            l_ref[hs, :] = p_lane if first else l_ref[hs, :] + p_lane
            acc_ref[hs, :] = pv if first else acc_ref[hs, :] + pv

    def sweep(visit):
        visit([qi], 0, first=True)

        @pl.when(qi >= 1)
        def _():
            visit([qi - 1], 1)

        def far_tiles(lo, hi):
            def pair_body(i, carry):
                visit([lo + 2 * i, lo + 2 * i + 1], far_tab)
                return carry

            n_far = jnp.maximum(hi - lo, 0)
            lax.fori_loop(0, n_far // 2, pair_body, 0)

            @pl.when(n_far % 2 == 1)
            def _():
                visit([hi - 1], far_tab)

        if w_tiles is None:
            far_tiles(0, qi - 1)
        elif w_tiles >= 2:
            far_tiles(jnp.maximum(qi - w_tiles + 1, 0), qi - 1)

            @pl.when(qi >= w_tiles)
            def _():
                visit([qi - w_tiles], edge_tab)

    sweep(scores)
    mx_ref[...] = jnp.broadcast_to(jnp.max(mx_ref[...], axis=1, keepdims=True), (rows, HEAD_DIM))
    sweep(accumulate)

    out = acc_ref[...] / jnp.sum(l_ref[...], axis=1, keepdims=True)
    for j in range(group):
        o_ref[0, :, j * HEAD_DIM:(j + 1) * HEAD_DIM] = out[j * t:(j + 1) * t]


def _flash2(proj, tabs, far, sel, *, n_kv, group, q_col, k_col, v_col, head0=0,
            window=None, sel_blk=0, sel_per_head=False):
    b, s, _ = proj.shape
    t = ATTN_TILE
    assert t == 2 * HEAD_DIM
    gw = group * HEAD_DIM
    n_tab = tabs.shape[1]
    w_tiles = None if window is None else window // t
    edge_tab = n_tab - 1 if (w_tiles is not None and w_tiles >= 2) else None
    tabs, far = tabs * LOG2E, far * LOG2E
    in_specs = [
        pl.BlockSpec((1, t, gw), lambda bi, h, qi: (bi, qi, q_col // group + h)),
        pl.BlockSpec((1, s, HEAD_DIM), lambda bi, h, qi: (bi, 0, k_col + h)),
        pl.BlockSpec((1, s, HEAD_DIM), lambda bi, h, qi: (bi, 0, v_col + h)),
        pl.BlockSpec((group, n_tab, t, t), lambda bi, h, qi: (head0 // group + h, 0, 0, 0)),
        pl.BlockSpec(memory_space=pltpu.SMEM),
    ]
    args = [proj, proj, proj, tabs, far]
    if sel_blk:
        if sel_per_head:
            in_specs.append(pl.BlockSpec((1, group, t, HEAD_DIM), lambda bi, h, qi: (bi, h, qi, 0)))
        else:
            in_specs.append(pl.BlockSpec((1, 1, t, HEAD_DIM), lambda bi, h, qi: (bi, h, qi, 0)))
        args.append(sel)
    rows = group * t
    return pl.pallas_call(
        functools.partial(_flash2_kernel, group=group, t=t, w_tiles=w_tiles, far_tab=None, edge_tab=edge_tab,
                          sel_blk=sel_blk, sel_per_head=sel_per_head, head0=head0),
        grid=(b, n_kv, s // t),
        in_specs=in_specs,
        out_specs=pl.BlockSpec((1, t, gw), lambda bi, h, qi: (bi, qi, h)),
        out_shape=jax.ShapeDtypeStruct((b, s, n_kv * gw), F32),
        scratch_shapes=[pltpu.VMEM((s, HEAD_DIM), BF16), pltpu.VMEM((s, HEAD_DIM), BF16),
                        pltpu.VMEM((rows, HEAD_DIM), BF16), pltpu.VMEM((s // t, rows, t), F32),
                        pltpu.VMEM((rows, HEAD_DIM), F32), pltpu.VMEM((rows, HEAD_DIM), F32),
                        pltpu.VMEM((rows, HEAD_DIM), F32)],
        compiler_params=_params(3),
        name="flash2",
    )(*args)


def _dilated_kernel(q_ref, k_ref, v_ref, tab_ref, o_ref, lse_ref, m_ref, l_ref, acc_ref, *, t, dil, n_q):
    def rows(r, tile_idx):
        return pl.ds(r + tile_idx * (t * dil), t, stride=dil)

    def lane_pair(op, x):
        return op(x[:, :HEAD_DIM], x[:, HEAD_DIM:])

    def one(idx):
        r = idx // n_q
        qi = idx % n_q
        q = q_ref[0, rows(r, qi), :].astype(BF16)
        tiles = [(qi, 0, None)]
        if n_q > 1:
            tiles.append((jnp.maximum(qi - 1, 0), 1, jnp.where(qi >= 1, 0.0, NEG_INF)))
        scores, values = [], []
        for kj, tab, extra in tiles:
            k = k_ref[0, rows(r, kj), :].astype(BF16)
            values.append(v_ref[0, rows(r, kj), :].astype(BF16))
            s = _dot(q, k, _NT) * SOFTMAX_SCALE + tab_ref[0, tab]
            scores.append(s if extra is None else s + extra)
        m_lane = lane_pair(jnp.maximum, scores[0])
        for s in scores[1:]:
            m_lane = jnp.maximum(m_lane, lane_pair(jnp.maximum, s))
        m = jnp.max(m_lane, axis=1, keepdims=True)
        l_lane, acc = None, None
        for s, v in zip(scores, values):
            p = jnp.exp(s - m)
            pl_ = lane_pair(jnp.add, p)
            pv = _dot(p.astype(BF16), v)
            l_lane = pl_ if l_lane is None else l_lane + pl_
            acc = pv if acc is None else acc + pv
        l = jnp.sum(l_lane, axis=1, keepdims=True)
        o_ref[0, rows(r, qi), :] = acc / l
        lse_ref[0, rows(r, qi), :] = jnp.broadcast_to(m + jnp.log(l), (t, HEAD_DIM))

    n_it = dil * n_q
    chains = 4 if n_it % 4 == 0 else 1

    def body(it, carry):
        for u in range(chains):
            one(it * chains + u)
        return carry

    lax.fori_loop(0, n_it // chains, body, 0)


def _dilated_attention(proj, tabs, *, n_heads, q_col, k_col, v_col, dil):
    b, s, _ = proj.shape
    t = ATTN_TILE
    assert t == 2 * HEAD_DIM and (s // dil) % t == 0
    spec = lambda col: pl.BlockSpec((1, s, HEAD_DIM), lambda bi, h: (bi, 0, col + h))
    out_sds = jax.ShapeDtypeStruct((b, s, n_heads * HEAD_DIM), F32)
    return pl.pallas_call(
        functools.partial(_dilated_kernel, t=t, dil=dil, n_q=s // dil // t),
        grid=(b, n_heads),
        in_specs=[spec(q_col), spec(k_col), spec(v_col),
                  pl.BlockSpec((1, 2, t, t), lambda bi, h: (h, 0, 0, 0))],
        out_specs=[spec(0), spec(0)],
        out_shape=[out_sds, out_sds],
        scratch_shapes=[pltpu.VMEM((t, HEAD_DIM), F32)] * 3,
        compiler_params=_params(2),
        name="dilated_attention",
    )(proj, proj, proj, tabs)


def _moba_select_kernel(q_ref, k_ref, sel_ref, kmean_ref, *, group, t, n_blk):
    qi = pl.program_id(2)

    @pl.when(qi == 0)
    def _():
        kmean_ref[...] = jnp.zeros_like(kmean_ref)
        s = k_ref.shape[1]
        km = jnp.sum(k_ref[0].reshape(n_blk, s // n_blk, HEAD_DIM), axis=1) * (1.0 / B_BLOCK)
        kmean_ref[0:n_blk, :] = km

    lane = lax.broadcasted_iota(jnp.int32, (t, HEAD_DIM), 1)
    pos = qi * t + lax.broadcasted_iota(jnp.int32, (t, HEAD_DIM), 0)
    own = pos // B_BLOCK
    past = lane < own
    kmean = kmean_ref[...]
    for j in range(group):
        q = q_ref[0, :, j * HEAD_DIM:(j + 1) * HEAD_DIM]
        gate = _dot_f32_both(q, kmean, _NT)
        gate = jnp.where(past, gate, NEG_INF)
        rank = _rank_lanes(gate, n_blk)
        sel = (past & (rank < B_TOPK)) | (lane == own)
        sel_ref[0, j] = jnp.where(sel, 0.0, NEG_INF)


def _moba_select(proj, *, n_kv, group, k_col):
    b, s, _ = proj.shape
    t = ATTN_TILE
    n_blk = s // B_BLOCK
    return pl.pallas_call(
        functools.partial(_moba_select_kernel, group=group, t=t, n_blk=n_blk),
        grid=(b, n_kv, s // t),
        in_specs=[pl.BlockSpec((1, t, group * HEAD_DIM), lambda bi, h, qi: (bi, qi, h)),
                  pl.BlockSpec((1, s, HEAD_DIM), lambda bi, h, qi: (bi, 0, k_col + h))],
        out_specs=pl.BlockSpec((1, group, t, HEAD_DIM), lambda bi, h, qi: (bi, h, qi, 0)),
        out_shape=jax.ShapeDtypeStruct((b, n_kv * group, s, HEAD_DIM), F32),
        scratch_shapes=[pltpu.VMEM((HEAD_DIM, HEAD_DIM), F32)],
        compiler_params=_params(3),
        name="moba_select",
    )(proj, proj)


def _cmp1_prompt_kernel(x_ref, w_ref, y_ref, *, n_seg):
    x2 = jnp.concatenate([x_ref[pl.ds(s, n_seg, stride=C_CMP_STRIDE), :] for s in range(C_CMP_STRIDE)], axis=1)
    y_ref[...] = _dot(x2.astype(BF16), w_ref[...])


def _cmp1_prompt(proj, w1cat, *, col0):
    b, s, _ = proj.shape
    n_seg = s // C_CMP_STRIDE
    return pl.pallas_call(
        functools.partial(_cmp1_prompt_kernel, n_seg=n_seg),
        grid=(b, 2, C_KV_HEADS),
        in_specs=[pl.BlockSpec((None, s, HEAD_DIM), lambda bi, c, g: (bi, 0, col0 + c * C_KV_HEADS + g)),
                  pl.BlockSpec((None, C_CMP_STRIDE * HEAD_DIM, 2 * HEAD_DIM), lambda bi, c, g: (c, 0, 0))],
        out_specs=pl.BlockSpec((None, n_seg, 2 * HEAD_DIM), lambda bi, c, g: ((bi * 2 + c) * C_KV_HEADS + g, 0, 0)),
        out_shape=jax.ShapeDtypeStruct((b * 2 * C_KV_HEADS, n_seg, 2 * HEAD_DIM), F32),
        compiler_params=_params(3),
        name="cmp1_prompt",
    )(proj, w1cat)


def _cmp2_kernel(y_ref, pe_ref, w1_ref, w2_ref, o_ref, *, n_seg):
    y = y_ref[0]
    first = y[:, :HEAD_DIM]
    second = pltpu.roll(y[:, HEAD_DIM:], n_seg - 1, 0)
    pe_term = _dot(pe_ref[0].astype(BF16), w1_ref[0].astype(BF16))[0:1]
    pre = first + second + pe_term
    tok = _dot(jnp.maximum(pre, 0.0).astype(BF16), w2_ref[0].astype(BF16))
    row = lax.broadcasted_iota(jnp.int32, tok.shape, 0)
    o_ref[0] = jnp.where(row < n_seg - 1, tok, 0.0)


def _cmp2(y, pe8, w1, w2):
    r, n_seg, _ = y.shape
    comp = lambda i: ((i // C_KV_HEADS) % 2, 0, 0)
    return pl.pallas_call(
        functools.partial(_cmp2_kernel, n_seg=n_seg),
        grid=(r,),
        in_specs=[pl.BlockSpec((1, n_seg, 2 * HEAD_DIM), lambda i: (i, 0, 0)),
                  pl.BlockSpec((1, 8, C_CMP_BLOCK * HEAD_DIM), comp),
                  pl.BlockSpec((1, C_CMP_BLOCK * HEAD_DIM, HEAD_DIM), comp),
                  pl.BlockSpec((1, HEAD_DIM, HEAD_DIM), comp)],
        out_specs=pl.BlockSpec((1, n_seg, HEAD_DIM), lambda i: (i, 0, 0)),
        out_shape=jax.ShapeDtypeStruct((r, n_seg, HEAD_DIM), F32),
        compiler_params=_params(1),
        name="cmp2",
    )(y, pe8, w1, w2)


CMP_BAND = 8


def _nsa_cmp_select_kernel(q_ref, kc_ref, vc_ref, bias_ref, far_ref, map_ref, oc_ref, sel_ref, *,
                           group, t, n_cmp, n_sel):
    kvh = pl.program_id(1)
    qi = pl.program_id(2)
    n_pad = kc_ref.shape[1]
    pos = qi * t + lax.broadcasted_iota(jnp.int32, (t, n_pad), 0)
    tok = lax.broadcasted_iota(jnp.int32, (t, n_pad), 1)
    valid = (pos - (tok * C_CMP_STRIDE + (C_CMP_BLOCK - 1)) >= 0) & (tok < n_cmp)
    band0 = qi * (t // C_CMP_STRIDE) - CMP_BAND
    shift = jnp.where(band0 < 0, band0 + n_pad, band0)
    is_far = tok < band0
    kc = kc_ref[0].astype(BF16)
    vc = vc_ref[0].astype(BF16)
    p_sum = jnp.zeros((t, n_pad), F32)
    for j in range(group):
        q = q_ref[0, :, j * HEAD_DIM:(j + 1) * HEAD_DIM].astype(BF16)
        bias = jnp.where(is_far, far_ref[kvh * group + j], pltpu.roll(bias_ref[j], shift, 1))
        s = _dot(q, kc, _NT) * SOFTMAX_SCALE + bias
        s = jnp.where(valid, s, NEG_INF)
        m = jnp.max(s, axis=1, keepdims=True)
        e = jnp.where(valid, jnp.exp(s - m), 0.0)
        p = e / jnp.maximum(jnp.sum(e, axis=1, keepdims=True), 1e-30)
        oc_ref[0, :, j * HEAD_DIM:(j + 1) * HEAD_DIM] = _dot(p.astype(BF16), vc)
        p_sum = p_sum + p
    imp = _dot_f32_lhs(p_sum, map_ref[...])
    blk = lax.broadcasted_iota(jnp.int32, (t, HEAD_DIM), 1)
    cur = (qi * t + lax.broadcasted_iota(jnp.int32, (t, HEAD_DIM), 0)) // C_SLC_BLOCK
    avail = (blk <= cur) & (blk < n_sel)
    forced = (blk == 0) | (blk == cur) | (blk == cur - 1)
    score = jnp.where(forced, jnp.inf, imp)
    score = jnp.where(avail, score, -jnp.inf)
    rank = _rank_lanes(score, n_sel)
    sel = avail & (rank < min(C_SLC_TOPK, n_sel))
    sel_ref[0, 0] = jnp.where(sel, 0.0, NEG_INF)


def _cmp_bias_table(rel_bias, s, n_seg):
    assert s == n_seg * C_CMP_STRIDE
    h = rel_bias.shape[1]
    k = np.arange(2 * n_seg + 1)
    e = np.where(k <= n_seg, -k, 2 * n_seg + 1 - k)
    dist = C_CMP_STRIDE * e[None, :] + np.arange(C_CMP_STRIDE)[:, None] - (C_CMP_BLOCK - 1)
    w = jnp.transpose(rel_bias[jnp.asarray(_bucket_np(dist))], (2, 0, 1))
    tab = _toeplitz(w.reshape(h * C_CMP_STRIDE, -1), n_seg).reshape(h, C_CMP_STRIDE, n_seg, n_seg)
    return jnp.transpose(tab, (0, 2, 1, 3)).reshape(h, s, n_seg)


def _cmp_map_np(n_rows, n_cmp, n_cols):
    ratio = C_SLC_BLOCK // C_CMP_STRIDE
    m = np.zeros((n_rows, n_cols), np.float32)
    for i in range(n_cmp):
        for r in range(C_CMP_BLOCK // C_CMP_STRIDE):
            m[i, (i + r) // ratio] += 1.0
    return m


def _nsa_cmp_select(proj, kvc, rel_bias, far, *, n_kv, group):
    b, s, _ = proj.shape
    t = ATTN_TILE
    n_seg = kvc.shape[1]
    n_cmp = n_seg - 1
    n_sel = -(-s // C_SLC_BLOCK)
    cmap = jnp.asarray(_cmp_map_np(n_seg, n_cmp, HEAD_DIM), BF16)
    gw = group * HEAD_DIM
    n_band = CMP_BAND + 2 * (t // C_CMP_STRIDE) - (C_CMP_BLOCK - 1) // C_CMP_STRIDE
    assert n_band <= n_seg and C_CMP_STRIDE * (CMP_BAND + 1) - (C_CMP_BLOCK - 1) >= 113
    dist = np.arange(t)[:, None] - C_CMP_STRIDE * (np.arange(n_band)[None, :] - CMP_BAND) - (C_CMP_BLOCK - 1)
    bias_c = jnp.pad(_bias_by_dist(rel_bias, np.maximum(dist, 0)), ((0, 0), (0, 0), (0, n_seg - n_band)))
    return pl.pallas_call(
        functools.partial(_nsa_cmp_select_kernel, group=group, t=t, n_cmp=n_cmp, n_sel=n_sel),
        grid=(b, n_kv, s // t),
        in_specs=[pl.BlockSpec((1, t, gw), lambda bi, h, qi: (bi, qi, h)),
                  pl.BlockSpec((1, n_seg, HEAD_DIM), lambda bi, h, qi: (bi * 8 + h, 0, 0)),
                  pl.BlockSpec((1, n_seg, HEAD_DIM), lambda bi, h, qi: (bi * 8 + 4 + h, 0, 0)),
                  pl.BlockSpec((group, t, n_seg), lambda bi, h, qi: (h, 0, 0)),
                  pl.BlockSpec(memory_space=pltpu.SMEM),
                  pl.BlockSpec((n_seg, HEAD_DIM), lambda bi, h, qi: (0, 0))],
        out_specs=[pl.BlockSpec((1, t, gw), lambda bi, h, qi: (bi, qi, h)),
                   pl.BlockSpec((1, 1, t, HEAD_DIM), lambda bi, h, qi: (bi, h, qi, 0))],
        out_shape=[jax.ShapeDtypeStruct((b, s, n_kv * gw), F32),
                   jax.ShapeDtypeStruct((b, n_kv, s, HEAD_DIM), F32)],
        compiler_params=_params(3),
        name="nsa_cmp_select",
    )(proj, kvc, kvc, bias_c, far, cmap)


def _row_tile(m, pref):
    return pref if m % pref == 0 else m


def _mixer_a_prompt(x, xs, b, g_mix, li, w_in, w_out, slot, tabs, far):
    m = x.shape[0]
    s = m // b
    n_in = w_in.shape[2]
    proj, proj_s = _norm_matmul(x, g_mix, li, w_in, slot, n_in, _row_tile(m, PROMPT_ROW_TILE), 512, xs=xs)
    proj3 = proj.reshape(b, s, n_in)
    outs, lses = [], []
    for g, (window, dil) in enumerate(A_PATTERNS):
        c0 = g * 3 * A_HEADS
        o, lse = _dilated_attention(proj3, tabs[g], n_heads=A_HEADS, q_col=c0, k_col=c0 + A_HEADS,
                                    v_col=c0 + 2 * A_HEADS, dil=dil)
        outs.append(o.reshape(m, -1))
        lses.append(lse.reshape(m, -1))
    x = _outproj(outs, lses, w_out, slot, x, "lse3", _row_tile(m, 512), 512)
    return x, proj3, proj_s


def _mixer_b_prompt(x, xs, b, g_mix, li, w_in, w_out, slot, tabs, far):
    m = x.shape[0]
    s = m // b
    n_in = w_in.shape[2]
    proj, proj_s = _norm_matmul(x, g_mix, li, w_in, slot, n_in, _row_tile(m, PROMPT_ROW_TILE), 512, xs=xs)
    proj3 = proj.reshape(b, s, n_in)
    sel = _moba_select(proj3, n_kv=B_KV_HEADS, group=GQA_GROUP, k_col=B_HEADS)
    o = _flash2(proj3, tabs, far, sel, n_kv=B_KV_HEADS, group=GQA_GROUP, q_col=0, k_col=B_HEADS,
               v_col=B_HEADS + B_KV_HEADS, sel_blk=B_BLOCK, sel_per_head=True)
    x = _outproj([o.reshape(m, -1)], [], w_out, slot, x, "plain", _row_tile(m, 512), 512)
    return x, proj3, proj_s


def _cmp_weights(cmp_pe, cmp_w1, cmp_w2, slot):
    half = C_CMP_STRIDE * HEAD_DIM
    w1 = cmp_w1[slot]
    w1cat = jnp.concatenate([w1[:, :half], w1[:, half:]], axis=2).astype(BF16)
    pe8 = jnp.broadcast_to(cmp_pe[slot].reshape(2, 1, C_CMP_BLOCK * HEAD_DIM), (2, 8, C_CMP_BLOCK * HEAD_DIM))
    return w1cat, pe8, w1, cmp_w2[slot]


def _mixer_c_prompt(x, xs, b, g_mix, li, w_in, w_gate, w_out, slot, cmp_w, tabs, tabs_win, far, rel_bias):
    m = x.shape[0]
    s = m // b
    n_main = (C_HEADS + 6 * C_KV_HEADS) * HEAD_DIM
    proj, proj_s = _norm_matmul(x, g_mix, li, w_in, slot, n_main, _row_tile(m, PROMPT_ROW_TILE), 512, xs=xs)
    gate, gate_s = _norm_matmul(x, g_mix, li, w_gate, 0, HEAD_DIM, _row_tile(m, 512), HEAD_DIM, xs=xs)
    proj3 = proj.reshape(b, s, n_main)
    w1cat, pe8, w1, w2 = cmp_w
    y = _cmp1_prompt(proj3, w1cat, col0=C_HEADS)
    kvc = _cmp2(y, pe8, w1, w2)
    n_seg = kvc.shape[1]
    o_cmp, sel = _nsa_cmp_select(proj3, kvc, rel_bias, far, n_kv=C_KV_HEADS, group=GQA_GROUP)
    kv0 = C_HEADS
    o_slc = _flash2(proj3, tabs, far, sel, n_kv=C_KV_HEADS, group=GQA_GROUP, q_col=0, k_col=kv0 + 2 * C_KV_HEADS,
                   v_col=kv0 + 3 * C_KV_HEADS, sel_blk=C_SLC_BLOCK)
    o_win = _flash2(proj3, tabs_win, far, None, n_kv=C_KV_HEADS, group=GQA_GROUP, q_col=0, k_col=kv0 + 4 * C_KV_HEADS,
                   v_col=kv0 + 5 * C_KV_HEADS, window=C_WINDOW)
    x = _outproj([o_cmp.reshape(m, -1), o_slc.reshape(m, -1), o_win.reshape(m, -1)], [gate], w_out, slot, x,
                 "gate3", _row_tile(m, 512), 512)
    return x, proj3, proj_s, gate_s


def _a_decode_kernel(qkv_ref, buf0_ref, buf1_ref, buf2_ref, bias_ref, o_ref):
    outs, lses = [], []
    for g, buf in enumerate((buf0_ref, buf1_ref, buf2_ref)):
        q, k_new, v_new = qkv_ref[0, g, 0], qkv_ref[0, g, 1], qkv_ref[0, g, 2]
        kb, vb = buf[:, 0], buf[:, 1]
        n = kb.shape[0]
        s = jnp.sum(q[None] * kb, axis=-1, keepdims=True) * SOFTMAX_SCALE + bias_ref[g, 0:n]
        s_new = jnp.sum(q * k_new, axis=-1, keepdims=True) * SOFTMAX_SCALE + bias_ref[g, n]
        m = jnp.maximum(jnp.max(s, axis=0), s_new)
        e = jnp.exp(s - m[None])
        e_new = jnp.exp(s_new - m)
        l = jnp.sum(e, axis=0) + e_new
        outs.append((jnp.sum(e * vb, axis=0) + e_new * v_new) / l)
        lses.append(m + jnp.log(l))
    mx = jnp.maximum(jnp.maximum(lses[0], lses[1]), lses[2])
    w = [jnp.exp(l - mx) for l in lses]
    o_ref[0] = (w[0] * outs[0] + w[1] * outs[1] + w[2] * outs[2]) / (w[0] + w[1] + w[2])


def _a_decode(qkv, bufs, slot, bias):
    n_dec = qkv.shape[0]
    in_specs = [pl.BlockSpec((1, 3, 3, A_HEADS, HEAD_DIM), lambda i: (i, 0, 0, 0, 0))]
    views = []
    for (window, dil), buf in zip(A_PATTERNS, bufs):
        n = window // dil
        assert buf.shape[2] == window, "window buffers must hold exactly `window` past rows"
        views.append(buf.reshape(buf.shape[0], n_dec, n, dil, 2, A_HEADS, HEAD_DIM))
        in_specs.append(pl.BlockSpec((None, None, n, None, 2, A_HEADS, HEAD_DIM),
                                     lambda i: (slot, i, 0, 0, 0, 0, 0)))
    in_specs.append(pl.BlockSpec(bias.shape, lambda i: (0, 0, 0, 0)))
    return pl.pallas_call(
        _a_decode_kernel,
        grid=(n_dec,),
        in_specs=in_specs,
        out_specs=pl.BlockSpec((1, A_HEADS, HEAD_DIM), lambda i: (i, 0, 0)),
        out_shape=jax.ShapeDtypeStruct((n_dec, A_HEADS, HEAD_DIM), F32),
        compiler_params=_params(1),
        name="a_decode",
    )(qkv, *views, bias)


DECODE_CHUNK = 2048


def _head_rows_store(o_ref, o_all, l_row):
    l_col = jnp.broadcast_to(l_row, (HEAD_DIM, HEAD_DIM)).T
    for g in range(B_KV_HEADS):
        rows = slice(g * GQA_GROUP, (g + 1) * GQA_GROUP)
        o_ref[0, rows, :] = o_all[rows, g * HEAD_DIM:(g + 1) * HEAD_DIM] / l_col[rows, :]


def _paged_decode_kernel(pt_ref, *refs, n_pg, n_sub, k_sub, v_sub, blk, topk, n_pages):
    page_refs = refs[:n_pg]
    qbd_ref, new_ref, bias_ref = refs[n_pg:n_pg + 3]
    rest = refs[n_pg + 3:]
    if topk is None:
        sel_ref, o_ref, st_ref, vs_ref, sel_s = rest
    else:
        o_ref, st_ref, vs_ref, sel_s = rest
    j = pl.program_id(1)
    past = n_pages * PAGE_SIZE
    n_blk = past // blk
    ch = DECODE_CHUNK

    if topk is not None:
        @pl.when(j == 0)
        def _():
            sel_s[...] = jnp.zeros_like(sel_s)

    for p in range(n_pg):
        page = j * n_pg + p
        row0 = pl.multiple_of(page * PAGE_SIZE, PAGE_SIZE)
        s_acc = jnp.zeros((PAGE_SIZE, HEAD_DIM), F32)
        for g in range(B_KV_HEADS):
            kg = page_refs[p][pl.ds(k_sub + g, PAGE_SIZE, stride=n_sub), :].astype(BF16)
            s_acc = s_acc + _dot(kg, qbd_ref[0, g])
            vg = page_refs[p][pl.ds(v_sub + g, PAGE_SIZE, stride=n_sub), :].astype(BF16)
            vs_ref[pl.ds(row0, PAGE_SIZE), g * HEAD_DIM:(g + 1) * HEAD_DIM] = vg
        st_ref[pl.ds(row0, PAGE_SIZE), :] = s_acc
        if topk is not None:
            b_idx = page // (blk // PAGE_SIZE)
            sel_s[pl.ds(b_idx, 1), :] = sel_s[pl.ds(b_idx, 1), :] + jnp.sum(s_acc, axis=0, keepdims=True)

    @pl.when(j == pl.num_programs(1) - 1)
    def _():
        s_new = jnp.zeros((8, HEAD_DIM), F32)
        for g in range(B_KV_HEADS):
            k_new = jnp.broadcast_to(new_ref[0, g:g + 1, :], (8, HEAD_DIM)).astype(BF16)
            s_new = s_new + _dot(k_new, qbd_ref[0, g])
            v_new = jnp.broadcast_to(new_ref[0, B_KV_HEADS + g:B_KV_HEADS + g + 1, :], (8, HEAD_DIM))
            vs_ref[past:past + 8, g * HEAD_DIM:(g + 1) * HEAD_DIM] = v_new.astype(BF16)
        l_new = s_new[0:1] * SOFTMAX_SCALE + bias_ref[ch + 1:ch + 2, :]

        if topk is not None:
            gate = sel_s[...] * (1.0 / blk)
            rank = _rank_rows(gate, n_blk)
            sel_s[...] = jnp.where(rank < topk, 1.0, 0.0)
        else:
            sel_s[...] = sel_ref[0, 0:n_blk, :]

        def masked_logits(c, bias):
            start = pl.multiple_of(c * ch, ch)
            lg = st_ref[pl.ds(start, ch), :] * SOFTMAX_SCALE + bias
            b0 = pl.multiple_of(c * (ch // blk), 8)
            sl = sel_s[pl.ds(b0, ch // blk), :]
            mk = jnp.broadcast_to(sl[:, None, :], (ch // blk, blk, HEAD_DIM)).reshape(ch, HEAD_DIM) > 0.5
            return jnp.where(mk, lg, NEG_INF), mk, start

        n_ch = past // ch
        far = bias_ref[ch:ch + 1, :]
        tail = bias_ref[0:ch, :]

        def max_body(c, m):
            lg, _, _ = masked_logits(c, far)
            return jnp.maximum(m, jnp.max(lg, axis=0, keepdims=True))

        m = lax.fori_loop(0, n_ch - 1, max_body, l_new)
        lg_t, mk_t, start_t = masked_logits(n_ch - 1, tail)
        m = jnp.maximum(m, jnp.max(lg_t, axis=0, keepdims=True))

        def acc_chunk(lg, mk, start, carry):
            l_sum, o_all = carry
            e = jnp.where(mk, jnp.exp(lg - m), 0.0)
            o_all = o_all + _dot(e.astype(BF16), vs_ref[pl.ds(start, ch), :], _TN)
            return l_sum + jnp.sum(e, axis=0, keepdims=True), o_all

        def acc_body(c, carry):
            lg, mk, start = masked_logits(c, far)
            return acc_chunk(lg, mk, start, carry)

        e_new = jnp.exp(l_new - m)
        row8 = lax.broadcasted_iota(jnp.int32, (8, HEAD_DIM), 0)
        e_new8 = jnp.where(row8 == 0, jnp.broadcast_to(e_new, (8, HEAD_DIM)), 0.0)
        o0 = _dot(e_new8.astype(BF16), vs_ref[past:past + 8, :], _TN)
        carry = lax.fori_loop(0, n_ch - 1, acc_body, (e_new, o0))
        l_sum, o_all = acc_chunk(lg_t, mk_t, start_t, carry)
        _head_rows_store(o_ref, o_all, l_sum)


def _paged_decode(page_table, pool2d, qbd, new_kv, bias, sel, *, n_pg, n_sub, k_sub, v_sub, blk, topk):
    n_dec, n_pages = page_table.shape
    past = n_pages * PAGE_SIZE
    assert past % DECODE_CHUNK == 0 and n_pages % n_pg == 0 and past % blk == 0
    n_blk = past // blk
    page_rows = PAGE_SIZE * n_sub
    in_specs = [pl.BlockSpec((page_rows, HEAD_DIM), functools.partial(lambda n, j, pt, p: (pt[n, j * n_pg + p], 0), p=p))
                for p in range(n_pg)]
    in_specs += [pl.BlockSpec((1, B_KV_HEADS, HEAD_DIM, HEAD_DIM), lambda n, j, pt: (n, 0, 0, 0)),
                 pl.BlockSpec((1, 2 * B_KV_HEADS, HEAD_DIM), lambda n, j, pt: (n, 0, 0)),
                 pl.BlockSpec(bias.shape, lambda n, j, pt: (0, 0))]
    args = [pool2d] * n_pg + [qbd, new_kv, bias]
    if topk is None:
        in_specs.append(pl.BlockSpec((1,) + sel.shape[1:], lambda n, j, pt: (n, 0, 0)))
        args.append(sel)
    return pl.pallas_call(
        functools.partial(_paged_decode_kernel, n_pg=n_pg, n_sub=n_sub, k_sub=k_sub, v_sub=v_sub, blk=blk,
                          topk=topk, n_pages=n_pages),
        grid_spec=pltpu.PrefetchScalarGridSpec(
            num_scalar_prefetch=1,
            grid=(n_dec, n_pages // n_pg),
            in_specs=in_specs,
            out_specs=pl.BlockSpec((1, B_HEADS, HEAD_DIM), lambda n, j, pt: (n, 0, 0)),
            scratch_shapes=[pltpu.VMEM((past + 8, HEAD_DIM), F32),
                            pltpu.VMEM((past + 8, B_KV_HEADS * HEAD_DIM), BF16),
                            pltpu.VMEM((n_blk, HEAD_DIM), F32)],
        ),
        out_shape=jax.ShapeDtypeStruct((n_dec, B_HEADS, HEAD_DIM), F32),
        compiler_params=_params(2),
        name="paged_decode",
    )(page_table, *args)


def _block_diag_q(q):
    n, h, d = q.shape
    qt = jnp.pad(jnp.swapaxes(q, 1, 2), ((0, 0), (0, 0), (0, HEAD_DIM - h)))
    grp = (np.arange(HEAD_DIM)[None, :] // GQA_GROUP == np.arange(B_KV_HEADS)[:, None]) & (np.arange(HEAD_DIM)[None, :] < h)
    return jnp.where(jnp.asarray(grp)[None, :, None, :], qt[:, None], 0.0).astype(BF16)


def _decode_bias(rel_bias, past, n_tail):
    dist = np.concatenate([n_tail - np.arange(n_tail), [MAX_DISTANCE * 8, 0], np.zeros(6, np.int64)])
    tab = rel_bias[jnp.asarray(_bucket_np(dist))]
    return jnp.pad(tab, ((0, 0), (0, HEAD_DIM - tab.shape[1])))


def _cmp1_paged_kernel(pt_ref, *refs, n_pg):
    page_refs = refs[:n_pg]
    w_ref, y_ref, ys_ref = refs[n_pg:]
    segs_per_page = PAGE_SIZE // C_CMP_STRIDE
    rows = segs_per_page * 8
    x2 = jnp.concatenate(
        [jnp.concatenate([ref[:, s, :, :].reshape(rows, HEAD_DIM) for s in range(C_CMP_STRIDE)], axis=1)
         for ref in page_refs], axis=0)
    y4 = _dot(x2.astype(BF16), w_ref[...])
    is_v = (lax.broadcasted_iota(jnp.int32, (x2.shape[0], 2 * HEAD_DIM), 0) % 8) >= C_KV_HEADS
    y = jnp.where(is_v, y4[:, 2 * HEAD_DIM:], y4[:, :2 * HEAD_DIM])
    ys_ref[0] = y[:, :HEAD_DIM]
    ys_ref[1] = y[:, HEAD_DIM:]
    for sub in range(8):
        y_ref[sub] = jnp.concatenate(
            [ys_ref[half, pl.ds(sub, n_pg * segs_per_page, stride=8), :] for half in range(2)], axis=1)


def _cmp1_paged(page_table, pool4d, w1cat4, *, sub_blk, n_pg):
    n_dec, n_pages = page_table.shape
    segs_per_page = PAGE_SIZE // C_CMP_STRIDE
    n_seg = n_pages * segs_per_page
    in_specs = [pl.BlockSpec((segs_per_page, C_CMP_STRIDE, 8, HEAD_DIM),
                             functools.partial(lambda n, j, pt, p: (pt[n, j * n_pg + p], 0, sub_blk, 0), p=p))
                for p in range(n_pg)]
    in_specs.append(pl.BlockSpec(w1cat4.shape, lambda n, j, pt: (0, 0)))
    return pl.pallas_call(
        functools.partial(_cmp1_paged_kernel, n_pg=n_pg),
        grid_spec=pltpu.PrefetchScalarGridSpec(
            num_scalar_prefetch=1,
            grid=(n_dec, n_pages // n_pg),
            in_specs=in_specs,
            out_specs=pl.BlockSpec((8, n_pg * segs_per_page, 2 * HEAD_DIM), lambda n, j, pt: (n, j, 0)),
            scratch_shapes=[pltpu.VMEM((2, n_pg * segs_per_page * 8, HEAD_DIM), F32)],
        ),
        out_shape=jax.ShapeDtypeStruct((n_dec * 8, n_seg, 2 * HEAD_DIM), F32),
        compiler_params=_params(2),
        name="cmp1_paged",
    )(page_table, *([pool4d] * n_pg), w1cat4)


def _nsa_decode_kernel(kvc_ref, qbd_ref, bias_c_ref, map_ref, gsum_ref, win_ref, new_ref, bias_w_ref,
                       oc_ref, ow_ref, sel_ref, *, n_cmp, n_blk, n_sub_w, k_sub_w):
    n_seg = kvc_ref.shape[1]
    sc = jnp.zeros((n_seg, HEAD_DIM), F32)
    for g in range(C_KV_HEADS):
        sc = sc + _dot(kvc_ref[g].astype(BF16), qbd_ref[0, g])
    valid = lax.broadcasted_iota(jnp.int32, (n_seg, HEAD_DIM), 0) < n_cmp
    lg = jnp.where(valid, sc * SOFTMAX_SCALE + bias_c_ref[...], NEG_INF)
    m = jnp.max(lg, axis=0, keepdims=True)
    e = jnp.where(valid, jnp.exp(lg - m), 0.0)
    p = e / jnp.maximum(jnp.sum(e, axis=0, keepdims=True), 1e-30)
    vc = jnp.concatenate([kvc_ref[C_KV_HEADS + g].astype(BF16) for g in range(C_KV_HEADS)], axis=1)
    _head_rows_store(oc_ref, _dot(p.astype(BF16), vc, _TN), jnp.ones((1, HEAD_DIM), F32))
    imp_h = _dot_f32_rhs(map_ref[...], p)
    imp = _dot_f32_lhs(imp_h, gsum_ref[...])
    blk = lax.broadcasted_iota(jnp.int32, imp.shape, 0)
    cur = n_blk - 1
    avail = blk <= cur
    forced = (blk == 0) | (blk == cur) | (blk == cur - 1)
    score = jnp.where(forced, jnp.inf, imp)
    score = jnp.where(avail, score, -jnp.inf)
    rank = _rank_rows(score, n_blk)
    sel_ref[0] = jnp.where(avail & (rank < min(C_SLC_TOPK, n_blk)), 1.0, 0.0)
    n_win = bias_w_ref.shape[0] - 8
    sw = jnp.zeros((n_win, HEAD_DIM), F32)
    s_new = jnp.zeros((8, HEAD_DIM), F32)
    v_parts, vn_parts = [], []
    for g in range(C_KV_HEADS):
        kg = win_ref[pl.ds(k_sub_w + g, n_win, stride=n_sub_w), :].astype(BF16)
        sw = sw + _dot(kg, qbd_ref[0, g])
        v_parts.append(win_ref[pl.ds(k_sub_w + C_KV_HEADS + g, n_win, stride=n_sub_w), :].astype(BF16))
        k_new = jnp.broadcast_to(new_ref[0, g:g + 1, :], (8, HEAD_DIM)).astype(BF16)
        s_new = s_new + _dot(k_new, qbd_ref[0, g])
        vn_parts.append(jnp.broadcast_to(new_ref[0, C_KV_HEADS + g:C_KV_HEADS + g + 1, :], (8, HEAD_DIM)).astype(BF16))
    lw = sw * SOFTMAX_SCALE + bias_w_ref[0:n_win, :]
    l_new = s_new[0:1] * SOFTMAX_SCALE + bias_w_ref[n_win + 1:n_win + 2, :]
    mw = jnp.maximum(jnp.max(lw, axis=0, keepdims=True), l_new)
    ew = jnp.exp(lw - mw)
    e_new = jnp.exp(l_new - mw)
    row8 = lax.broadcasted_iota(jnp.int32, (8, HEAD_DIM), 0)
    e_new8 = jnp.where(row8 == 0, jnp.broadcast_to(e_new, (8, HEAD_DIM)), 0.0)
    o_all = _dot(ew.astype(BF16), jnp.concatenate(v_parts, axis=1), _TN)
    o_all = o_all + _dot(e_new8.astype(BF16), jnp.concatenate(vn_parts, axis=1), _TN)
    _head_rows_store(ow_ref, o_all, jnp.sum(ew, axis=0, keepdims=True) + e_new)


def _nsa_decode(kvc, qbd, bias_c, cmap_t, gsum, win2d, new_win, bias_w, *, slot, n_blk):
    n_dec = qbd.shape[0]
    n_seg = kvc.shape[1]
    n_win = bias_w.shape[0] - 8
    n_blk_pad = cmap_t.shape[0]
    win_rows = n_win * 8
    o_sds = jax.ShapeDtypeStruct((n_dec, C_HEADS, HEAD_DIM), F32)
    o_spec = pl.BlockSpec((1, C_HEADS, HEAD_DIM), lambda n: (n, 0, 0))
    return pl.pallas_call(
        functools.partial(_nsa_decode_kernel, n_cmp=n_seg - 1, n_blk=n_blk, n_sub_w=8, k_sub_w=0),
        grid=(n_dec,),
        in_specs=[pl.BlockSpec((8, n_seg, HEAD_DIM), lambda n: (n, 0, 0)),
                  pl.BlockSpec((1, C_KV_HEADS, HEAD_DIM, HEAD_DIM), lambda n: (n, 0, 0, 0)),
                  pl.BlockSpec(bias_c.shape, lambda n: (0, 0)),
                  pl.BlockSpec(cmap_t.shape, lambda n: (0, 0)),
                  pl.BlockSpec(gsum.shape, lambda n: (0, 0)),
                  pl.BlockSpec((win_rows, HEAD_DIM), lambda n: (slot * n_dec + n, 0)),
                  pl.BlockSpec((1, 2 * C_KV_HEADS, HEAD_DIM), lambda n: (n, 0, 0)),
                  pl.BlockSpec(bias_w.shape, lambda n: (0, 0))],
        out_specs=[o_spec, o_spec, pl.BlockSpec((1, n_blk_pad, HEAD_DIM), lambda n: (n, 0, 0))],
        out_shape=[o_sds, o_sds, jax.ShapeDtypeStruct((n_dec, n_blk_pad, HEAD_DIM), F32)],
        compiler_params=_params(1),
        name="nsa_decode",
    )(kvc, qbd, bias_c, cmap_t, gsum, win2d, new_win, bias_w)


def _mixer_a_decode(x, proj, w_out, slot, bufs, rel_bias):
    n_dec = x.shape[0]
    qkv = proj.reshape(n_dec, len(A_PATTERNS), 3, A_HEADS, HEAD_DIM)
    dist = np.stack([np.concatenate([(w // d - np.arange(w // d)) * d, [0]]) for w, d in A_PATTERNS])
    bias = _bias_by_dist(rel_bias[:, :A_HEADS], dist)
    bias = jnp.broadcast_to(jnp.transpose(bias, (1, 2, 0))[..., None], dist.shape + (A_HEADS, HEAD_DIM))
    o = _a_decode(qkv, bufs, slot, bias)
    return _outproj([o.reshape(n_dec, -1)], [], w_out, slot, x, "plain", n_dec, 512)


def _mixer_b_decode(x, proj, w_out, slot, pool, page_table, bias_dec):
    n_dec = x.shape[0]
    nq = B_HEADS * HEAD_DIM
    qbd = _block_diag_q(proj[:, :nq].reshape(n_dec, B_HEADS, HEAD_DIM))
    new_kv = proj[:, nq:].reshape(n_dec, 2 * B_KV_HEADS, HEAD_DIM)
    n_sub = pool.shape[2] * 2 * B_KV_HEADS
    o = _paged_decode(page_table, pool.reshape(-1, HEAD_DIM), qbd, new_kv, bias_dec, None, n_pg=8, n_sub=n_sub,
                      k_sub=slot * 2 * B_KV_HEADS, v_sub=slot * 2 * B_KV_HEADS + B_KV_HEADS, blk=B_BLOCK, topk=B_TOPK)
    return _outproj([o.reshape(n_dec, -1)], [], w_out, slot, x, "plain", n_dec, 512)


def _mixer_c_decode(x, proj, gate, w_out, slot, cmp_w, pool, win_buf, page_table, bias_dec, rel_bias):
    n_dec = x.shape[0]
    n_pages = page_table.shape[1]
    past = n_pages * PAGE_SIZE
    assert win_buf.shape[2] == C_WINDOW, "window buffer must hold exactly C_WINDOW past rows"
    nq = C_HEADS * HEAD_DIM
    kvw = C_KV_HEADS * HEAD_DIM
    qbd = _block_diag_q(proj[:, :nq].reshape(n_dec, C_HEADS, HEAD_DIM))
    new_slc = proj[:, nq + 2 * kvw:nq + 4 * kvw].reshape(n_dec, 2 * C_KV_HEADS, HEAD_DIM)
    new_win = proj[:, nq + 4 * kvw:nq + 6 * kvw].reshape(n_dec, 2 * C_KV_HEADS, HEAD_DIM)
    w1cat, pe8, w1, w2 = cmp_w
    w1cat4 = jnp.concatenate([w1cat[0], w1cat[1]], axis=1)
    n_layers = pool.shape[2]
    n_sub = n_layers * 4 * C_KV_HEADS
    pool4d = pool.reshape(-1, C_CMP_STRIDE, n_sub, HEAD_DIM)
    y = _cmp1_paged(page_table, pool4d, w1cat4, sub_blk=slot * 2, n_pg=8)
    kvc = _cmp2(y, pe8, w1, w2)
    n_seg = kvc.shape[1]
    n_cmp = n_seg - 1
    n_blk = past // C_SLC_BLOCK + 1
    n_blk_pad = -(-n_blk // 8) * 8
    dist_c = past - (np.arange(n_seg) * C_CMP_STRIDE + (C_CMP_BLOCK - 1))
    bias_c = jnp.pad(rel_bias[jnp.asarray(_bucket_np(dist_c))], ((0, 0), (0, HEAD_DIM - C_HEADS)))
    cmap_t = jnp.asarray(_cmp_map_np(n_seg, n_cmp, n_blk_pad).T, BF16)
    lanes = np.arange(HEAD_DIM)
    gsum = jnp.asarray((lanes[:, None] // GQA_GROUP == lanes[None, :] // GQA_GROUP)
                       & (lanes[:, None] < C_HEADS) & (lanes[None, :] < C_HEADS), BF16)
    bias_w = _decode_bias(rel_bias, past, C_WINDOW)
    o_cmp, o_win, sel = _nsa_decode(kvc, qbd, bias_c, cmap_t, gsum, win_buf.reshape(-1, HEAD_DIM), new_win, bias_w,
                                    slot=slot, n_blk=n_blk)
    o_slc = _paged_decode(page_table, pool.reshape(-1, HEAD_DIM), qbd, new_slc, bias_dec, sel, n_pg=4, n_sub=n_sub,
                          k_sub=slot * 16 + 2 * C_KV_HEADS, v_sub=slot * 16 + 3 * C_KV_HEADS, blk=C_SLC_BLOCK, topk=None)
    return _outproj([o_cmp.reshape(n_dec, -1), o_slc.reshape(n_dec, -1), o_win.reshape(n_dec, -1)], [gate], w_out, slot,
                    x, "gate3", n_dec, 512)


SHIFT_CHUNK_ROWS = 8192


def _shift_kernel(cur_ref, nxt_ref, new_ref, out_ref, *, row):
    r = cur_ref.shape[1]
    out_ref[0, 0:r - row, :] = cur_ref[0, row:r, :]
    last = pl.program_id(1) == pl.num_programs(1) - 1

    @pl.when(last)
    def _():
        out_ref[0, r - row:r, :] = new_ref[0]

    @pl.when(jnp.logical_not(last))
    def _():
        out_ref[0, r - row:r, :] = nxt_ref[0]


def _shifted(buf, new_rows):
    n_l, n_seq, w = buf.shape[:3]
    row = math.prod(buf.shape[3:]) // HEAD_DIM
    assert row % 8 == 0
    total = w * row
    r = min(SHIFT_CHUNK_ROWS, total)
    assert total % r == 0 and r % row == 0
    n_chunks = total // r
    per_chunk = r // row
    out = pl.pallas_call(
        functools.partial(_shift_kernel, row=row),
        grid=(n_l * n_seq, n_chunks),
        in_specs=[pl.BlockSpec((1, r, HEAD_DIM), lambda i, c: (i, c, 0)),
                  pl.BlockSpec((1, row, HEAD_DIM), lambda i, c: (i, jnp.minimum((c + 1) * per_chunk, w - 1), 0)),
                  pl.BlockSpec((1, row, HEAD_DIM), lambda i, c: (i, 0, 0))],
        out_specs=pl.BlockSpec((1, r, HEAD_DIM), lambda i, c: (i, c, 0)),
        out_shape=jax.ShapeDtypeStruct((n_l * n_seq, total, HEAD_DIM), buf.dtype),
        compiler_params=_params(2),
        name="shift_window",
    )(buf.reshape(n_l * n_seq, total, HEAD_DIM), buf.reshape(n_l * n_seq, total, HEAD_DIM),
      new_rows.reshape(n_l * n_seq, row, HEAD_DIM))
    return out.reshape(buf.shape)


def kernel(x_prompt, x_sample, state_a_w128, state_a_w512, state_a_w2048, cache_b_kv, cache_c_kv, state_c_win, page_table, rel_bias, norm_mix, norm_ffn, norm_final, w_in_a, w_out_a, w_in_b, w_out_b, w_in_c, w_out_c, cmp_pe, cmp_w1, cmp_w2, w_up, w_down):
    b, s, d = x_prompt.shape
    n_dec, t_dec, _ = x_sample.shape
    depth = norm_mix.shape[0]
    assert t_dec == 1 and s % B_BLOCK == 0 and s % ATTN_TILE == 0
    past = page_table.shape[1] * PAGE_SIZE
    assert past % B_BLOCK == 0 and past % C_SLC_BLOCK == 0

    xp = x_prompt.reshape(b * s, d)
    xs = x_sample.reshape(n_dec, d)
    g_mix = norm_mix.reshape(depth, 1, d)
    g_ffn = norm_ffn.reshape(depth, 1, d)
    g_fin = norm_final.reshape(1, d)
    tabs = _tile_tables(rel_bias, ATTN_TILE)
    tabs_win = _tile_tables(rel_bias, ATTN_TILE, window=C_WINDOW)
    tabs_a = [_tile_tables(rel_bias[:, :A_HEADS], ATTN_TILE, window=w // dl, dist_scale=dl) for w, dl in A_PATTERNS]
    far = rel_bias[NUM_BUCKETS - 1]
    bias_dec = _decode_bias(rel_bias, past, DECODE_CHUNK)
    n_gate = 3 * C_HEADS
    n_main_c = (C_HEADS + 6 * C_KV_HEADS) * HEAD_DIM
    w_gate = jnp.pad(w_in_c[:, :, n_main_c:], ((0, 0), (0, 0), (0, HEAD_DIM - n_gate)))
    a_bufs = (state_a_w128, state_a_w512, state_a_w2048)
    w_in_a, w_in_b, w_in_c = w_in_a.astype(BF16), w_in_b.astype(BF16), w_in_c.astype(BF16)

    a_p, a_s = [[], [], []], [[], [], []]
    b_p, b_s, c_p, c_s, cw_p, cw_s = [], [], [], [], [], []
    for li in range(depth):
        kind, slot = li % 3, li // 3
        if kind == 0:
            xp, pp, ps = _mixer_a_prompt(xp, xs, b, g_mix, li, w_in_a, w_out_a, slot, tabs_a, far)
            xs = _mixer_a_decode(xs, ps, w_out_a, slot, a_bufs, rel_bias)
            gw = 3 * A_HEADS * HEAD_DIM
            kvw = 2 * A_HEADS * HEAD_DIM
            for g, (window, _) in enumerate(A_PATTERNS):
                c0 = g * gw + A_HEADS * HEAD_DIM
                a_p[g].append(pp[:, s - min(window, s):, c0:c0 + kvw].reshape(b, min(window, s), 2, A_HEADS, HEAD_DIM))
                a_s[g].append(ps[:, c0:c0 + kvw].reshape(n_dec, 2, A_HEADS, HEAD_DIM))
        elif kind == 1:
            xp, pp, ps = _mixer_b_prompt(xp, xs, b, g_mix, li, w_in_b, w_out_b, slot, tabs, far)
            xs = _mixer_b_decode(xs, ps, w_out_b, slot, cache_b_kv, page_table, bias_dec)
            nq = B_HEADS * HEAD_DIM
            b_p.append(pp[:, :, nq:].reshape(b, s, 2, B_KV_HEADS, HEAD_DIM))
            b_s.append(ps[:, nq:].reshape(n_dec, 1, 2, B_KV_HEADS, HEAD_DIM))
        else:
            cmp_w = _cmp_weights(cmp_pe, cmp_w1, cmp_w2, slot)
            xp, pp, ps, gate_s = _mixer_c_prompt(xp, xs, b, g_mix, li, w_in_c, w_gate[slot:slot + 1], w_out_c, slot, cmp_w,
                                                 tabs, tabs_win, far, rel_bias)
            xs = _mixer_c_decode(xs, ps, gate_s, w_out_c, slot, cmp_w, cache_c_kv, state_c_win, page_table, bias_dec,
                                 rel_bias)
            nq = C_HEADS * HEAD_DIM
            kvw = C_KV_HEADS * HEAD_DIM
            c_p.append(pp[:, :, nq:nq + 4 * kvw].reshape(b, s, 4, C_KV_HEADS, HEAD_DIM))
            c_s.append(ps[:, nq:nq + 4 * kvw].reshape(n_dec, 1, 4, C_KV_HEADS, HEAD_DIM))
            wlen = min(C_WINDOW, s)
            cw_p.append(pp[:, s - wlen:, nq + 4 * kvw:nq + 6 * kvw].reshape(b, wlen, 2, C_KV_HEADS, HEAD_DIM))
            cw_s.append(ps[:, nq + 4 * kvw:nq + 6 * kvw].reshape(n_dec, 2, C_KV_HEADS, HEAD_DIM))
        last = li == depth - 1
        xp, xs = _mlp(xp, xs, g_ffn, li, w_up, w_down, g_fin if last else None, _row_tile(b * s, PROMPT_ROW_TILE), 512)

    outs = [xp.reshape(b, s, d), xs.reshape(n_dec, 1, d)]
    for g in range(len(A_PATTERNS)):
        outs.append(jnp.stack(a_p[g], axis=0))
        outs.append(_shifted(a_bufs[g], jnp.stack(a_s[g], axis=0)))
    outs += [jnp.stack(b_p, axis=2), jnp.stack(b_s, axis=2), jnp.stack(c_p, axis=2), jnp.stack(c_s, axis=2),
             jnp.stack(cw_p, axis=0), _shifted(state_c_win, jnp.stack(cw_s, axis=0))]
    return tuple(outs)
```

```python
import functools
import math

import jax
import jax.numpy as jnp
import numpy as np
from jax import lax
from jax.experimental import pallas as pl
from jax.experimental.pallas import tpu as pltpu

F32 = jnp.float32
BF16 = jnp.bfloat16

HEAD_DIM = 128
RMS_EPS = 1e-6
NEG_INF = -1e30
SOFTMAX_SCALE = HEAD_DIM ** -0.5
LOG2E = math.log2(math.e)
PAGE_SIZE = 128

NUM_BUCKETS = 32
MAX_DISTANCE = 128

A_PATTERNS = ((128, 1), (512, 4), (2048, 16))
A_HEADS = 8
B_HEADS = 16
B_KV_HEADS = 4
B_BLOCK = 256
B_TOPK = 3
C_HEADS = 16
C_KV_HEADS = 4
C_CMP_BLOCK = 32
C_CMP_STRIDE = 16
C_SLC_BLOCK = 64
C_SLC_TOPK = 16
C_WINDOW = 512
GQA_GROUP = 4

ATTN_TILE = 256
PROMPT_ROW_TILE = 1024
VMEM_LIMIT_BYTES = 56 * 1024 * 1024


def _params(n_axes, vmem=VMEM_LIMIT_BYTES):
    return pltpu.CompilerParams(dimension_semantics=("arbitrary",) * n_axes, vmem_limit_bytes=vmem)


def _bucket_np(dist):
    exact = NUM_BUCKETS // 2
    d = np.maximum(np.asarray(dist, np.int64), 0)
    far = exact + (np.log(np.maximum(d, exact).astype(np.float64) / exact)
                   / math.log(MAX_DISTANCE / exact) * (NUM_BUCKETS - exact)).astype(np.int64)
    return np.where(d < exact, d, np.minimum(far, NUM_BUCKETS - 1)).astype(np.int32)


def _bias_by_dist(rel_bias, dist_np):
    tab = rel_bias[jnp.asarray(_bucket_np(dist_np))]
    return jnp.moveaxis(tab, -1, 0)


def _toeplitz(w, t):
    return jnp.tile(w, (1, t))[:, :2 * t * t].reshape(w.shape[0], t, 2 * t)[:, :, :t]


def _tile_tables(rel_bias, t, window=None, dil=1, dist_scale=1):
    assert window is None or window < t or (window % t == 0 and window >= 2 * t)
    k = np.arange(2 * t + 1)
    e = np.where(k <= t, -k, 2 * t + 1 - k)

    def table(dist, ok):
        vals = rel_bias[jnp.asarray(_bucket_np(dist * dist_scale))].T
        return _toeplitz(jnp.where(jnp.asarray(ok)[None], vals, NEG_INF), t)

    def allowed(dist):
        ok = dist >= 0
        if window is not None:
            ok &= dist <= window
        if dil > 1:
            ok &= dist % dil == 0
        return ok

    tabs = [table(e, allowed(e)), table(t + e, allowed(t + e))]
    if dil > 1:
        tabs.append(table(2 * t + e, (2 * t + e) % dil == 0))
    if window is not None and window >= 2 * t:
        tabs.append(table(window + e, allowed(window + e)))
    return jnp.stack(tabs, axis=1)


def _rms(x, g):
    return x * lax.rsqrt(jnp.mean(x * x, axis=-1, keepdims=True) + RMS_EPS) * g


def _split_bf16(x, parts):
    out = []
    for _ in range(parts):
        p = x.astype(BF16)
        out.append(p)
        x = x - p.astype(F32)
    return out


def _dot(a, b, dims=(((1,), (0,)), ((), ()))):
    return lax.dot_general(a, b, dims, preferred_element_type=F32)


_NT = (((1,), (1,)), ((), ()))
_TN = (((0,), (0,)), ((), ()))


def _dot_f32_lhs(a, b_bf16, dims=(((1,), (0,)), ((), ()))):
    return sum(_dot(p, b_bf16, dims) for p in _split_bf16(a, 3))


def _dot_f32_rhs(a_bf16, b, dims=(((1,), (0,)), ((), ()))):
    return sum(_dot(a_bf16, p, dims) for p in _split_bf16(b, 3))


def _dot_f32_both(a, b, dims):
    a_hi, a_lo = _split_bf16(a, 2)
    b_hi, b_lo = _split_bf16(b, 2)
    return _dot(a_hi, b_hi, dims) + (_dot(a_hi, b_lo, dims) + _dot(a_lo, b_hi, dims))


def _rank_rows(score, n_rows):
    row = lax.broadcasted_iota(jnp.int32, score.shape, 0)
    rank = jnp.zeros(score.shape, jnp.int32)
    for m in range(n_rows):
        sm = score[m:m + 1, :]
        beats = (sm > score) | ((sm == score) & (row > m))
        rank = rank + beats.astype(jnp.int32)
    return rank


def _rank_lanes(score, n_lanes):
    n_rows, width = score.shape
    n_pad = -(-n_lanes // 8) * 8
    rank_t = _rank_rows(score.T[:n_pad], n_lanes).astype(F32)
    if n_pad < width:
        rank_t = jnp.concatenate([rank_t, jnp.zeros((width - n_pad, n_rows), F32)], axis=0)
    return rank_t.T


def _norm_matmul_kernel(x_ref, g_ref, w_ref, o_ref, h_ref):
    @pl.when(pl.program_id(1) == 0)
    def _():
        h_ref[...] = _rms(x_ref[...], g_ref[...]).astype(BF16)

    o_ref[...] = _dot(h_ref[...], w_ref[...].astype(BF16))


def _norm_matmul2_kernel(x_ref, xs_ref, g_ref, w_ref, o_ref, os_ref, h_ref, hs_ref):
    i = pl.program_id(0)
    j = pl.program_id(1)

    @pl.when(j == 0)
    def _():
        h_ref[...] = _rms(x_ref[...], g_ref[...]).astype(BF16)

    @pl.when((i == 0) & (j == 0))
    def _():
        hs_ref[...] = _rms(xs_ref[...], g_ref[...]).astype(BF16)

    w = w_ref[...].astype(BF16)
    o_ref[...] = _dot(h_ref[...], w)

    @pl.when(i == 0)
    def _():
        os_ref[j] = _dot(hs_ref[...], w)


def _norm_matmul(x, g_all, li, w_all, slot, n_cols, tm, tn, xs=None):
    m, k = x.shape
    n_j = n_cols // tn
    x_spec = pl.BlockSpec((tm, k), lambda i, j: (i, 0), pipeline_mode=pl.Buffered(1))
    g_spec = pl.BlockSpec((None, 1, k), lambda i, j: (li, 0, 0))
    w_spec = pl.BlockSpec((None, k, tn), lambda i, j: (slot, 0, j))
    o_spec = pl.BlockSpec((tm, tn), lambda i, j: (i, j))
    if xs is None:
        return pl.pallas_call(
            _norm_matmul_kernel,
            grid=(m // tm, n_j),
            in_specs=[x_spec, g_spec, w_spec],
            out_specs=o_spec,
            out_shape=jax.ShapeDtypeStruct((m, n_cols), F32),
            scratch_shapes=[pltpu.VMEM((tm, k), BF16)],
            compiler_params=_params(2),
            name="norm_matmul",
        )(x, g_all, w_all)
    ms = xs.shape[0]
    out, out_s = pl.pallas_call(
        _norm_matmul2_kernel,
        grid=(m // tm, n_j),
        in_specs=[x_spec, pl.BlockSpec((ms, k), lambda i, j: (0, 0)), g_spec, w_spec],
        out_specs=[o_spec, pl.BlockSpec((n_j, ms, tn), lambda i, j: (0, 0, 0))],
        out_shape=[jax.ShapeDtypeStruct((m, n_cols), F32), jax.ShapeDtypeStruct((n_j, ms, tn), F32)],
        scratch_shapes=[pltpu.VMEM((tm, k), BF16), pltpu.VMEM((ms, k), BF16)],
        compiler_params=_params(2),
        name="norm_matmul2",
    )(x, xs, g_all, w_all)
    return out, jnp.swapaxes(out_s, 0, 1).reshape(ms, n_cols)


def _outproj_kernel(*refs, mode, n_heads):
    if mode == "plain":
        o_refs, extra, (w_ref, x_ref, out_ref, a_ref) = refs[:1], (), refs[1:]
    elif mode == "lse3":
        o_refs, extra, (w_ref, x_ref, out_ref, a_ref) = refs[:3], refs[3:6], refs[6:]
    else:
        o_refs, extra, (w_ref, x_ref, out_ref, a_ref) = refs[:3], refs[3:4], refs[4:]

    @pl.when(pl.program_id(1) == 0)
    def _():
        if mode == "plain":
            a = o_refs[0][...]
        elif mode == "lse3":
            lse = [r[...] for r in extra]
            mx = jnp.maximum(jnp.maximum(lse[0], lse[1]), lse[2])
            e = [jnp.exp(l - mx) for l in lse]
            den = e[0] + e[1] + e[2]
            a = sum((e[g] / den) * o_refs[g][...] for g in range(3))
        else:
            gate = jax.nn.sigmoid(extra[0][...])
            tm = gate.shape[0]
            heads = []
            for h in range(n_heads):
                acc = jnp.zeros((tm, HEAD_DIM), F32)
                for br in range(3):
                    gcol = gate[:, 3 * h + br:3 * h + br + 1]
                    acc = acc + gcol * o_refs[br][:, h * HEAD_DIM:(h + 1) * HEAD_DIM]
                heads.append(acc)
            a = jnp.concatenate(heads, axis=1)
        a_ref[...] = a.astype(BF16)

    out_ref[...] = x_ref[...] + _dot(a_ref[...], w_ref[...].astype(BF16))


def _outproj(o_list, extra_list, w_all, slot, x, mode, tm, tn):
    m, k = o_list[0].shape
    n = x.shape[1]
    row_spec = lambda width: pl.BlockSpec((tm, width), lambda i, j: (i, 0))
    in_specs = [row_spec(k) for _ in o_list] + [row_spec(e.shape[1]) for e in extra_list]
    in_specs += [pl.BlockSpec((None, k, tn), lambda i, j: (slot, 0, j)),
                 pl.BlockSpec((tm, tn), lambda i, j: (i, j))]
    return pl.pallas_call(
        functools.partial(_outproj_kernel, mode=mode, n_heads=k // HEAD_DIM),
        grid=(m // tm, n // tn),
        in_specs=in_specs,
        out_specs=pl.BlockSpec((tm, tn), lambda i, j: (i, j)),
        out_shape=jax.ShapeDtypeStruct((m, n), F32),
        scratch_shapes=[pltpu.VMEM((tm, k), BF16)],
        compiler_params=_params(2),
        name="outproj_" + mode,
    )(*o_list, *extra_list, w_all, x)


def _mlp_kernel(*refs, final):
    if final:
        x_ref, xs_ref, g_ref, wu_ref, wd_ref, gf_ref, o_ref, os_ref, h_ref, hs_ref = refs
    else:
        x_ref, xs_ref, g_ref, wu_ref, wd_ref, o_ref, os_ref, h_ref, hs_ref = refs
    i = pl.program_id(0)
    f = pl.program_id(1)
    last = pl.num_programs(1) - 1

    @pl.when(f == 0)
    def _():
        x = x_ref[...]
        h_ref[...] = _rms(x, g_ref[...]).astype(BF16)
        o_ref[...] = x

    @pl.when((i == 0) & (f == 0))
    def _():
        xs = xs_ref[...]
        hs_ref[...] = _rms(xs, g_ref[...]).astype(BF16)
        os_ref[...] = xs

    wu = wu_ref[...].astype(BF16)
    wd = wd_ref[...].astype(BF16)

    def up_act_down(h):
        u = _dot(h, wu)
        return _dot(jnp.square(jnp.maximum(u, 0.0)).astype(BF16), wd)

    o_ref[...] += up_act_down(h_ref[...])

    @pl.when(i == 0)
    def _():
        os_ref[...] += up_act_down(hs_ref[...])

    if final:
        @pl.when(f == last)
        def _():
            o_ref[...] = _rms(o_ref[...], gf_ref[...])

        @pl.when((i == 0) & (f == last))
        def _():
            os_ref[...] = _rms(os_ref[...], gf_ref[...])


def _mlp(x, xs, g_all, li, w_up, w_down, final_g, tm, tf):
    m, d = x.shape
    ms = xs.shape[0]
    d_ff = w_up.shape[2]
    in_specs = [
        pl.BlockSpec((tm, d), lambda i, f: (i, 0), pipeline_mode=pl.Buffered(1)),
        pl.BlockSpec((ms, d), lambda i, f: (0, 0)),
        pl.BlockSpec((None, 1, d), lambda i, f: (li, 0, 0)),
        pl.BlockSpec((None, d, tf), lambda i, f: (li, 0, f)),
        pl.BlockSpec((None, tf, d), lambda i, f: (li, f, 0)),
    ]
    args = [x, xs, g_all, w_up, w_down]
    if final_g is not None:
        in_specs.append(pl.BlockSpec((1, d), lambda i, f: (0, 0)))
        args.append(final_g)
    return pl.pallas_call(
        functools.partial(_mlp_kernel, final=final_g is not None),
        grid=(m // tm, d_ff // tf),
        in_specs=in_specs,
        out_specs=[pl.BlockSpec((tm, d), lambda i, f: (i, 0)), pl.BlockSpec((ms, d), lambda i, f: (0, 0))],
        out_shape=[jax.ShapeDtypeStruct((m, d), F32), jax.ShapeDtypeStruct((ms, d), F32)],
        scratch_shapes=[pltpu.VMEM((tm, d), BF16), pltpu.VMEM((ms, d), BF16)],
        compiler_params=_params(2),
        name="mlp",
    )(*args)


def _flash2_kernel(*refs, group, t, w_tiles, far_tab, edge_tab, sel_blk, sel_per_head, head0):
    it = iter(refs)
    q_ref, k_ref, v_ref, tab_ref, far_ref = (next(it) for _ in range(5))
    sel_ref = next(it) if sel_blk else None
    o_ref = next(it)
    kb_ref, vb_ref, qs_ref, s_ref, mx_ref, l_ref, acc_ref = (next(it) for _ in range(7))

    kvh = pl.program_id(1)
    qi = pl.program_id(2)
    rows = group * t

    @pl.when(qi == 0)
    def _():
        kb_ref[...] = k_ref[0].astype(BF16)
        vb_ref[...] = v_ref[0].astype(BF16)

    for j in range(group):
        qs_ref[j * t:(j + 1) * t, :] = q_ref[0, :, j * HEAD_DIM:(j + 1) * HEAD_DIM].astype(BF16)

    def scores(tiles, tab, first=False):
        per_head = [[] for _ in range(group)]
        for kj in tiles:
            start = pl.multiple_of(kj * t, t)
            k = kb_ref[pl.ds(start, t), :]
            sel_add = None
            if sel_blk and not sel_per_head:
                blk_of_key = (start + lax.broadcasted_iota(jnp.int32, (HEAD_DIM, t), 1)) // sel_blk
                expand = lax.broadcasted_iota(jnp.int32, (HEAD_DIM, t), 0) == blk_of_key
                expand = jnp.where(expand, 1.0, 0.0).astype(BF16)
                sel_add = _dot(sel_ref[0, 0].astype(BF16), expand)
            for j in range(group):
                hs = slice(j * t, (j + 1) * t)
                sj = _dot(qs_ref[hs, :], k, _NT) * (SOFTMAX_SCALE * LOG2E)
                sj = sj + (far_ref[head0 + kvh * group + j] if tab is None else tab_ref[j, tab])
                if sel_blk and sel_per_head:
                    assert sel_blk == t
                    lane = lax.broadcasted_iota(jnp.int32, (t, HEAD_DIM), 1)
                    sj = sj + jnp.sum(jnp.where(lane == kj, sel_ref[0, j], 0.0), axis=1, keepdims=True)
                elif sel_blk:
                    sj = sj + sel_add
                s_ref[kj, hs, :] = sj
                per_head[j].append(jnp.maximum(sj[:, :HEAD_DIM], sj[:, HEAD_DIM:]))
        for j in range(group):
            hs = slice(j * t, (j + 1) * t)
            mj = functools.reduce(jnp.maximum, per_head[j])
            mx_ref[hs, :] = mj if first else jnp.maximum(mx_ref[hs, :], mj)

    def accumulate(tiles, tab, first=False):
        for j in range(group):
            hs = slice(j * t, (j + 1) * t)
            m = mx_ref[hs, :]
            m2 = jnp.concatenate([m, m], axis=1)
            p_lane, pv = None, None
            for kj in tiles:
                v = vb_ref[pl.ds(pl.multiple_of(kj * t, t), t), :]
                p = jnp.exp2(s_ref[kj, hs, :] - m2)
                pl_ = p[:, :HEAD_DIM] + p[:, HEAD_DIM:]
                pv_ = _dot(p.astype(BF16), v)
                p_lane = pl_ if p_lane is None else p_lane + pl_
                pv = pv_ if pv is None else pv + pv_
            l_ref[hs, :] = p_lane if first else l_ref[hs, :] + p_lane
            acc_ref[hs, :] = pv if first else acc_ref[hs, :] + pv

    def sweep(visit):
        visit([qi], 0, first=True)

        @pl.when(qi >= 1)
        def _():
            visit([qi - 1], 1)

        def far_tiles(lo, hi):
            def pair_body(i, carry):
                visit([lo + 2 * i, lo + 2 * i + 1], far_tab)
                return carry

            n_far = jnp.maximum(hi - lo, 0)
            lax.fori_loop(0, n_far // 2, pair_body, 0)

            @pl.when(n_far % 2 == 1)
            def _():
                visit([hi - 1], far_tab)

        if w_tiles is None:
            far_tiles(0, qi - 1)
        elif w_tiles >= 2:
            far_tiles(jnp.maximum(qi - w_tiles + 1, 0), qi - 1)

            @pl.when(qi >= w_tiles)
            def _():
                visit([qi - w_tiles], edge_tab)

    sweep(scores)
    mx_ref[...] = jnp.broadcast_to(jnp.max(mx_ref[...], axis=1, keepdims=True), (rows, HEAD_DIM))
    sweep(accumulate)

    out = acc_ref[...] / jnp.sum(l_ref[...], axis=1, keepdims=True)
    for j in range(group):
        o_ref[0, :, j * HEAD_DIM:(j + 1) * HEAD_DIM] = out[j * t:(j + 1) * t]


def _flash2(proj, tabs, far, sel, *, n_kv, group, q_col, k_col, v_col, head0=0,
            window=None, sel_blk=0, sel_per_head=False):
    b, s, _ = proj.shape
    t = ATTN_TILE
    assert t == 2 * HEAD_DIM
    gw = group * HEAD_DIM
    n_tab = tabs.shape[1]
    w_tiles = None if window is None else window // t
    edge_tab = n_tab - 1 if (w_tiles is not None and w_tiles >= 2) else None
    tabs, far = tabs * LOG2E, far * LOG2E
    in_specs = [
        pl.BlockSpec((1, t, gw), lambda bi, h, qi: (bi, qi, q_col // group + h)),
        pl.BlockSpec((1, s, HEAD_DIM), lambda bi, h, qi: (bi, 0, k_col + h)),
        pl.BlockSpec((1, s, HEAD_DIM), lambda bi, h, qi: (bi, 0, v_col + h)),
        pl.BlockSpec((group, n_tab, t, t), lambda bi, h, qi: (head0 // group + h, 0, 0, 0)),
        pl.BlockSpec(memory_space=pltpu.SMEM),
    ]
    args = [proj, proj, proj, tabs, far]
    if sel_blk:
        if sel_per_head:
            in_specs.append(pl.BlockSpec((1, group, t, HEAD_DIM), lambda bi, h, qi: (bi, h, qi, 0)))
        else:
            in_specs.append(pl.BlockSpec((1, 1, t, HEAD_DIM), lambda bi, h, qi: (bi, h, qi, 0)))
        args.append(sel)
    rows = group * t
    return pl.pallas_call(
        functools.partial(_flash2_kernel, group=group, t=t, w_tiles=w_tiles, far_tab=None, edge_tab=edge_tab,
                          sel_blk=sel_blk, sel_per_head=sel_per_head, head0=head0),
        grid=(b, n_kv, s // t),
        in_specs=in_specs,
        out_specs=pl.BlockSpec((1, t, gw), lambda bi, h, qi: (bi, qi, h)),
        out_shape=jax.ShapeDtypeStruct((b, s, n_kv * gw), F32),
        scratch_shapes=[pltpu.VMEM((s, HEAD_DIM), BF16), pltpu.VMEM((s, HEAD_DIM), BF16),
                        pltpu.VMEM((rows, HEAD_DIM), BF16), pltpu.VMEM((s // t, rows, t), F32),
                        pltpu.VMEM((rows, HEAD_DIM), F32), pltpu.VMEM((rows, HEAD_DIM), F32),
                        pltpu.VMEM((rows, HEAD_DIM), F32)],
        compiler_params=_params(3),
        name="flash2",
    )(*args)


def _dilated_kernel(q_ref, k_ref, v_ref, tab_ref, o_ref, lse_ref, *, t, dil, n_q):
    def rows(r, tile_idx):
        return pl.ds(r + tile_idx * (t * dil), t, stride=dil)

    def lane_pair(op, x):
        return op(x[:, :HEAD_DIM], x[:, HEAD_DIM:])

    def one(idx):
        r = idx // n_q
        qi = idx % n_q
        q = q_ref[0, rows(r, qi), :].astype(BF16)
        tiles = [(qi, 0, None)]
        if n_q > 1:
            tiles.append((jnp.maximum(qi - 1, 0), 1, jnp.where(qi >= 1, 0.0, NEG_INF)))
        scores, values = [], []
        for kj, tab, extra in tiles:
            k = k_ref[0, rows(r, kj), :].astype(BF16)
            values.append(v_ref[0, rows(r, kj), :].astype(BF16))
            s = _dot(q, k, _NT) * SOFTMAX_SCALE + tab_ref[0, tab]
            scores.append(s if extra is None else s + extra)
        m_lane = lane_pair(jnp.maximum, scores[0])
        for s in scores[1:]:
            m_lane = jnp.maximum(m_lane, lane_pair(jnp.maximum, s))
        m = jnp.max(m_lane, axis=1, keepdims=True)
        l_lane, acc = None, None
        for s, v in zip(scores, values):
            p = jnp.exp(s - m)
            pl_ = lane_pair(jnp.add, p)
            pv = _dot(p.astype(BF16), v)
            l_lane = pl_ if l_lane is None else l_lane + pl_
            acc = pv if acc is None else acc + pv
        l = jnp.sum(l_lane, axis=1, keepdims=True)
        o_ref[0, rows(r, qi), :] = acc / l
        lse_ref[0, rows(r, qi), :] = jnp.broadcast_to(m + jnp.log(l), (t, HEAD_DIM))

    n_it = dil * n_q
    chains = 4 if n_it % 4 == 0 else 1

    def body(it, carry):
        for u in range(chains):
            one(it * chains + u)
        return carry

    lax.fori_loop(0, n_it // chains, body, 0)


def _dilated_attention(proj, tabs, *, n_heads, q_col, k_col, v_col, dil):
    b, s, _ = proj.shape
    t = ATTN_TILE
    assert t == 2 * HEAD_DIM and (s // dil) % t == 0
    spec = lambda col: pl.BlockSpec((1, s, HEAD_DIM), lambda bi, h: (bi, 0, col + h))
    out_sds = jax.ShapeDtypeStruct((b, s, n_heads * HEAD_DIM), F32)
    return pl.pallas_call(
        functools.partial(_dilated_kernel, t=t, dil=dil, n_q=s // dil // t),
        grid=(b, n_heads),
        in_specs=[spec(q_col), spec(k_col), spec(v_col),
                  pl.BlockSpec((1, 2, t, t), lambda bi, h: (h, 0, 0, 0))],
        out_specs=[spec(0), spec(0)],
        out_shape=[out_sds, out_sds],
        compiler_params=_params(2),
        name="dilated_attention",
    )(proj, proj, proj, tabs)


def _moba_select_kernel(q_ref, k_ref, sel_ref, kmean_ref, *, group, t, n_blk):
    qi = pl.program_id(2)

    @pl.when(qi == 0)
    def _():
        kmean_ref[...] = jnp.zeros_like(kmean_ref)
        s = k_ref.shape[1]
        km = jnp.sum(k_ref[0].reshape(n_blk, s // n_blk, HEAD_DIM), axis=1) * (1.0 / B_BLOCK)
        kmean_ref[0:n_blk, :] = km

    lane = lax.broadcasted_iota(jnp.int32, (t, HEAD_DIM), 1)
    pos = qi * t + lax.broadcasted_iota(jnp.int32, (t, HEAD_DIM), 0)
    own = pos // B_BLOCK
    past = lane < own
    kmean = kmean_ref[...]
    for j in range(group):
        q = q_ref[0, :, j * HEAD_DIM:(j + 1) * HEAD_DIM]
        gate = _dot_f32_both(q, kmean, _NT)
        gate = jnp.where(past, gate, NEG_INF)
        rank = _rank_lanes(gate, n_blk)
        sel = (past & (rank < B_TOPK)) | (lane == own)
        sel_ref[0, j] = jnp.where(sel, 0.0, NEG_INF)


def _moba_select(proj, *, n_kv, group, k_col):
    b, s, _ = proj.shape
    t = ATTN_TILE
    n_blk = s // B_BLOCK
    return pl.pallas_call(
        functools.partial(_moba_select_kernel, group=group, t=t, n_blk=n_blk),
        grid=(b, n_kv, s // t),
        in_specs=[pl.BlockSpec((1, t, group * HEAD_DIM), lambda bi, h, qi: (bi, qi, h)),
                  pl.BlockSpec((1, s, HEAD_DIM), lambda bi, h, qi: (bi, 0, k_col + h))],
        out_specs=pl.BlockSpec((1, group, t, HEAD_DIM), lambda bi, h, qi: (bi, h, qi, 0)),
        out_shape=jax.ShapeDtypeStruct((b, n_kv * group, s, HEAD_DIM), F32),
        scratch_shapes=[pltpu.VMEM((HEAD_DIM, HEAD_DIM), F32)],
        compiler_params=_params(3),
        name="moba_select",
    )(proj, proj)


def _cmp1_prompt_kernel(x_ref, w_ref, y_ref, *, n_seg):
    x2 = jnp.concatenate([x_ref[pl.ds(s, n_seg, stride=C_CMP_STRIDE), :] for s in range(C_CMP_STRIDE)], axis=1)
    y_ref[...] = _dot(x2.astype(BF16), w_ref[...])


def _cmp1_prompt(proj, w1cat, *, col0):
    b, s, _ = proj.shape
    n_seg = s // C_CMP_STRIDE
    return pl.pallas_call(
        functools.partial(_cmp1_prompt_kernel, n_seg=n_seg),
        grid=(b, 2, C_KV_HEADS),
        in_specs=[pl.BlockSpec((None, s, HEAD_DIM), lambda bi, c, g: (bi, 0, col0 + c * C_KV_HEADS + g)),
                  pl.BlockSpec((None, C_CMP_STRIDE * HEAD_DIM, 2 * HEAD_DIM), lambda bi, c, g: (c, 0, 0))],
        out_specs=pl.BlockSpec((None, n_seg, 2 * HEAD_DIM), lambda bi, c, g: ((bi * 2 + c) * C_KV_HEADS + g, 0, 0)),
        out_shape=jax.ShapeDtypeStruct((b * 2 * C_KV_HEADS, n_seg, 2 * HEAD_DIM), F32),
        compiler_params=_params(3),
        name="cmp1_prompt",
    )(proj, w1cat)


def _cmp2_kernel(y_ref, pe_ref, w1_ref, w2_ref, o_ref, *, n_seg):
    y = y_ref[0]
    first = y[:, :HEAD_DIM]
    second = pltpu.roll(y[:, HEAD_DIM:], n_seg - 1, 0)
    pe_term = _dot(pe_ref[0].astype(BF16), w1_ref[0].astype(BF16))[0:1]
    pre = first + second + pe_term
    tok = _dot(jnp.maximum(pre, 0.0).astype(BF16), w2_ref[0].astype(BF16))
    row = lax.broadcasted_iota(jnp.int32, tok.shape, 0)
    o_ref[0] = jnp.where(row < n_seg - 1, tok, 0.0)


def _cmp2(y, pe8, w1, w2):
    r, n_seg, _ = y.shape
    comp = lambda i: ((i // C_KV_HEADS) % 2, 0, 0)
    return pl.pallas_call(
        functools.partial(_cmp2_kernel, n_seg=n_seg),
        grid=(r,),
        in_specs=[pl.BlockSpec((1, n_seg, 2 * HEAD_DIM), lambda i: (i, 0, 0)),
                  pl.BlockSpec((1, 8, C_CMP_BLOCK * HEAD_DIM), comp),
                  pl.BlockSpec((1, C_CMP_BLOCK * HEAD_DIM, HEAD_DIM), comp),
                  pl.BlockSpec((1, HEAD_DIM, HEAD_DIM), comp)],
        out_specs=pl.BlockSpec((1, n_seg, HEAD_DIM), lambda i: (i, 0, 0)),
        out_shape=jax.ShapeDtypeStruct((r, n_seg, HEAD_DIM), F32),
        compiler_params=_params(1),
        name="cmp2",
    )(y, pe8, w1, w2)


CMP_BAND = 8


def _nsa_cmp_select_kernel(q_ref, kc_ref, vc_ref, bias_ref, far_ref, map_ref, oc_ref, sel_ref, *,
                           group, t, n_cmp, n_sel):
    kvh = pl.program_id(1)
    qi = pl.program_id(2)
    n_pad = kc_ref.shape[1]
    pos = qi * t + lax.broadcasted_iota(jnp.int32, (t, n_pad), 0)
    tok = lax.broadcasted_iota(jnp.int32, (t, n_pad), 1)
    valid = (pos - (tok * C_CMP_STRIDE + (C_CMP_BLOCK - 1)) >= 0) & (tok < n_cmp)
    band0 = qi * (t // C_CMP_STRIDE) - CMP_BAND
    shift = jnp.where(band0 < 0, band0 + n_pad, band0)
    is_far = tok < band0
    kc = kc_ref[0].astype(BF16)
    vc = vc_ref[0].astype(BF16)
    p_sum = jnp.zeros((t, n_pad), F32)
    for j in range(group):
        q = q_ref[0, :, j * HEAD_DIM:(j + 1) * HEAD_DIM].astype(BF16)
        bias = jnp.where(is_far, far_ref[kvh * group + j], pltpu.roll(bias_ref[j], shift, 1))
        s = _dot(q, kc, _NT) * SOFTMAX_SCALE + bias
        s = jnp.where(valid, s, NEG_INF)
        m = jnp.max(s, axis=1, keepdims=True)
        e = jnp.where(valid, jnp.exp(s - m), 0.0)
        p = e / jnp.maximum(jnp.sum(e, axis=1, keepdims=True), 1e-30)
        oc_ref[0, :, j * HEAD_DIM:(j + 1) * HEAD_DIM] = _dot(p.astype(BF16), vc)
        p_sum = p_sum + p
    imp = _dot_f32_lhs(p_sum, map_ref[...])
    blk = lax.broadcasted_iota(jnp.int32, (t, HEAD_DIM), 1)
    cur = (qi * t + lax.broadcasted_iota(jnp.int32, (t, HEAD_DIM), 0)) // C_SLC_BLOCK
    avail = (blk <= cur) & (blk < n_sel)
    forced = (blk == 0) | (blk == cur) | (blk == cur - 1)
    score = jnp.where(forced, jnp.inf, imp)
    score = jnp.where(avail, score, -jnp.inf)
    rank = _rank_lanes(score, n_sel)
    sel = avail & (rank < min(C_SLC_TOPK, n_sel))
    sel_ref[0, 0] = jnp.where(sel, 0.0, NEG_INF)


def _cmp_map_np(n_rows, n_cmp, n_cols):
    ratio = C_SLC_BLOCK // C_CMP_STRIDE
    m = np.zeros((n_rows, n_cols), np.float32)
    for i in range(n_cmp):
        for r in range(C_CMP_BLOCK // C_CMP_STRIDE):
            m[i, (i + r) // ratio] += 1.0
    return m


def _nsa_cmp_select(proj, kvc, rel_bias, far, *, n_kv, group):
    b, s, _ = proj.shape
    t = ATTN_TILE
    n_seg = kvc.shape[1]
    n_cmp = n_seg - 1
    n_sel = -(-s // C_SLC_BLOCK)
    cmap = jnp.asarray(_cmp_map_np(n_seg, n_cmp, HEAD_DIM), BF16)
    gw = group * HEAD_DIM
    n_band = CMP_BAND + 2 * (t // C_CMP_STRIDE) - (C_CMP_BLOCK - 1) // C_CMP_STRIDE
    assert n_band <= n_seg and C_CMP_STRIDE * (CMP_BAND + 1) - (C_CMP_BLOCK - 1) >= 113
    dist = np.arange(t)[:, None] - C_CMP_STRIDE * (np.arange(n_band)[None, :] - CMP_BAND) - (C_CMP_BLOCK - 1)
    bias_c = jnp.pad(_bias_by_dist(rel_bias, np.maximum(dist, 0)), ((0, 0), (0, 0), (0, n_seg - n_band)))
    return pl.pallas_call(
        functools.partial(_nsa_cmp_select_kernel, group=group, t=t, n_cmp=n_cmp, n_sel=n_sel),
        grid=(b, n_kv, s // t),
        in_specs=[pl.BlockSpec((1, t, gw), lambda bi, h, qi: (bi, qi, h)),
                  pl.BlockSpec((1, n_seg, HEAD_DIM), lambda bi, h, qi: (bi * 8 + h, 0, 0)),
                  pl.BlockSpec((1, n_seg, HEAD_DIM), lambda bi, h, qi: (bi * 8 + 4 + h, 0, 0)),
                  pl.BlockSpec((group, t, n_seg), lambda bi, h, qi: (h, 0, 0)),
                  pl.BlockSpec(memory_space=pltpu.SMEM),
                  pl.BlockSpec((n_seg, HEAD_DIM), lambda bi, h, qi: (0, 0))],
        out_specs=[pl.BlockSpec((1, t, gw), lambda bi, h, qi: (bi, qi, h)),
                   pl.BlockSpec((1, 1, t, HEAD_DIM), lambda bi, h, qi: (bi, h, qi, 0))],
        out_shape=[jax.ShapeDtypeStruct((b, s, n_kv * gw), F32),
                   jax.ShapeDtypeStruct((b, n_kv, s, HEAD_DIM), F32)],
        compiler_params=_params(3),
        name="nsa_cmp_select",
    )(proj, kvc, kvc, bias_c, far, cmap)


def _row_tile(m, pref):
    return pref if m % pref == 0 else m


def _mixer_a_prompt(x, xs, b, g_mix, li, w_in, w_out, slot, tabs, far):
    m = x.shape[0]
    s = m // b
    n_in = w_in.shape[2]
    proj, proj_s = _norm_matmul(x, g_mix, li, w_in, slot, n_in, _row_tile(m, PROMPT_ROW_TILE), 512, xs=xs)
    proj3 = proj.reshape(b, s, n_in)
    outs, lses = [], []
    for g, (window, dil) in enumerate(A_PATTERNS):
        c0 = g * 3 * A_HEADS
        o, lse = _dilated_attention(proj3, tabs[g], n_heads=A_HEADS, q_col=c0, k_col=c0 + A_HEADS,
                                    v_col=c0 + 2 * A_HEADS, dil=dil)
        outs.append(o.reshape(m, -1))
        lses.append(lse.reshape(m, -1))
    x = _outproj(outs, lses, w_out, slot, x, "lse3", _row_tile(m, 512), 512)
    return x, proj3, proj_s


def _mixer_b_prompt(x, xs, b, g_mix, li, w_in, w_out, slot, tabs, far):
    m = x.shape[0]
    s = m // b
    n_in = w_in.shape[2]
    proj, proj_s = _norm_matmul(x, g_mix, li, w_in, slot, n_in, _row_tile(m, PROMPT_ROW_TILE), 512, xs=xs)
    proj3 = proj.reshape(b, s, n_in)
    sel = _moba_select(proj3, n_kv=B_KV_HEADS, group=GQA_GROUP, k_col=B_HEADS)
    o = _flash2(proj3, tabs, far, sel, n_kv=B_KV_HEADS, group=GQA_GROUP, q_col=0, k_col=B_HEADS,
               v_col=B_HEADS + B_KV_HEADS, sel_blk=B_BLOCK, sel_per_head=True)
    x = _outproj([o.reshape(m, -1)], [], w_out, slot, x, "plain", _row_tile(m, 512), 512)
    return x, proj3, proj_s


def _cmp_weights(cmp_pe, cmp_w1, cmp_w2, slot):
    half = C_CMP_STRIDE * HEAD_DIM
    w1 = cmp_w1[slot]
    w1cat = jnp.concatenate([w1[:, :half], w1[:, half:]], axis=2).astype(BF16)
    pe8 = jnp.broadcast_to(cmp_pe[slot].reshape(2, 1, C_CMP_BLOCK * HEAD_DIM), (2, 8, C_CMP_BLOCK * HEAD_DIM))
    return w1cat, pe8, w1, cmp_w2[slot]


def _mixer_c_prompt(x, xs, b, g_mix, li, w_in, w_gate, w_out, slot, cmp_w, tabs, tabs_win, far, rel_bias):
    m = x.shape[0]
    s = m // b
    n_main = (C_HEADS + 6 * C_KV_HEADS) * HEAD_DIM
    proj, proj_s = _norm_matmul(x, g_mix, li, w_in, slot, n_main, _row_tile(m, PROMPT_ROW_TILE), 512, xs=xs)
    gate, gate_s = _norm_matmul(x, g_mix, li, w_gate, 0, HEAD_DIM, _row_tile(m, 512), HEAD_DIM, xs=xs)
    proj3 = proj.reshape(b, s, n_main)
    w1cat, pe8, w1, w2 = cmp_w
    y = _cmp1_prompt(proj3, w1cat, col0=C_HEADS)
    kvc = _cmp2(y, pe8, w1, w2)
    n_seg = kvc.shape[1]
    o_cmp, sel = _nsa_cmp_select(proj3, kvc, rel_bias, far, n_kv=C_KV_HEADS, group=GQA_GROUP)
    kv0 = C_HEADS
    o_slc = _flash2(proj3, tabs, far, sel, n_kv=C_KV_HEADS, group=GQA_GROUP, q_col=0, k_col=kv0 + 2 * C_KV_HEADS,
                   v_col=kv0 + 3 * C_KV_HEADS, sel_blk=C_SLC_BLOCK)
    o_win = _flash2(proj3, tabs_win, far, None, n_kv=C_KV_HEADS, group=GQA_GROUP, q_col=0, k_col=kv0 + 4 * C_KV_HEADS,
                   v_col=kv0 + 5 * C_KV_HEADS, window=C_WINDOW)
    x = _outproj([o_cmp.reshape(m, -1), o_slc.reshape(m, -1), o_win.reshape(m, -1)], [gate], w_out, slot, x,
                 "gate3", _row_tile(m, 512), 512)
    return x, proj3, proj_s, gate_s


def _a_decode_kernel(qkv_ref, buf0_ref, buf1_ref, buf2_ref, bias_ref, o_ref):
    outs, lses = [], []
    for g, buf in enumerate((buf0_ref, buf1_ref, buf2_ref)):
        q, k_new, v_new = qkv_ref[0, g, 0], qkv_ref[0, g, 1], qkv_ref[0, g, 2]
        kb, vb = buf[:, 0], buf[:, 1]
        n = kb.shape[0]
        s = jnp.sum(q[None] * kb, axis=-1, keepdims=True) * SOFTMAX_SCALE + bias_ref[g, 0:n]
        s_new = jnp.sum(q * k_new, axis=-1, keepdims=True) * SOFTMAX_SCALE + bias_ref[g, n]
        m = jnp.maximum(jnp.max(s, axis=0), s_new)
        e = jnp.exp(s - m[None])
        e_new = jnp.exp(s_new - m)
        l = jnp.sum(e, axis=0) + e_new
        outs.append((jnp.sum(e * vb, axis=0) + e_new * v_new) / l)
        lses.append(m + jnp.log(l))
    mx = jnp.maximum(jnp.maximum(lses[0], lses[1]), lses[2])
    w = [jnp.exp(l - mx) for l in lses]
    o_ref[0] = (w[0] * outs[0] + w[1] * outs[1] + w[2] * outs[2]) / (w[0] + w[1] + w[2])


def _a_decode(qkv, bufs, slot, bias):
    n_dec = qkv.shape[0]
    in_specs = [pl.BlockSpec((1, 3, 3, A_HEADS, HEAD_DIM), lambda i: (i, 0, 0, 0, 0))]
    views = []
    for (window, dil), buf in zip(A_PATTERNS, bufs):
        n = window // dil
        assert buf.shape[2] == window, "window buffers must hold exactly `window` past rows"
        views.append(buf.reshape(buf.shape[0], n_dec, n, dil, 2, A_HEADS, HEAD_DIM))
        in_specs.append(pl.BlockSpec((None, None, n, None, 2, A_HEADS, HEAD_DIM),
                                     lambda i: (slot, i, 0, 0, 0, 0, 0)))
    in_specs.append(pl.BlockSpec(bias.shape, lambda i: (0, 0, 0, 0)))
    return pl.pallas_call(
        _a_decode_kernel,
        grid=(n_dec,),
        in_specs=in_specs,
        out_specs=pl.BlockSpec((1, A_HEADS, HEAD_DIM), lambda i: (i, 0, 0)),
        out_shape=jax.ShapeDtypeStruct((n_dec, A_HEADS, HEAD_DIM), F32),
        compiler_params=_params(1),
        name="a_decode",
    )(qkv, *views, bias)


DECODE_CHUNK = 2048


def _head_rows_store(o_ref, o_all, l_row):
    l_col = jnp.broadcast_to(l_row, (HEAD_DIM, HEAD_DIM)).T
    for g in range(B_KV_HEADS):
        rows = slice(g * GQA_GROUP, (g + 1) * GQA_GROUP)
        o_ref[0, rows, :] = o_all[rows, g * HEAD_DIM:(g + 1) * HEAD_DIM] / l_col[rows, :]


def _paged_decode_kernel(pt_ref, *refs, n_pg, n_sub, k_sub, v_sub, blk, topk, n_pages):
    page_refs = refs[:n_pg]
    qbd_ref, new_ref, bias_ref = refs[n_pg:n_pg + 3]
    rest = refs[n_pg + 3:]
    if topk is None:
        sel_ref, o_ref, st_ref, vs_ref, sel_s = rest
    else:
        o_ref, st_ref, vs_ref, sel_s = rest
    j = pl.program_id(1)
    past = n_pages * PAGE_SIZE
    n_blk = past // blk
    ch = DECODE_CHUNK

    if topk is not None:
        @pl.when(j == 0)
        def _():
            sel_s[...] = jnp.zeros_like(sel_s)

    for p in range(n_pg):
        page = j * n_pg + p
        row0 = pl.multiple_of(page * PAGE_SIZE, PAGE_SIZE)
        s_acc = jnp.zeros((PAGE_SIZE, HEAD_DIM), F32)
        for g in range(B_KV_HEADS):
            kg = page_refs[p][pl.ds(k_sub + g, PAGE_SIZE, stride=n_sub), :].astype(BF16)
            s_acc = s_acc + _dot(kg, qbd_ref[0, g])
            vg = page_refs[p][pl.ds(v_sub + g, PAGE_SIZE, stride=n_sub), :].astype(BF16)
            vs_ref[pl.ds(row0, PAGE_SIZE), g * HEAD_DIM:(g + 1) * HEAD_DIM] = vg
        st_ref[pl.ds(row0, PAGE_SIZE), :] = s_acc
        if topk is not None:
            b_idx = page // (blk // PAGE_SIZE)
            sel_s[pl.ds(b_idx, 1), :] = sel_s[pl.ds(b_idx, 1), :] + jnp.sum(s_acc, axis=0, keepdims=True)

    @pl.when(j == pl.num_programs(1) - 1)
    def _():
        s_new = jnp.zeros((8, HEAD_DIM), F32)
        for g in range(B_KV_HEADS):
            k_new = jnp.broadcast_to(new_ref[0, g:g + 1, :], (8, HEAD_DIM)).astype(BF16)
            s_new = s_new + _dot(k_new, qbd_ref[0, g])
            v_new = jnp.broadcast_to(new_ref[0, B_KV_HEADS + g:B_KV_HEADS + g + 1, :], (8, HEAD_DIM))
            vs_ref[past:past + 8, g * HEAD_DIM:(g + 1) * HEAD_DIM] = v_new.astype(BF16)
        l_new = s_new[0:1] * SOFTMAX_SCALE + bias_ref[ch + 1:ch + 2, :]

        if topk is not None:
            gate = sel_s[...] * (1.0 / blk)
            rank = _rank_rows(gate, n_blk)
            sel_s[...] = jnp.where(rank < topk, 1.0, 0.0)
        else:
            sel_s[...] = sel_ref[0, 0:n_blk, :]

        def masked_logits(c, bias):
            start = pl.multiple_of(c * ch, ch)
            lg = st_ref[pl.ds(start, ch), :] * SOFTMAX_SCALE + bias
            b0 = pl.multiple_of(c * (ch // blk), 8)
            sl = sel_s[pl.ds(b0, ch // blk), :]
            mk = jnp.broadcast_to(sl[:, None, :], (ch // blk, blk, HEAD_DIM)).reshape(ch, HEAD_DIM) > 0.5
            return jnp.where(mk, lg, NEG_INF), mk, start

        n_ch = past // ch
        far = bias_ref[ch:ch + 1, :]
        tail = bias_ref[0:ch, :]

        def max_body(c, m):
            lg, _, _ = masked_logits(c, far)
            return jnp.maximum(m, jnp.max(lg, axis=0, keepdims=True))

        m = lax.fori_loop(0, n_ch - 1, max_body, l_new)
        lg_t, mk_t, start_t = masked_logits(n_ch - 1, tail)
        m = jnp.maximum(m, jnp.max(lg_t, axis=0, keepdims=True))

        def acc_chunk(lg, mk, start, carry):
            l_sum, o_all = carry
            e = jnp.where(mk, jnp.exp(lg - m), 0.0)
            o_all = o_all + _dot(e.astype(BF16), vs_ref[pl.ds(start, ch), :], _TN)
            return l_sum + jnp.sum(e, axis=0, keepdims=True), o_all

        def acc_body(c, carry):
            lg, mk, start = masked_logits(c, far)
            return acc_chunk(lg, mk, start, carry)

        e_new = jnp.exp(l_new - m)
        row8 = lax.broadcasted_iota(jnp.int32, (8, HEAD_DIM), 0)
        e_new8 = jnp.where(row8 == 0, jnp.broadcast_to(e_new, (8, HEAD_DIM)), 0.0)
        o0 = _dot(e_new8.astype(BF16), vs_ref[past:past + 8, :], _TN)
        carry = lax.fori_loop(0, n_ch - 1, acc_body, (e_new, o0))
        l_sum, o_all = acc_chunk(lg_t, mk_t, start_t, carry)
        _head_rows_store(o_ref, o_all, l_sum)


def _paged_decode(page_table, pool2d, qbd, new_kv, bias, sel, *, n_pg, n_sub, k_sub, v_sub, blk, topk):
    n_dec, n_pages = page_table.shape
    past = n_pages * PAGE_SIZE
    assert past % DECODE_CHUNK == 0 and n_pages % n_pg == 0 and past % blk == 0
    n_blk = past // blk
    page_rows = PAGE_SIZE * n_sub
    in_specs = [pl.BlockSpec((page_rows, HEAD_DIM), functools.partial(lambda n, j, pt, p: (pt[n, j * n_pg + p], 0), p=p))
                for p in range(n_pg)]
    in_specs += [pl.BlockSpec((1, B_KV_HEADS, HEAD_DIM, HEAD_DIM), lambda n, j, pt: (n, 0, 0, 0)),
                 pl.BlockSpec((1, 2 * B_KV_HEADS, HEAD_DIM), lambda n, j, pt: (n, 0, 0)),
                 pl.BlockSpec(bias.shape, lambda n, j, pt: (0, 0))]
    args = [pool2d] * n_pg + [qbd, new_kv, bias]
    if topk is None:
        in_specs.append(pl.BlockSpec((1,) + sel.shape[1:], lambda n, j, pt: (n, 0, 0)))
        args.append(sel)
    return pl.pallas_call(
        functools.partial(_paged_decode_kernel, n_pg=n_pg, n_sub=n_sub, k_sub=k_sub, v_sub=v_sub, blk=blk,
                          topk=topk, n_pages=n_pages),
        grid_spec=pltpu.PrefetchScalarGridSpec(
            num_scalar_prefetch=1,
            grid=(n_dec, n_pages // n_pg),
            in_specs=in_specs,
            out_specs=pl.BlockSpec((1, B_HEADS, HEAD_DIM), lambda n, j, pt: (n, 0, 0)),
            scratch_shapes=[pltpu.VMEM((past + 8, HEAD_DIM), F32),
                            pltpu.VMEM((past + 8, B_KV_HEADS * HEAD_DIM), BF16),
                            pltpu.VMEM((n_blk, HEAD_DIM), F32)],
        ),
        out_shape=jax.ShapeDtypeStruct((n_dec, B_HEADS, HEAD_DIM), F32),
        compiler_params=_params(2),
        name="paged_decode",
    )(page_table, *args)


def _block_diag_q(q):
    n, h, d = q.shape
    qt = jnp.pad(jnp.swapaxes(q, 1, 2), ((0, 0), (0, 0), (0, HEAD_DIM - h)))
    grp = (np.arange(HEAD_DIM)[None, :] // GQA_GROUP == np.arange(B_KV_HEADS)[:, None]) & (np.arange(HEAD_DIM)[None, :] < h)
    return jnp.where(jnp.asarray(grp)[None, :, None, :], qt[:, None], 0.0).astype(BF16)


def _decode_bias(rel_bias, past, n_tail):
    dist = np.concatenate([n_tail - np.arange(n_tail), [MAX_DISTANCE * 8, 0], np.zeros(6, np.int64)])
    tab = rel_bias[jnp.asarray(_bucket_np(dist))]
    return jnp.pad(tab, ((0, 0), (0, HEAD_DIM - tab.shape[1])))


def _cmp1_paged_kernel(pt_ref, *refs, n_pg):
    page_refs = refs[:n_pg]
    w_ref, y_ref, ys_ref = refs[n_pg:]
    segs_per_page = PAGE_SIZE // C_CMP_STRIDE
    rows = segs_per_page * 8
    x2 = jnp.concatenate(
        [jnp.concatenate([ref[:, s, :, :].reshape(rows, HEAD_DIM) for s in range(C_CMP_STRIDE)], axis=1)
         for ref in page_refs], axis=0)
    y4 = _dot(x2.astype(BF16), w_ref[...])
    is_v = (lax.broadcasted_iota(jnp.int32, (x2.shape[0], 2 * HEAD_DIM), 0) % 8) >= C_KV_HEADS
    y = jnp.where(is_v, y4[:, 2 * HEAD_DIM:], y4[:, :2 * HEAD_DIM])
    ys_ref[0] = y[:, :HEAD_DIM]
    ys_ref[1] = y[:, HEAD_DIM:]
    for sub in range(8):
        y_ref[sub] = jnp.concatenate(
            [ys_ref[half, pl.ds(sub, n_pg * segs_per_page, stride=8), :] for half in range(2)], axis=1)


def _cmp1_paged(page_table, pool4d, w1cat4, *, sub_blk, n_pg):
    n_dec, n_pages = page_table.shape
    segs_per_page = PAGE_SIZE // C_CMP_STRIDE
    n_seg = n_pages * segs_per_page
    in_specs = [pl.BlockSpec((segs_per_page, C_CMP_STRIDE, 8, HEAD_DIM),
                             functools.partial(lambda n, j, pt, p: (pt[n, j * n_pg + p], 0, sub_blk, 0), p=p))
                for p in range(n_pg)]
    in_specs.append(pl.BlockSpec(w1cat4.shape, lambda n, j, pt: (0, 0)))
    return pl.pallas_call(
        functools.partial(_cmp1_paged_kernel, n_pg=n_pg),
        grid_spec=pltpu.PrefetchScalarGridSpec(
            num_scalar_prefetch=1,
            grid=(n_dec, n_pages // n_pg),
            in_specs=in_specs,
            out_specs=pl.BlockSpec((8, n_pg * segs_per_page, 2 * HEAD_DIM), lambda n, j, pt: (n, j, 0)),
            scratch_shapes=[pltpu.VMEM((2, n_pg * segs_per_page * 8, HEAD_DIM), F32)],
        ),
        out_shape=jax.ShapeDtypeStruct((n_dec * 8, n_seg, 2 * HEAD_DIM), F32),
        compiler_params=_params(2),
        name="cmp1_paged",
    )(page_table, *([pool4d] * n_pg), w1cat4)


def _nsa_decode_kernel(kvc_ref, qbd_ref, bias_c_ref, map_ref, gsum_ref, win_ref, new_ref, bias_w_ref,
                       oc_ref, ow_ref, sel_ref, *, n_cmp, n_blk, n_sub_w, k_sub_w):
    n_seg = kvc_ref.shape[1]
    sc = jnp.zeros((n_seg, HEAD_DIM), F32)
    for g in range(C_KV_HEADS):
        sc = sc + _dot(kvc_ref[g].astype(BF16), qbd_ref[0, g])
    valid = lax.broadcasted_iota(jnp.int32, (n_seg, HEAD_DIM), 0) < n_cmp
    lg = jnp.where(valid, sc * SOFTMAX_SCALE + bias_c_ref[...], NEG_INF)
    m = jnp.max(lg, axis=0, keepdims=True)
    e = jnp.where(valid, jnp.exp(lg - m), 0.0)
    p = e / jnp.maximum(jnp.sum(e, axis=0, keepdims=True), 1e-30)
    vc = jnp.concatenate([kvc_ref[C_KV_HEADS + g].astype(BF16) for g in range(C_KV_HEADS)], axis=1)
    _head_rows_store(oc_ref, _dot(p.astype(BF16), vc, _TN), jnp.ones((1, HEAD_DIM), F32))
    imp_h = _dot_f32_rhs(map_ref[...], p)
    imp = _dot_f32_lhs(imp_h, gsum_ref[...])
    blk = lax.broadcasted_iota(jnp.int32, imp.shape, 0)
    cur = n_blk - 1
    avail = blk <= cur
    forced = (blk == 0) | (blk == cur) | (blk == cur - 1)
    score = jnp.where(forced, jnp.inf, imp)
    score = jnp.where(avail, score, -jnp.inf)
    rank = _rank_rows(score, n_blk)
    sel_ref[0] = jnp.where(avail & (rank < min(C_SLC_TOPK, n_blk)), 1.0, 0.0)
    n_win = bias_w_ref.shape[0] - 8
    sw = jnp.zeros((n_win, HEAD_DIM), F32)
    s_new = jnp.zeros((8, HEAD_DIM), F32)
    v_parts, vn_parts = [], []
    for g in range(C_KV_HEADS):
        kg = win_ref[pl.ds(k_sub_w + g, n_win, stride=n_sub_w), :].astype(BF16)
        sw = sw + _dot(kg, qbd_ref[0, g])
        v_parts.append(win_ref[pl.ds(k_sub_w + C_KV_HEADS + g, n_win, stride=n_sub_w), :].astype(BF16))
        k_new = jnp.broadcast_to(new_ref[0, g:g + 1, :], (8, HEAD_DIM)).astype(BF16)
        s_new = s_new + _dot(k_new, qbd_ref[0, g])
        vn_parts.append(jnp.broadcast_to(new_ref[0, C_KV_HEADS + g:C_KV_HEADS + g + 1, :], (8, HEAD_DIM)).astype(BF16))
    lw = sw * SOFTMAX_SCALE + bias_w_ref[0:n_win, :]
    l_new = s_new[0:1] * SOFTMAX_SCALE + bias_w_ref[n_win + 1:n_win + 2, :]
    mw = jnp.maximum(jnp.max(lw, axis=0, keepdims=True), l_new)
    ew = jnp.exp(lw - mw)
    e_new = jnp.exp(l_new - mw)
    row8 = lax.broadcasted_iota(jnp.int32, (8, HEAD_DIM), 0)
    e_new8 = jnp.where(row8 == 0, jnp.broadcast_to(e_new, (8, HEAD_DIM)), 0.0)
    o_all = _dot(ew.astype(BF16), jnp.concatenate(v_parts, axis=1), _TN)
    o_all = o_all + _dot(e_new8.astype(BF16), jnp.concatenate(vn_parts, axis=1), _TN)
    _head_rows_store(ow_ref, o_all, jnp.sum(ew, axis=0, keepdims=True) + e_new)


def _nsa_decode(kvc, qbd, bias_c, cmap_t, gsum, win2d, new_win, bias_w, *, slot, n_blk):
    n_dec = qbd.shape[0]
    n_seg = kvc.shape[1]
    n_win = bias_w.shape[0] - 8
    n_blk_pad = cmap_t.shape[0]
    win_rows = n_win * 8
    o_sds = jax.ShapeDtypeStruct((n_dec, C_HEADS, HEAD_DIM), F32)
    o_spec = pl.BlockSpec((1, C_HEADS, HEAD_DIM), lambda n: (n, 0, 0))
    return pl.pallas_call(
        functools.partial(_nsa_decode_kernel, n_cmp=n_seg - 1, n_blk=n_blk, n_sub_w=8, k_sub_w=0),
        grid=(n_dec,),
        in_specs=[pl.BlockSpec((8, n_seg, HEAD_DIM), lambda n: (n, 0, 0)),
                  pl.BlockSpec((1, C_KV_HEADS, HEAD_DIM, HEAD_DIM), lambda n: (n, 0, 0, 0)),
                  pl.BlockSpec(bias_c.shape, lambda n: (0, 0)),
                  pl.BlockSpec(cmap_t.shape, lambda n: (0, 0)),
                  pl.BlockSpec(gsum.shape, lambda n: (0, 0)),
                  pl.BlockSpec((win_rows, HEAD_DIM), lambda n: (slot * n_dec + n, 0)),
                  pl.BlockSpec((1, 2 * C_KV_HEADS, HEAD_DIM), lambda n: (n, 0, 0)),
                  pl.BlockSpec(bias_w.shape, lambda n: (0, 0))],
        out_specs=[o_spec, o_spec, pl.BlockSpec((1, n_blk_pad, HEAD_DIM), lambda n: (n, 0, 0))],
        out_shape=[o_sds, o_sds, jax.ShapeDtypeStruct((n_dec, n_blk_pad, HEAD_DIM), F32)],
        compiler_params=_params(1),
        name="nsa_decode",
    )(kvc, qbd, bias_c, cmap_t, gsum, win2d, new_win, bias_w)


def _mixer_a_decode(x, proj, w_out, slot, bufs, rel_bias):
    n_dec = x.shape[0]
    qkv = proj.reshape(n_dec, len(A_PATTERNS), 3, A_HEADS, HEAD_DIM)
    dist = np.stack([np.concatenate([(w // d - np.arange(w // d)) * d, [0]]) for w, d in A_PATTERNS])
    bias = _bias_by_dist(rel_bias[:, :A_HEADS], dist)
    bias = jnp.broadcast_to(jnp.transpose(bias, (1, 2, 0))[..., None], dist.shape + (A_HEADS, HEAD_DIM))
    o = _a_decode(qkv, bufs, slot, bias)
    return _outproj([o.reshape(n_dec, -1)], [], w_out, slot, x, "plain", n_dec, 512)


def _mixer_b_decode(x, proj, w_out, slot, pool, page_table, bias_dec):
    n_dec = x.shape[0]
    nq = B_HEADS * HEAD_DIM
    qbd = _block_diag_q(proj[:, :nq].reshape(n_dec, B_HEADS, HEAD_DIM))
    new_kv = proj[:, nq:].reshape(n_dec, 2 * B_KV_HEADS, HEAD_DIM)
    n_sub = pool.shape[2] * 2 * B_KV_HEADS
    o = _paged_decode(page_table, pool.reshape(-1, HEAD_DIM), qbd, new_kv, bias_dec, None, n_pg=8, n_sub=n_sub,
                      k_sub=slot * 2 * B_KV_HEADS, v_sub=slot * 2 * B_KV_HEADS + B_KV_HEADS, blk=B_BLOCK, topk=B_TOPK)
    return _outproj([o.reshape(n_dec, -1)], [], w_out, slot, x, "plain", n_dec, 512)


def _mixer_c_decode(x, proj, gate, w_out, slot, cmp_w, pool, win_buf, page_table, bias_dec, rel_bias):
    n_dec = x.shape[0]
    n_pages = page_table.shape[1]
    past = n_pages * PAGE_SIZE
    assert win_buf.shape[2] == C_WINDOW, "window buffer must hold exactly C_WINDOW past rows"
    nq = C_HEADS * HEAD_DIM
    kvw = C_KV_HEADS * HEAD_DIM
    qbd = _block_diag_q(proj[:, :nq].reshape(n_dec, C_HEADS, HEAD_DIM))
    new_slc = proj[:, nq + 2 * kvw:nq + 4 * kvw].reshape(n_dec, 2 * C_KV_HEADS, HEAD_DIM)
    new_win = proj[:, nq + 4 * kvw:nq + 6 * kvw].reshape(n_dec, 2 * C_KV_HEADS, HEAD_DIM)
    w1cat, pe8, w1, w2 = cmp_w
    w1cat4 = jnp.concatenate([w1cat[0], w1cat[1]], axis=1)
    n_layers = pool.shape[2]
    n_sub = n_layers * 4 * C_KV_HEADS
    pool4d = pool.reshape(-1, C_CMP_STRIDE, n_sub, HEAD_DIM)
    y = _cmp1_paged(page_table, pool4d, w1cat4, sub_blk=slot * 2, n_pg=8)
    kvc = _cmp2(y, pe8, w1, w2)
    n_seg = kvc.shape[1]
    n_cmp = n_seg - 1
    n_blk = past // C_SLC_BLOCK + 1
    n_blk_pad = -(-n_blk // 8) * 8
    dist_c = past - (np.arange(n_seg) * C_CMP_STRIDE + (C_CMP_BLOCK - 1))
    bias_c = jnp.pad(rel_bias[jnp.asarray(_bucket_np(dist_c))], ((0, 0), (0, HEAD_DIM - C_HEADS)))
    cmap_t = jnp.asarray(_cmp_map_np(n_seg, n_cmp, n_blk_pad).T, BF16)
    lanes = np.arange(HEAD_DIM)
    gsum = jnp.asarray((lanes[:, None] // GQA_GROUP == lanes[None, :] // GQA_GROUP)
                       & (lanes[:, None] < C_HEADS) & (lanes[None, :] < C_HEADS), BF16)
    bias_w = _decode_bias(rel_bias, past, C_WINDOW)
    o_cmp, o_win, sel = _nsa_decode(kvc, qbd, bias_c, cmap_t, gsum, win_buf.reshape(-1, HEAD_DIM), new_win, bias_w,
                                    slot=slot, n_blk=n_blk)
    o_slc = _paged_decode(page_table, pool.reshape(-1, HEAD_DIM), qbd, new_slc, bias_dec, sel, n_pg=4, n_sub=n_sub,
                          k_sub=slot * 16 + 2 * C_KV_HEADS, v_sub=slot * 16 + 3 * C_KV_HEADS, blk=C_SLC_BLOCK, topk=None)
    return _outproj([o_cmp.reshape(n_dec, -1), o_slc.reshape(n_dec, -1), o_win.reshape(n_dec, -1)], [gate], w_out, slot,
                    x, "gate3", n_dec, 512)


SHIFT_CHUNK_ROWS = 8192


def _shift_kernel(cur_ref, nxt_ref, new_ref, out_ref, *, row):
    r = cur_ref.shape[1]
    out_ref[0, 0:r - row, :] = cur_ref[0, row:r, :]
    last = pl.program_id(1) == pl.num_programs(1) - 1

    @pl.when(last)
    def _():
        out_ref[0, r - row:r, :] = new_ref[0]

    @pl.when(jnp.logical_not(last))
    def _():
        out_ref[0, r - row:r, :] = nxt_ref[0]


def _shifted(buf, new_rows):
    n_l, n_seq, w = buf.shape[:3]
    row = math.prod(buf.shape[3:]) // HEAD_DIM
    assert row % 8 == 0
    total = w * row
    r = min(SHIFT_CHUNK_ROWS, total)
    assert total % r == 0 and r % row == 0
    n_chunks = total // r
    per_chunk = r // row
    out = pl.pallas_call(
        functools.partial(_shift_kernel, row=row),
        grid=(n_l * n_seq, n_chunks),
        in_specs=[pl.BlockSpec((1, r, HEAD_DIM), lambda i, c: (i, c, 0)),
                  pl.BlockSpec((1, row, HEAD_DIM), lambda i, c: (i, jnp.minimum((c + 1) * per_chunk, w - 1), 0)),
                  pl.BlockSpec((1, row, HEAD_DIM), lambda i, c: (i, 0, 0))],
        out_specs=pl.BlockSpec((1, r, HEAD_DIM), lambda i, c: (i, c, 0)),
        out_shape=jax.ShapeDtypeStruct((n_l * n_seq, total, HEAD_DIM), buf.dtype),
        compiler_params=_params(2),
        name="shift_window",
    )(buf.reshape(n_l * n_seq, total, HEAD_DIM), buf.reshape(n_l * n_seq, total, HEAD_DIM),
      new_rows.reshape(n_l * n_seq, row, HEAD_DIM))
    return out.reshape(buf.shape)


def kernel(x_prompt, x_sample, state_a_w128, state_a_w512, state_a_w2048, cache_b_kv, cache_c_kv, state_c_win, page_table, rel_bias, norm_mix, norm_ffn, norm_final, w_in_a, w_out_a, w_in_b, w_out_b, w_in_c, w_out_c, cmp_pe, cmp_w1, cmp_w2, w_up, w_down):
    b, s, d = x_prompt.shape
    n_dec, t_dec, _ = x_sample.shape
    depth = norm_mix.shape[0]
    assert t_dec == 1 and s % B_BLOCK == 0 and s % ATTN_TILE == 0
    past = page_table.shape[1] * PAGE_SIZE
    assert past % B_BLOCK == 0 and past % C_SLC_BLOCK == 0

    xp = x_prompt.reshape(b * s, d)
    xs = x_sample.reshape(n_dec, d)
    g_mix = norm_mix.reshape(depth, 1, d)
    g_ffn = norm_ffn.reshape(depth, 1, d)
    g_fin = norm_final.reshape(1, d)
    tabs = _tile_tables(rel_bias, ATTN_TILE)
    tabs_win = _tile_tables(rel_bias, ATTN_TILE, window=C_WINDOW)
    tabs_a = [_tile_tables(rel_bias[:, :A_HEADS], ATTN_TILE, window=w // dl, dist_scale=dl) for w, dl in A_PATTERNS]
    far = rel_bias[NUM_BUCKETS - 1]
    bias_dec = _decode_bias(rel_bias, past, DECODE_CHUNK)
    n_gate = 3 * C_HEADS
    n_main_c = (C_HEADS + 6 * C_KV_HEADS) * HEAD_DIM
    w_gate = jnp.pad(w_in_c[:, :, n_main_c:], ((0, 0), (0, 0), (0, HEAD_DIM - n_gate)))
    a_bufs = (state_a_w128, state_a_w512, state_a_w2048)
    w_in_a, w_in_b, w_in_c = w_in_a.astype(BF16), w_in_b.astype(BF16), w_in_c.astype(BF16)

    a_p, a_s = [[], [], []], [[], [], []]
    b_p, b_s, c_p, c_s, cw_p, cw_s = [], [], [], [], [], []
    for li in range(depth):
        kind, slot = li % 3, li // 3
        if kind == 0:
            xp, pp, ps = _mixer_a_prompt(xp, xs, b, g_mix, li, w_in_a, w_out_a, slot, tabs_a, far)
            xs = _mixer_a_decode(xs, ps, w_out_a, slot, a_bufs, rel_bias)
            gw = 3 * A_HEADS * HEAD_DIM
            kvw = 2 * A_HEADS * HEAD_DIM
            for g, (window, _) in enumerate(A_PATTERNS):
                c0 = g * gw + A_HEADS * HEAD_DIM
                a_p[g].append(pp[:, s - min(window, s):, c0:c0 + kvw].reshape(b, min(window, s), 2, A_HEADS, HEAD_DIM))
                a_s[g].append(ps[:, c0:c0 + kvw].reshape(n_dec, 2, A_HEADS, HEAD_DIM))
        elif kind == 1:
            xp, pp, ps = _mixer_b_prompt(xp, xs, b, g_mix, li, w_in_b, w_out_b, slot, tabs, far)
            xs = _mixer_b_decode(xs, ps, w_out_b, slot, cache_b_kv, page_table, bias_dec)
            nq = B_HEADS * HEAD_DIM
            b_p.append(pp[:, :, nq:].reshape(b, s, 2, B_KV_HEADS, HEAD_DIM))
            b_s.append(ps[:, nq:].reshape(n_dec, 1, 2, B_KV_HEADS, HEAD_DIM))
        else:
            cmp_w = _cmp_weights(cmp_pe, cmp_w1, cmp_w2, slot)
            xp, pp, ps, gate_s = _mixer_c_prompt(xp, xs, b, g_mix, li, w_in_c, w_gate[slot:slot + 1], w_out_c, slot, cmp_w,
                                                 tabs, tabs_win, far, rel_bias)
            xs = _mixer_c_decode(xs, ps, gate_s, w_out_c, slot, cmp_w, cache_c_kv, state_c_win, page_table, bias_dec,
                                 rel_bias)
            nq = C_HEADS * HEAD_DIM
            kvw = C_KV_HEADS * HEAD_DIM
            c_p.append(pp[:, :, nq:nq + 4 * kvw].reshape(b, s, 4, C_KV_HEADS, HEAD_DIM))
            c_s.append(ps[:, nq:nq + 4 * kvw].reshape(n_dec, 1, 4, C_KV_HEADS, HEAD_DIM))
            wlen = min(C_WINDOW, s)
            cw_p.append(pp[:, s - wlen:, nq + 4 * kvw:nq + 6 * kvw].reshape(b, wlen, 2, C_KV_HEADS, HEAD_DIM))
            cw_s.append(ps[:, nq + 4 * kvw:nq + 6 * kvw].reshape(n_dec, 2, C_KV_HEADS, HEAD_DIM))
        last = li == depth - 1
        xp, xs = _mlp(xp, xs, g_ffn, li, w_up, w_down, g_fin if last else None, _row_tile(b * s, PROMPT_ROW_TILE), 512)

    outs = [xp.reshape(b, s, d), xs.reshape(n_dec, 1, d)]
    for g in range(len(A_PATTERNS)):
        outs.append(jnp.stack(a_p[g], axis=0))
        outs.append(_shifted(a_bufs[g], jnp.stack(a_s[g], axis=0)))
    outs += [jnp.stack(b_p, axis=2), jnp.stack(b_s, axis=2), jnp.stack(c_p, axis=2), jnp.stack(c_s, axis=2),
             jnp.stack(cw_p, axis=0), _shifted(state_c_win, jnp.stack(cw_s, axis=0))]
    return tuple(outs)
```
